```python
import jax, jax.numpy as jnp
from jax import lax
import numpy as np

D_MODEL = 1024
BATCH = 2
SEQ = 8192
DEPTH = 4
DEC_BATCH = 128
DEC_SEQ = 1
PAST_LEN = 8192
PAGE_SIZE = 128

HEAD_DIM = 64
A_HEADS = (D_MODEL // 2) // HEAD_DIM
A_KV_HEADS = 2
A_GROUP = A_HEADS // A_KV_HEADS
WINDOW = 128
ROPE_THETA = 10000.0
A_Q = A_HEADS * HEAD_DIM
A_KV = A_KV_HEADS * HEAD_DIM
A_PROJ = A_Q + 2 * A_KV
B_HEAD_DIM = 64
B_WIDTH = D_MODEL // 2
B_HEADS = B_WIDTH // B_HEAD_DIM
DECAY_RANK = 64
ICLR_RANK = 64
GATE_RANK = 128
B_PROJ = 3 * B_WIDTH + DECAY_RANK + ICLR_RANK + GATE_RANK
DECAY_SCALE = 0.606531
GN_EPS = 64e-5
EVEN_PROJ = A_PROJ + B_PROJ
EVEN_OUT = A_Q + B_WIDTH
CHUNK = 128
C_WIDTH = D_MODEL
C_HEADS = 8
C_HEAD_DIM = C_WIDTH // C_HEADS
D_FF = 2816
CONV_W = 3
N_EVEN = (DEPTH + 1) // 2
N_ODD = DEPTH // 2
RMS_EPS = 1e-6
LN_EPS = 1e-5
NEG = -1e30

kernel_name = "hybrid_swa_rwkv7_gmlp_convffn_step"


def rmsnorm(x, g):
    xf = x.astype(jnp.float32)
    y = xf * lax.rsqrt(jnp.mean(xf * xf, axis=-1, keepdims=True) + RMS_EPS)
    return (y * g.astype(jnp.float32)).astype(x.dtype)


def layernorm(x, g, b):
    xf = x.astype(jnp.float32)
    mu = jnp.mean(xf, axis=-1, keepdims=True)
    var = jnp.mean(jnp.square(xf - mu), axis=-1, keepdims=True)
    return ((xf - mu) * lax.rsqrt(var + LN_EPS) * g.astype(jnp.float32) + b.astype(jnp.float32)).astype(x.dtype)


def rope(x, pos):
    half = x.shape[-1] // 2
    inv = ROPE_THETA ** (-jnp.arange(half, dtype=jnp.float32) / half)
    ang = pos.astype(jnp.float32)[:, None] * inv[None, :]
    shape = (1, pos.shape[0]) + (1,) * (x.ndim - 3) + (half,)
    cos, sin = jnp.cos(ang).reshape(shape), jnp.sin(ang).reshape(shape)
    xf = x.astype(jnp.float32)
    x1, x2 = xf[..., :half], xf[..., half:]
    return jnp.concatenate([x1 * cos - x2 * sin, x2 * cos + x1 * sin], axis=-1).astype(x.dtype)


def sink_attention(q, k, v, mask, sinks):
    s = jnp.einsum('...qkgd,...skd->...kgqs', q, k).astype(jnp.float32) * (HEAD_DIM ** -0.5)
    s = jnp.where(mask, s, NEG)
    sink = sinks.astype(jnp.float32).reshape(A_KV_HEADS, A_GROUP, 1, 1)
    m = jnp.maximum(jnp.max(s, axis=-1, keepdims=True), sink)
    p = jnp.exp(s - m)
    p = p / (jnp.sum(p, axis=-1, keepdims=True) + jnp.exp(sink - m))
    return jnp.einsum('...kgqs,...skd->...qkgd', p.astype(v.dtype), v)


def banded_window_attention(q, k, v, sinks):
    n, t = q.shape[:2]
    nb = t // WINDOW
    qb = q.reshape(n, nb, WINDOW, A_KV_HEADS, A_GROUP, HEAD_DIM)
    kb = k.reshape(n, nb, WINDOW, A_KV_HEADS, HEAD_DIM)
    vb = v.reshape(n, nb, WINDOW, A_KV_HEADS, HEAD_DIM)
    shift = lambda u: jnp.concatenate([jnp.zeros_like(u[:, :1]), u[:, :-1]], axis=1)
    kk = jnp.concatenate([shift(kb), kb], axis=2)
    vv = jnp.concatenate([shift(vb), vb], axis=2)
    qi = jnp.arange(WINDOW)[:, None]
    kj = jnp.arange(2 * WINDOW)[None, :]
    diff = WINDOW + qi - kj
    blk = jnp.arange(nb)[:, None, None]
    mask = (diff >= 0) & (diff <= WINDOW) & ((blk > 0) | (kj >= WINDOW))
    o = sink_attention(qb, kk, vv, mask[:, None, None], sinks)
    return o.reshape(n, t, A_KV_HEADS, A_GROUP, HEAD_DIM)


def cached_window_attention(q, k, v, k_buf, v_buf, sinks):
    t = q.shape[1]
    wb = k_buf.shape[1]
    kk = jnp.concatenate([k_buf.astype(k.dtype), k], axis=1)
    vv = jnp.concatenate([v_buf.astype(v.dtype), v], axis=1)
    q_pos = PAST_LEN + jnp.arange(t)
    k_pos = jnp.concatenate([PAST_LEN - wb + jnp.arange(wb), q_pos])
    diff = q_pos[:, None] - k_pos[None, :]
    mask = (diff >= 0) & (diff <= WINDOW)
    return sink_attention(q, kk, vv, mask, sinks)


def rwkv7_mix(zb, shift, wkv, mu, w0, w2, a0, a2, g2, k_k, k_a, r_k, gn_g, gn_b):
    f32 = jnp.float32
    n, t, _ = zb.shape
    prev = jnp.concatenate([shift[:, None].astype(zb.dtype), zb[:, :-1]], axis=1)
    zs = zb + (prev - zb) * mu
    o1, o2, o3 = B_WIDTH, 2 * B_WIDTH, 3 * B_WIDTH
    o4, o5 = o3 + DECAY_RANK, o3 + DECAY_RANK + ICLR_RANK
    r, k, v = zs[..., :o1], zs[..., o1:o2], zs[..., o2:o3]
    wd, ad, gd = zs[..., o3:o4], zs[..., o4:o5], zs[..., o5:]
    decay = jnp.exp(-DECAY_SCALE * jax.nn.sigmoid((w0 + jnp.tanh(wd) @ w2).astype(f32)))
    a = jax.nn.sigmoid((a0 + ad @ a2).astype(f32))
    g = jax.nn.sigmoid(gd) @ g2
    heads = lambda u: u.astype(f32).reshape(n, t, B_HEADS, B_HEAD_DIM)
    kk = heads(k * k_k)
    kk = kk / jnp.maximum(jnp.sqrt(jnp.sum(kk * kk, axis=-1, keepdims=True)), 1e-12)
    k = heads(k.astype(f32) * (1.0 + (a - 1.0) * k_a.astype(f32)))
    r, v, decay, a = heads(r), heads(v), heads(decay), heads(a)

    def step(s, inp):
        r_t, w_t, k_t, v_t, kk_t, a_t = inp
        sa = jnp.einsum('bhij,bhj->bhi', s, -kk_t)
        s = (s * w_t[:, :, None, :] + sa[..., None] * (kk_t * a_t)[:, :, None, :]
             + v_t[..., None] * k_t[:, :, None, :])
        return s, jnp.einsum('bhij,bhj->bhi', s, r_t)

    xs = tuple(jnp.moveaxis(u, 1, 0) for u in (r, decay, k, v, kk, a))
    s_last, o = lax.scan(step, wkv.astype(f32), xs)
    o = jnp.moveaxis(o, 0, 1)
    mean = jnp.mean(o, axis=-1, keepdims=True)
    var = jnp.mean(jnp.square(o - mean), axis=-1, keepdims=True)
    o = ((o - mean) * lax.rsqrt(var + GN_EPS)).reshape(n, t, B_WIDTH) * gn_g.astype(f32) + gn_b.astype(f32)
    bonus = jnp.sum(r * k * r_k.astype(f32), axis=-1, keepdims=True) * v
    o = (o + bonus.reshape(n, t, B_WIDTH)) * g.astype(f32)
    return o.astype(zb.dtype), zb[:, -1], s_last.astype(wkv.dtype)


def even_mix(xn, pos, k_buf, v_buf, shift, wkv, w_in, sinks, mu, w0, w2, a0, a2, g2, k_k, k_a, r_k, gn_g, gn_b, w_out):
    n, t, _ = xn.shape
    z = xn @ w_in
    q = z[..., :A_Q].reshape(n, t, A_KV_HEADS, A_GROUP, HEAD_DIM)
    k = z[..., A_Q:A_Q + A_KV].reshape(n, t, A_KV_HEADS, HEAD_DIM)
    v = z[..., A_Q + A_KV:A_PROJ].reshape(n, t, A_KV_HEADS, HEAD_DIM)
    q, k = rope(q, pos), rope(k, pos)
    if k_buf is None:
        o_a = banded_window_attention(q, k, v, sinks)
    else:
        o_a = cached_window_attention(q, k, v, k_buf, v_buf, sinks)
    o_b, new_shift, new_wkv = rwkv7_mix(z[..., A_PROJ:], shift, wkv, mu, w0, w2, a0, a2, g2, k_k, k_a, r_k, gn_g, gn_b)
    o = jnp.concatenate([o_a.reshape(n, t, A_Q), o_b], axis=-1) @ w_out
    return o, k, v, new_shift, new_wkv


def chunk_gmlp_mix(xn, w_in, ln_g, ln_b, w_s, b_s, w_out):
    n, t, _ = xn.shape
    z = jax.nn.gelu(xn @ w_in, approximate=False)
    u, v = z[..., :C_WIDTH], z[..., C_WIDTH:]
    v = layernorm(v, ln_g, ln_b)
    L = min(t, CHUNK)
    vc = v.reshape(n, t // L, L, C_HEADS, C_HEAD_DIM)
    ws = jnp.tril(w_s[:, :L, :L])
    mixed = jnp.einsum('hts,bnshc->bnthc', ws, vc) + b_s[:, :L].T[None, None, :, :, None]
    y = u * mixed.reshape(n, t, C_WIDTH)
    return y @ w_out, v


def conv_ffn(xn, conv_state, w_gate, w_up, conv_w, conv_b, w_down):
    t = xn.shape[1]
    gpre = xn @ w_gate
    ext = jnp.concatenate([conv_state.astype(gpre.dtype), gpre], axis=1)
    conv = conv_b + conv_w[CONV_W - 1] * ext[:, CONV_W - 1:CONV_W - 1 + t]
    for j in range(CONV_W - 1):
        conv = conv + conv_w[j] * ext[:, j:j + t]
    h = jax.nn.gelu(conv, approximate=True) * (xn @ w_up)
    return h @ w_down, ext[:, -(CONV_W - 1):]


def setup_inputs(seed: int = 0) -> dict:
    key = jax.random.key(seed)
    ks = iter(jax.random.split(key, 40))
    nrm = lambda shape, scale=1.0: scale * jax.random.normal(next(ks), shape, jnp.float32)
    unif = lambda shape, lo, hi: jax.random.uniform(next(ks), shape, jnp.float32, lo, hi)
    w_buf = min(WINDOW, PAST_LEN)
    return {
        "x_prompt": nrm((BATCH, SEQ, D_MODEL)),
        "x_sample": nrm((DEC_BATCH, DEC_SEQ, D_MODEL)),
        "cache_win_k": nrm((N_EVEN, DEC_BATCH, w_buf, A_KV_HEADS, HEAD_DIM)),
        "cache_win_v": nrm((N_EVEN, DEC_BATCH, w_buf, A_KV_HEADS, HEAD_DIM)),
        "state_wkv": nrm((N_EVEN, DEC_BATCH, B_HEADS, B_HEAD_DIM, B_HEAD_DIM), 0.5),
        "state_shift": nrm((N_EVEN, DEC_BATCH, B_PROJ)),
        "state_ffn_conv": nrm((DEPTH, DEC_BATCH, CONV_W - 1, D_FF)),
        "norm_mix_pre": 1.0 + nrm((DEPTH, D_MODEL), 0.05),
        "norm_mix_post": 1.0 + nrm((DEPTH, D_MODEL), 0.05),
        "norm_ffn_pre": 1.0 + nrm((DEPTH, D_MODEL), 0.05),
        "norm_ffn_post": 1.0 + nrm((DEPTH, D_MODEL), 0.05),
        "w_in_even": nrm((N_EVEN, D_MODEL, EVEN_PROJ), D_MODEL ** -0.5),
        "attn_sinks": nrm((N_EVEN, A_HEADS), 0.5),
        "shift_mu": unif((N_EVEN, B_PROJ), 0.0, 1.0),
        "decay_w0": unif((N_EVEN, B_WIDTH), -4.0, 1.0),
        "decay_w2": nrm((N_EVEN, DECAY_RANK, B_WIDTH), 0.1),
        "iclr_a0": nrm((N_EVEN, B_WIDTH), 0.1),
        "iclr_a2": nrm((N_EVEN, ICLR_RANK, B_WIDTH), 0.5 * ICLR_RANK ** -0.5),
        "gate_g2": nrm((N_EVEN, GATE_RANK, B_WIDTH), 2.0 * GATE_RANK ** -0.5),
        "key_k": 1.0 + nrm((N_EVEN, B_WIDTH), 0.1),
        "key_a": 1.0 + nrm((N_EVEN, B_WIDTH), 0.1),
        "bonus_r_k": nrm((N_EVEN, B_HEADS, B_HEAD_DIM), 0.1),
        "gn_gain": 1.0 + nrm((N_EVEN, B_WIDTH), 0.05),
        "gn_bias": nrm((N_EVEN, B_WIDTH), 0.02),
        "w_out_even": nrm((N_EVEN, EVEN_OUT, D_MODEL), EVEN_OUT ** -0.5),
        "w_in_odd": nrm((N_ODD, D_MODEL, 2 * C_WIDTH), D_MODEL ** -0.5),
        "sgu_ln_gain": 1.0 + nrm((N_ODD, C_WIDTH), 0.05),
        "sgu_ln_bias": nrm((N_ODD, C_WIDTH), 0.02),
        "sgu_w": nrm((N_ODD, C_HEADS, CHUNK, CHUNK), CHUNK ** -0.5),
        "sgu_b": 1.0 + nrm((N_ODD, C_HEADS, CHUNK), 0.1),
        "w_out_odd": nrm((N_ODD, C_WIDTH, D_MODEL), C_WIDTH ** -0.5),
        "ffn_w_gate": nrm((DEPTH, D_MODEL, D_FF), D_MODEL ** -0.5),
        "ffn_w_up": nrm((DEPTH, D_MODEL, D_FF), D_MODEL ** -0.5),
        "ffn_conv_w": nrm((DEPTH, CONV_W, D_FF), CONV_W ** -0.5),
        "ffn_conv_b": nrm((DEPTH, D_FF), 0.02),
        "ffn_w_down": nrm((DEPTH, D_FF, D_MODEL), D_FF ** -0.5),
    }


def reference(x_prompt, x_sample, cache_win_k, cache_win_v, state_wkv, state_shift, state_ffn_conv,
              norm_mix_pre, norm_mix_post, norm_ffn_pre, norm_ffn_post,
              w_in_even, attn_sinks, shift_mu, decay_w0, decay_w2, iclr_a0, iclr_a2, gate_g2,
              key_k, key_a, bonus_r_k, gn_gain, gn_bias, w_out_even,
              w_in_odd, sgu_ln_gain, sgu_ln_bias, sgu_w, sgu_b, w_out_odd,
              ffn_w_gate, ffn_w_up, ffn_conv_w, ffn_conv_b, ffn_w_down):
    pos_p = jnp.arange(SEQ, dtype=jnp.int32)
    pos_s = PAST_LEN + jnp.arange(DEC_SEQ, dtype=jnp.int32)
    hp, hs = x_prompt, x_sample
    kp_l, vp_l, ks_l, vs_l = [], [], [], []
    sp_l, ss_l, shp_l, shs_l = [], [], [], []
    vsgu_l, cp_l, cs_l = [], [], []
    for layer in range(DEPTH):
        j = layer // 2
        xp = rmsnorm(hp, norm_mix_pre[layer])
        xs = rmsnorm(hs, norm_mix_pre[layer])
        if layer % 2 == 0:
            ep = (w_in_even[j], attn_sinks[j], shift_mu[j], decay_w0[j], decay_w2[j], iclr_a0[j], iclr_a2[j],
                  gate_g2[j], key_k[j], key_a[j], bonus_r_k[j], gn_gain[j], gn_bias[j], w_out_even[j])
            shift0 = jnp.zeros((BATCH, B_PROJ), hp.dtype)
            wkv0 = jnp.zeros((BATCH, B_HEADS, B_HEAD_DIM, B_HEAD_DIM), hp.dtype)
            mp, kp, vp, shp, sp = even_mix(xp, pos_p, None, None, shift0, wkv0, *ep)
            ms, kn, vn, shs, sn = even_mix(xs, pos_s, cache_win_k[j], cache_win_v[j], state_shift[j], state_wkv[j], *ep)
            wp = min(WINDOW, SEQ)
            kp_l.append(kp[:, -wp:]); vp_l.append(vp[:, -wp:])
            ks_l.append(kn); vs_l.append(vn)
            sp_l.append(sp); ss_l.append(sn)
            shp_l.append(shp); shs_l.append(shs)
        else:
            op = (w_in_odd[j], sgu_ln_gain[j], sgu_ln_bias[j], sgu_w[j], sgu_b[j], w_out_odd[j])
            mp, _ = chunk_gmlp_mix(xp, *op)
            ms, vsg = chunk_gmlp_mix(xs, *op)
            vsgu_l.append(vsg)
        hp = hp + rmsnorm(mp, norm_mix_post[layer])
        hs = hs + rmsnorm(ms, norm_mix_post[layer])
        fprm = (ffn_w_gate[layer], ffn_w_up[layer], ffn_conv_w[layer], ffn_conv_b[layer], ffn_w_down[layer])
        fp, cp = conv_ffn(rmsnorm(hp, norm_ffn_pre[layer]), jnp.zeros((BATCH, CONV_W - 1, D_FF), hp.dtype), *fprm)
        fs, cs = conv_ffn(rmsnorm(hs, norm_ffn_pre[layer]), state_ffn_conv[layer], *fprm)
        hp = hp + rmsnorm(fp, norm_ffn_post[layer])
        hs = hs + rmsnorm(fs, norm_ffn_post[layer])
        cp_l.append(cp); cs_l.append(cs)
    return (hp, hs,
            jnp.stack(kp_l), jnp.stack(vp_l), jnp.stack(ks_l), jnp.stack(vs_l),
            jnp.stack(sp_l), jnp.stack(ss_l), jnp.stack(shp_l), jnp.stack(shs_l),
            jnp.stack(vsgu_l), jnp.stack(cp_l), jnp.stack(cs_l))
```

```python
import functools

import jax
import jax.numpy as jnp
from jax import lax
from jax.experimental import pallas as pl
from jax.experimental.pallas import tpu as pltpu

F32 = jnp.float32
BF16 = jnp.bfloat16
HIGHEST = lax.Precision.HIGHEST

PAST_LEN = 8192
HEAD_DIM = 64
A_HEADS = 8
A_KV_HEADS = 2
A_GROUP = A_HEADS // A_KV_HEADS
WINDOW = 128
ROPE_THETA = 10000.0
B_HEAD_DIM = 64
B_HEADS = 8
B_WIDTH = B_HEADS * B_HEAD_DIM
DECAY_RANK = 64
ICLR_RANK = 64
DECAY_SCALE = 0.606531
GN_EPS = 64e-5
CHUNK = 128
C_HEADS = 8
RMS_EPS = 1e-6
LN_EPS = 1e-5
NEG = -1e30
A_Q = A_HEADS * HEAD_DIM
A_KV = A_KV_HEADS * HEAD_DIM
A_PROJ = A_Q + 2 * A_KV

WKV_CHUNK = 64
V7X_VMEM_LIMIT_BYTES = 56 * 1024 * 1024

NT_DIMS = (((1,), (1,)), ((), ()))
TN_DIMS = (((0,), (0,)), ((), ()))


def _params(n_axes=1):
    return pltpu.CompilerParams(dimension_semantics=("arbitrary",) * n_axes,
                                vmem_limit_bytes=V7X_VMEM_LIMIT_BYTES)


def _rmsnorm(x, g):
    return x * lax.rsqrt(jnp.mean(x * x, axis=-1, keepdims=True) + RMS_EPS) * g


def _bdot(a, b):
    return jnp.dot(a.astype(BF16), b.astype(BF16), preferred_element_type=F32)


def _fdot(a, b):
    return jnp.dot(a, b, preferred_element_type=F32, precision=HIGHEST)


def _const_spec(shape):
    return pl.BlockSpec(shape, lambda *_: (0,) * len(shape))


def _norm_proj_kernel(x_ref, g_ref, w_ref, *out_refs, splits):
    xn = _rmsnorm(x_ref[...], g_ref[...])
    z = jnp.dot(xn.astype(BF16), w_ref[...], preferred_element_type=F32)
    off = 0
    for o_ref, n in zip(out_refs, splits):
        o_ref[...] = z[:, off:off + n]
        off += n


def _norm_proj(x, g, w, splits, tm):
    rows, d = x.shape
    n = w.shape[1]
    tm = min(tm, rows)
    return pl.pallas_call(
        functools.partial(_norm_proj_kernel, splits=splits),
        grid=(rows // tm,),
        in_specs=[pl.BlockSpec((tm, d), lambda i: (i, 0)), _const_spec((1, d)), _const_spec((d, n))],
        out_specs=[pl.BlockSpec((tm, s), lambda i: (i, 0)) for s in splits],
        out_shape=[jax.ShapeDtypeStruct((rows, s), F32) for s in splits],
        compiler_params=_params(),
        name="norm_proj",
    )(x, g, w)


def _rope_tables(pos):
    half = HEAD_DIM // 2
    inv = ROPE_THETA ** (-jnp.arange(half, dtype=F32) / half)
    ang = pos.astype(F32)[:, None] * inv[None, :]
    cos, sin = jnp.cos(ang), jnp.sin(ang)
    cos2 = jnp.concatenate([cos, cos], axis=-1)
    sin2 = jnp.concatenate([-sin, sin], axis=-1)
    return jnp.tile(cos2, (1, 2)), jnp.tile(sin2, (1, 2))


def _rope(x, cos, sin):
    width = x.shape[1]
    half = HEAD_DIM // 2
    lane = lax.broadcasted_iota(jnp.int32, x.shape, 1)
    upper = pltpu.roll(x, width - half, 1)
    lower = pltpu.roll(x, half, 1)
    rot = jnp.where((lane % HEAD_DIM) < half, upper, lower)
    return x * cos + rot * sin


def _tile_lanes(t, reps):
    return jnp.concatenate([t] * reps, axis=1)


def _attn_prompt_kernel(cur_ref, prev_ref, cos_ref, sin_ref, cosp_ref, sinp_ref, sink_ref,
                        o_ref, k_out_ref, v_out_ref):
    blk = pl.program_id(1)
    w = cur_ref.shape[0]
    cur = cur_ref[...]
    prev = prev_ref[...]
    cos, sin = cos_ref[...], sin_ref[...]
    q = _rope(cur[:, :A_Q], _tile_lanes(cos, A_HEADS // 2), _tile_lanes(sin, A_HEADS // 2))
    k = _rope(cur[:, A_Q:A_Q + A_KV], cos, sin)
    v = cur[:, A_Q + A_KV:A_PROJ]
    kp = _rope(prev[:, A_Q:A_Q + A_KV], cosp_ref[...], sinp_ref[...])
    vp = prev[:, A_Q + A_KV:A_PROJ]
    k_out_ref[0] = k
    v_out_ref[0] = v

    qi = lax.broadcasted_iota(jnp.int32, (w, 2 * w), 0)
    kj = lax.broadcasted_iota(jnp.int32, (w, 2 * w), 1)
    diff = w + qi - kj
    mask = (diff >= 0) & (diff <= WINDOW) & ((blk > 0) | (kj >= w))
    kk_all = jnp.concatenate([kp, k], axis=0).astype(BF16)
    vv_all = jnp.concatenate([vp, v], axis=0).astype(BF16)
    outs = []
    for h in range(A_HEADS):
        kv = h // A_GROUP
        qh = q[:, h * HEAD_DIM:(h + 1) * HEAD_DIM].astype(BF16)
        kh = kk_all[:, kv * HEAD_DIM:(kv + 1) * HEAD_DIM]
        vh = vv_all[:, kv * HEAD_DIM:(kv + 1) * HEAD_DIM]
        s = lax.dot_general(qh, kh, NT_DIMS, preferred_element_type=F32) * (HEAD_DIM ** -0.5)
        s = jnp.where(mask, s, NEG)
        sink = sink_ref[:, h:h + 1]
        m = jnp.maximum(jnp.max(s, axis=-1, keepdims=True), sink)
        p = jnp.exp(s - m)
        p = p / (jnp.sum(p, axis=-1, keepdims=True) + jnp.exp(sink - m))
        outs.append(jnp.dot(p.astype(BF16), vh, preferred_element_type=F32))
    o_ref[...] = jnp.concatenate(outs, axis=1).astype(o_ref.dtype)


def _attn_prompt(za, cos, sin, sinks, n_seq, t):
    w = min(WINDOW, t)
    nb = t // w
    cur_map = lambda b, i: (b * nb + i, 0)
    prev_map = lambda b, i: (b * nb + jnp.maximum(i - 1, 0), 0)
    tab_cur = lambda b, i: (i, 0)
    tab_prev = lambda b, i: (jnp.maximum(i - 1, 0), 0)
    return pl.pallas_call(
        _attn_prompt_kernel,
        grid=(n_seq, nb),
        in_specs=[pl.BlockSpec((w, A_PROJ), cur_map), pl.BlockSpec((w, A_PROJ), prev_map),
                  pl.BlockSpec((w, A_KV), tab_cur), pl.BlockSpec((w, A_KV), tab_cur),
                  pl.BlockSpec((w, A_KV), tab_prev), pl.BlockSpec((w, A_KV), tab_prev),
                  _const_spec((1, A_HEADS))],
        out_specs=[pl.BlockSpec((w, A_Q), cur_map),
                   pl.BlockSpec((1, w, A_KV), lambda b, i: (b, 0, 0)),
                   pl.BlockSpec((1, w, A_KV), lambda b, i: (b, 0, 0))],
        out_shape=[jax.ShapeDtypeStruct((n_seq * t, A_Q), BF16),
                   jax.ShapeDtypeStruct((n_seq, w, A_KV), F32),
                   jax.ShapeDtypeStruct((n_seq, w, A_KV), F32)],
        compiler_params=_params(2),
        name="attn_prompt",
    )(za, za, cos, sin, cos, sin, sinks)


def _attn_sample_kernel(za_ref, ck_ref, cv_ref, cos_ref, sin_ref, sink_ref, o_ref, kn_ref, vn_ref):
    sb, wb = ck_ref.shape[0], ck_ref.shape[1]
    za = za_ref[...]
    cos, sin = cos_ref[...], sin_ref[...]
    q = _rope(za[:, :A_Q], _tile_lanes(cos, A_HEADS // 2), _tile_lanes(sin, A_HEADS // 2))
    k = _rope(za[:, A_Q:A_Q + A_KV], cos, sin)
    v = za[:, A_Q + A_KV:A_PROJ]
    kn_ref[...] = k
    vn_ref[...] = v
    ck = ck_ref[...].reshape(sb * wb, A_KV).astype(BF16)
    cv = cv_ref[...].reshape(sb * wb, A_KV).astype(BF16)
    row = lax.broadcasted_iota(jnp.int32, (sb, sb * wb), 0)
    col = lax.broadcasted_iota(jnp.int32, (sb, sb * wb), 1)
    own = (col >= row * wb) & (col < (row + 1) * wb)
    outs = []
    for h in range(A_HEADS):
        kv = h // A_GROUP
        sl = slice(kv * HEAD_DIM, (kv + 1) * HEAD_DIM)
        qh = q[:, h * HEAD_DIM:(h + 1) * HEAD_DIM]
        s = lax.dot_general(qh.astype(BF16), ck[:, sl], NT_DIMS, preferred_element_type=F32)
        s = jnp.where(own, s * (HEAD_DIM ** -0.5), NEG)
        s_new = jnp.sum(qh * k[:, sl], axis=-1, keepdims=True) * (HEAD_DIM ** -0.5)
        sink = sink_ref[:, h:h + 1]
        m = jnp.maximum(jnp.maximum(jnp.max(s, axis=-1, keepdims=True), s_new), sink)
        p = jnp.exp(s - m)
        p_new = jnp.exp(s_new - m)
        den = jnp.sum(p, axis=-1, keepdims=True) + p_new + jnp.exp(sink - m)
        o = jnp.dot((p / den).astype(BF16), cv[:, sl], preferred_element_type=F32)
        outs.append(o + (p_new / den) * v[:, sl])
    o_ref[...] = jnp.concatenate(outs, axis=1).astype(o_ref.dtype)


def _attn_sample(za, cache_k, cache_v, cos, sin, sinks):
    n, wb = cache_k.shape[0], cache_k.shape[1]
    sb = min(16, n)
    return pl.pallas_call(
        _attn_sample_kernel,
        grid=(n // sb,),
        in_specs=[pl.BlockSpec((sb, A_PROJ), lambda i: (i, 0)),
                  pl.BlockSpec((sb, wb, A_KV), lambda i: (i, 0, 0)),
                  pl.BlockSpec((sb, wb, A_KV), lambda i: (i, 0, 0)),
                  _const_spec((1, A_KV)), _const_spec((1, A_KV)), _const_spec((1, A_HEADS))],
        out_specs=[pl.BlockSpec((sb, A_Q), lambda i: (i, 0)),
                   pl.BlockSpec((sb, A_KV), lambda i: (i, 0)),
                   pl.BlockSpec((sb, A_KV), lambda i: (i, 0))],
        out_shape=[jax.ShapeDtypeStruct((n, A_Q), BF16),
                   jax.ShapeDtypeStruct((n, A_KV), F32),
                   jax.ShapeDtypeStruct((n, A_KV), F32)],
        compiler_params=_params(),
        name="attn_sample",
    )(za, cache_k, cache_v, cos, sin, sinks)


RWKV_PARAM_NAMES = ("mu", "w0", "w2", "a0", "a2", "g2", "k_k", "k_a", "r_k", "gn_g", "gn_b", "seg")


def _rwkv_param_specs(p):
    return [_const_spec(p[name].shape) for name in RWKV_PARAM_NAMES]


def _rwkv_prep(zb, prev, p):
    o1, o2, o3 = B_WIDTH, 2 * B_WIDTH, 3 * B_WIDTH
    o4, o5 = o3 + DECAY_RANK, o3 + DECAY_RANK + ICLR_RANK
    zs = zb + (prev - zb) * p["mu"]
    r, k, v = zs[:, :o1], zs[:, o1:o2], zs[:, o2:o3]
    wd, ad, gd = zs[:, o3:o4], zs[:, o4:o5], zs[:, o5:]
    lw = -DECAY_SCALE * jax.nn.sigmoid(p["w0"] + _bdot(jnp.tanh(wd), p["w2"]))
    a = jax.nn.sigmoid(p["a0"] + _bdot(ad, p["a2"]))
    g = _bdot(jax.nn.sigmoid(gd), p["g2"])
    kk = k * p["k_k"]
    kk = kk / jnp.maximum(jnp.sqrt(_fdot(kk * kk, p["seg"])), 1e-12)
    k = k * (1.0 + (a - 1.0) * p["k_a"])
    return r, lw, k, v, kk, kk * a, g


def _rwkv_finish(o, r, k, v, g, p):
    seg = p["seg"]
    mean = _fdot(o, seg) * (1.0 / B_HEAD_DIM)
    d = o - mean
    var = _fdot(d * d, seg) * (1.0 / B_HEAD_DIM)
    o = d * lax.rsqrt(var + GN_EPS) * p["gn_g"] + p["gn_b"]
    bonus = _fdot(r * k * p["r_k"], seg) * v
    return (o + bonus) * g


def _inv_unit_lower(a):
    c = a.shape[0]
    eye = (lax.broadcasted_iota(jnp.int32, (c, c), 0) == lax.broadcasted_iota(jnp.int32, (c, c), 1)).astype(F32)
    pw = -a
    t = eye + pw
    span = 2
    while span < c:
        pw = _fdot(pw, pw)
        t = t + _fdot(t, pw)
        span *= 2
    return t


def _rwkv_chunk_kernel(zb_ref, *refs):
    n_p = len(RWKV_PARAM_NAMES)
    p = {name: ref[...] for name, ref in zip(RWKV_PARAM_NAMES, refs[:n_p])}
    o_ref, s_out_ref, carry_ref, state_ref = refs[n_p:]
    step = pl.program_id(1)
    c = zb_ref.shape[0]

    @pl.when(step == 0)
    def _():
        carry_ref[...] = jnp.zeros_like(carry_ref)
        state_ref[...] = jnp.zeros_like(state_ref)

    zb = zb_ref[...]
    rows = lax.broadcasted_iota(jnp.int32, zb.shape, 0)
    prev = jnp.where(rows == 0, carry_ref[0:1, :], pltpu.roll(zb, 1, 0))
    carry_ref[0:1, :] = zb[c - 1:c, :]
    r, lw, k, v, kk, b, g = _rwkv_prep(zb, prev, p)

    ri = lax.broadcasted_iota(jnp.int32, (c, c), 0)
    ci = lax.broadcasted_iota(jnp.int32, (c, c), 1)
    incl = ri >= ci
    strict = ri > ci
    gcum = _fdot(incl.astype(F32), lw)
    g_last = gcum[c - 1:c, :]
    e_neg = jnp.exp(-gcum)
    e_hat = jnp.exp(g_last - gcum)
    kap = kk * jnp.exp(gcum - lw)
    rt = r * jnp.exp(gcum)
    kt, bt = k * e_neg, b * e_neg
    khat, bhat = k * e_hat, b * e_hat
    dec_last = jnp.exp(g_last)

    outs = []
    for h in range(B_HEADS):
        sl = slice(h * B_HEAD_DIM, (h + 1) * B_HEAD_DIM)
        lhs = jnp.concatenate([kap[:, sl], rt[:, sl]], axis=0)
        rhs = jnp.concatenate([kt[:, sl], bt[:, sl]], axis=0)
        m = lax.dot_general(lhs, rhs, NT_DIMS, preferred_element_type=F32, precision=HIGHEST)
        a_k = jnp.where(strict, m[:c, :c], 0.0)
        a_b = jnp.where(strict, m[:c, c:], 0.0)
        p_k = jnp.where(incl, m[c:, :c], 0.0)
        p_b = jnp.where(incl, m[c:, c:], 0.0)
        s0 = state_ref[h]
        x = lax.dot_general(lhs, s0, NT_DIMS, preferred_element_type=F32, precision=HIGHEST)
        vh = v[:, sl]
        u = _fdot(_inv_unit_lower(a_b), x[:c] + _fdot(a_k, vh))
        outs.append(x[c:] + _fdot(p_k, vh) - _fdot(p_b, u))
        s_new = (s0 * dec_last[:, sl]
                 + lax.dot_general(vh, khat[:, sl], TN_DIMS, preferred_element_type=F32, precision=HIGHEST)
                 - lax.dot_general(u, bhat[:, sl], TN_DIMS, preferred_element_type=F32, precision=HIGHEST))
        state_ref[h] = s_new
        s_out_ref[0, h] = s_new
    o = jnp.concatenate(outs, axis=1)
    o_ref[...] = _rwkv_finish(o, r, k, v, g, p).astype(o_ref.dtype)


def _rwkv_prompt(zb, p, n_seq, t):
    c = min(WKV_CHUNK, t)
    nc = t // c
    width = zb.shape[1]
    return pl.pallas_call(
        _rwkv_chunk_kernel,
        grid=(n_seq, nc),
        in_specs=[pl.BlockSpec((c, width), lambda b, i: (b * nc + i, 0))] + _rwkv_param_specs(p),
        out_specs=[pl.BlockSpec((c, B_WIDTH), lambda b, i: (b * nc + i, 0)),
                   pl.BlockSpec((1, B_HEADS, B_HEAD_DIM, B_HEAD_DIM), lambda b, i: (b, 0, 0, 0))],
        out_shape=[jax.ShapeDtypeStruct((n_seq * t, B_WIDTH), BF16),
                   jax.ShapeDtypeStruct((n_seq, B_HEADS, B_HEAD_DIM, B_HEAD_DIM), F32)],
        scratch_shapes=[pltpu.VMEM((8, width), F32),
                        pltpu.VMEM((B_HEADS, B_HEAD_DIM, B_HEAD_DIM), F32)],
        compiler_params=_params(2),
        name="rwkv_prompt",
    )(zb, *[p[name] for name in RWKV_PARAM_NAMES])


def _rwkv_sample_prep_kernel(zb_ref, shift_ref, *refs):
    n_p = len(RWKV_PARAM_NAMES)
    p = {name: ref[...] for name, ref in zip(RWKV_PARAM_NAMES, refs[:n_p])}
    vec_ref, g_ref = refs[n_p:]
    r, lw, k, v, kk, b, g = _rwkv_prep(zb_ref[...], shift_ref[...], p)
    for idx, val in enumerate((r, jnp.exp(lw), k, v, kk, b)):
        vec_ref[idx] = val
    g_ref[...] = g


def _rwkv_sample_prep(zb, shift, p):
    n, width = zb.shape
    return pl.pallas_call(
        _rwkv_sample_prep_kernel,
        grid=(1,),
        in_specs=[_const_spec((n, width)), _const_spec((n, width))] + _rwkv_param_specs(p),
        out_specs=[_const_spec((6, n, B_WIDTH)), _const_spec((n, B_WIDTH))],
        out_shape=[jax.ShapeDtypeStruct((6, n, B_WIDTH), F32), jax.ShapeDtypeStruct((n, B_WIDTH), F32)],
        compiler_params=_params(),
        name="rwkv_sample_prep",
    )(zb, shift, *[p[name] for name in RWKV_PARAM_NAMES])


def _wkv_step_kernel(s_ref, vec_ref, s_out_ref, o_ref, st_ref):
    d = B_HEAD_DIM
    st_ref[...] = s_ref[...].T
    r, w, k, v, kk, b = (vec_ref[i] for i in range(6))
    for i in range(d):
        rows = slice(i * d, (i + 1) * d)
        s_i = st_ref[rows, :]
        sa = -jnp.sum(s_i * kk, axis=0, keepdims=True)
        s_i = s_i * w + sa * b + v[i:i + 1, :] * k
        st_ref[rows, :] = s_i
        o_ref[i:i + 1, :] = jnp.sum(s_i * r, axis=0, keepdims=True)
    s_out_ref[...] = st_ref[...].T


def _wkv_step(state2d, vecs_t):
    pairs, dd = state2d.shape
    d = B_HEAD_DIM
    pb = min(128, pairs)
    return pl.pallas_call(
        _wkv_step_kernel,
        grid=(pairs // pb,),
        in_specs=[pl.BlockSpec((pb, dd), lambda i: (i, 0)), pl.BlockSpec((6, d, pb), lambda i: (0, 0, i))],
        out_specs=[pl.BlockSpec((pb, dd), lambda i: (i, 0)), pl.BlockSpec((d, pb), lambda i: (0, i))],
        out_shape=[jax.ShapeDtypeStruct((pairs, dd), F32), jax.ShapeDtypeStruct((d, pairs), F32)],
        scratch_shapes=[pltpu.VMEM((dd, pb), F32)],
        compiler_params=_params(),
        name="wkv_step",
    )(state2d, vecs_t)


def _rwkv_sample_finish_kernel(o_ref, vec_ref, g_ref, *refs):
    n_p = len(RWKV_PARAM_NAMES)
    p = {name: ref[...] for name, ref in zip(RWKV_PARAM_NAMES, refs[:n_p])}
    out_ref = refs[n_p]
    out_ref[...] = _rwkv_finish(o_ref[...], vec_ref[0], vec_ref[2], vec_ref[3], g_ref[...], p).astype(out_ref.dtype)


def _rwkv_sample_finish(o, vecs, g, p):
    n = o.shape[0]
    return pl.pallas_call(
        _rwkv_sample_finish_kernel,
        grid=(1,),
        in_specs=[_const_spec((n, B_WIDTH)), _const_spec((6, n, B_WIDTH)), _const_spec((n, B_WIDTH))]
        + _rwkv_param_specs(p),
        out_specs=_const_spec((n, B_WIDTH)),
        out_shape=jax.ShapeDtypeStruct((n, B_WIDTH), BF16),
        compiler_params=_params(),
        name="rwkv_sample_finish",
    )(o, vecs, g, *[p[name] for name in RWKV_PARAM_NAMES])


def _proj_residual_kernel(a_ref, b_ref, wa_ref, wb_ref, h_ref, g_ref, o_ref):
    acc = jnp.dot(a_ref[...], wa_ref[...], preferred_element_type=F32)
    acc = acc + jnp.dot(b_ref[...], wb_ref[...], preferred_element_type=F32)
    o_ref[...] = h_ref[...] + _rmsnorm(acc, g_ref[...])


def _proj_residual(a, b, wa, wb, h, g, tm):
    rows, d = h.shape
    tm = min(tm, rows)
    ka, kb = a.shape[1], b.shape[1]
    return pl.pallas_call(
        _proj_residual_kernel,
        grid=(rows // tm,),
        in_specs=[pl.BlockSpec((tm, ka), lambda i: (i, 0)), pl.BlockSpec((tm, kb), lambda i: (i, 0)),
                  _const_spec((ka, d)), _const_spec((kb, d)),
                  pl.BlockSpec((tm, d), lambda i: (i, 0)), _const_spec((1, d))],
        out_specs=pl.BlockSpec((tm, d), lambda i: (i, 0)),
        out_shape=jax.ShapeDtypeStruct((rows, d), F32),
        compiler_params=_params(),
        name="proj_residual",
    )(a, b, wa, wb, h, g)


def _gelu_erf(x):
    return 0.5 * x * (1.0 + lax.erf(x * 0.7071067811865476))


def _gelu_tanh(x):
    return 0.5 * x * (1.0 + jnp.tanh(0.7978845608028654 * (x + 0.044715 * (x * x * x))))


def _layernorm(x, g, b):
    mu = jnp.mean(x, axis=-1, keepdims=True)
    d = x - mu
    var = jnp.mean(d * d, axis=-1, keepdims=True)
    return d * lax.rsqrt(var + LN_EPS) * g + b


def _gmlp_kernel(h_ref, gpre_ref, win_ref, lng_ref, lnb_ref, ws_ref, bs_ref, wout_ref, gpost_ref,
                 o_ref, *v_refs, chunk):
    x = h_ref[...]
    tm = x.shape[0]
    width = wout_ref.shape[0]
    hd = width // C_HEADS
    xn = _rmsnorm(x, gpre_ref[...])
    z = _gelu_erf(jnp.dot(xn.astype(BF16), win_ref[...], preferred_element_type=F32))
    u = z[:, :width]
    v = _layernorm(z[:, width:], lng_ref[...], lnb_ref[...])
    if chunk == 1:
        mixed = v * ws_ref[...] + bs_ref[...]
        v_refs[0][...] = v
    else:
        ri = lax.broadcasted_iota(jnp.int32, (chunk, chunk), 0)
        ci = lax.broadcasted_iota(jnp.int32, (chunk, chunk), 1)
        vb = v.astype(BF16)
        bias = bs_ref[...]
        cols = []
        for hh in range(C_HEADS):
            wtri = jnp.where(ri >= ci, ws_ref[hh], 0.0).astype(BF16)
            parts = [jnp.dot(wtri, vb[c0:c0 + chunk, hh * hd:(hh + 1) * hd], preferred_element_type=F32)
                     + bias[:, hh:hh + 1] for c0 in range(0, tm, chunk)]
            cols.append(jnp.concatenate(parts, axis=0) if len(parts) > 1 else parts[0])
        mixed = jnp.concatenate(cols, axis=1)
    y = u * mixed
    out = jnp.dot(y.astype(BF16), wout_ref[...], preferred_element_type=F32)
    o_ref[...] = x + _rmsnorm(out, gpost_ref[...])


def _gmlp(h, gpre, win, lng, lnb, ws, bs, wout, gpost, chunk, tm):
    rows, d = h.shape
    tm = min(tm, rows)
    width = wout.shape[0]
    out_specs = [pl.BlockSpec((tm, d), lambda i: (i, 0))]
    out_shape = [jax.ShapeDtypeStruct((rows, d), F32)]
    if chunk == 1:
        out_specs.append(pl.BlockSpec((tm, width), lambda i: (i, 0)))
        out_shape.append(jax.ShapeDtypeStruct((rows, width), F32))
    return pl.pallas_call(
        functools.partial(_gmlp_kernel, chunk=chunk),
        grid=(rows // tm,),
        in_specs=[pl.BlockSpec((tm, d), lambda i: (i, 0)), _const_spec((1, d)), _const_spec(win.shape),
                  _const_spec((1, width)), _const_spec((1, width)), _const_spec(ws.shape),
                  _const_spec(bs.shape), _const_spec(wout.shape), _const_spec((1, d))],
        out_specs=out_specs,
        out_shape=out_shape,
        compiler_params=_params(),
        name="gmlp",
    )(h, gpre, win, lng, lnb, ws, bs, wout, gpost)


def _ffn_kernel(h_ref, gpre_ref, wg_ref, wu_ref, cw_ref, cb_ref, wd_ref, gpost_ref, *refs,
                carried, tiles_per_seq):
    x = h_ref[...]
    tm = x.shape[0]
    xn = _rmsnorm(x, gpre_ref[...]).astype(BF16)
    gp = jnp.dot(xn, wg_ref[...], preferred_element_type=F32)
    up = jnp.dot(xn, wu_ref[...], preferred_element_type=F32)
    if carried:
        o_ref, tail_ref, carry_ref = refs

        @pl.when(pl.program_id(0) % tiles_per_seq == 0)
        def _():
            carry_ref[...] = jnp.zeros_like(carry_ref)

        rows = lax.broadcasted_iota(jnp.int32, gp.shape, 0)
        c2, c1 = carry_ref[0:1, :], carry_ref[1:2, :]
        x1 = jnp.where(rows == 0, c1, pltpu.roll(gp, 1, 0))
        x2 = jnp.where(rows == 0, c2, jnp.where(rows == 1, c1, pltpu.roll(gp, 2, 0)))
        carry_ref[0:2, :] = gp[tm - 2:tm, :]
        tail_ref[0] = gp[tm - 2:tm, :]
    else:
        s2_ref, s1_ref, o_ref, gp_ref = refs
        x2, x1 = s2_ref[...], s1_ref[...]
        gp_ref[...] = gp
    cw = cw_ref[...]
    conv = cb_ref[...] + cw[2:3, :] * gp + cw[0:1, :] * x2 + cw[1:2, :] * x1
    hid = _gelu_tanh(conv) * up
    out = jnp.dot(hid.astype(BF16), wd_ref[...], preferred_element_type=F32)
    o_ref[...] = x + _rmsnorm(out, gpost_ref[...])


def _ffn(h, gpre, wg, wu, cw, cb, wd, gpost, tm, seq_len=None, state=None):
    rows, d = h.shape
    f = wg.shape[1]
    tm = min(tm, rows)
    carried = state is None
    row_spec = pl.BlockSpec((tm, d), lambda i: (i, 0))
    in_specs = [row_spec, _const_spec((1, d)), _const_spec((d, f)), _const_spec((d, f)),
                _const_spec(cw.shape), _const_spec((1, f)), _const_spec((f, d)), _const_spec((1, d))]
    args = [h, gpre, wg, wu, cw, cb, wd, gpost]
    if carried:
        tm = min(tm, seq_len)
        tiles_per_seq = seq_len // tm
        row_spec = pl.BlockSpec((tm, d), lambda i: (i, 0))
        in_specs[0] = row_spec
        out_specs = [row_spec, pl.BlockSpec((1, 2, f), lambda i: (i // tiles_per_seq, 0, 0))]
        out_shape = [jax.ShapeDtypeStruct((rows, d), F32),
                     jax.ShapeDtypeStruct((rows // seq_len, 2, f), F32)]
        scratch = [pltpu.VMEM((8, f), F32)]
    else:
        tiles_per_seq = 1
        in_specs += [pl.BlockSpec((tm, f), lambda i: (i, 0))] * 2
        args += [state[:, 0], state[:, 1]]
        out_specs = [row_spec, pl.BlockSpec((tm, f), lambda i: (i, 0))]
        out_shape = [jax.ShapeDtypeStruct((rows, d), F32), jax.ShapeDtypeStruct((rows, f), F32)]
        scratch = []
    return pl.pallas_call(
        functools.partial(_ffn_kernel, carried=carried, tiles_per_seq=tiles_per_seq),
        grid=(rows // tm,),
        in_specs=in_specs,
        out_specs=out_specs,
        out_shape=out_shape,
        scratch_shapes=scratch,
        compiler_params=_params(),
        name="conv_ffn",
    )(*args)


def kernel(x_prompt, x_sample, cache_win_k, cache_win_v, state_wkv, state_shift, state_ffn_conv,
           norm_mix_pre, norm_mix_post, norm_ffn_pre, norm_ffn_post,
           w_in_even, attn_sinks, shift_mu, decay_w0, decay_w2, iclr_a0, iclr_a2, gate_g2,
           key_k, key_a, bonus_r_k, gn_gain, gn_bias, w_out_even,
           w_in_odd, sgu_ln_gain, sgu_ln_bias, sgu_w, sgu_b, w_out_odd,
           ffn_w_gate, ffn_w_up, ffn_conv_w, ffn_conv_b, ffn_w_down):
    n_seq, t, d = x_prompt.shape
    n_dec, dec_t, _ = x_sample.shape
    assert dec_t == 1, "the sample path is written for one new token per sequence"
    depth = norm_mix_pre.shape[0]
    wp = min(WINDOW, t)
    row = lambda a: a.reshape(1, -1)

    hp = x_prompt.reshape(n_seq * t, d)
    hs = x_sample.reshape(n_dec, d)
    cos_p, sin_p = _rope_tables(jnp.arange(t, dtype=jnp.int32))
    cos_s, sin_s = _rope_tables(PAST_LEN + jnp.arange(1, dtype=jnp.int32))
    seg = jnp.kron(jnp.eye(B_HEADS, dtype=F32), jnp.ones((B_HEAD_DIM, B_HEAD_DIM), F32))

    kp_l, vp_l, ks_l, vs_l, sp_l, ss_l, shp_l, shs_l, vsgu_l, cp_l, cs_l = ([] for _ in range(11))
    for layer in range(depth):
        j = layer // 2
        g_pre, g_post = row(norm_mix_pre[layer]), row(norm_mix_post[layer])
        if layer % 2 == 0:
            w_in = w_in_even[j].astype(BF16)
            w_out = w_out_even[j].astype(BF16)
            sinks = row(attn_sinks[j])
            p = {"mu": row(shift_mu[j]), "w0": row(decay_w0[j]), "w2": decay_w2[j].astype(BF16),
                 "a0": row(iclr_a0[j]), "a2": iclr_a2[j].astype(BF16), "g2": gate_g2[j].astype(BF16),
                 "k_k": row(key_k[j]), "k_a": row(key_a[j]), "r_k": row(bonus_r_k[j]),
                 "gn_g": row(gn_gain[j]), "gn_b": row(gn_bias[j]), "seg": seg}
            za, zb = _norm_proj(hp, g_pre, w_in, (A_PROJ, w_in.shape[1] - A_PROJ), tm=512)
            oa, kp, vp = _attn_prompt(za, cos_p, sin_p, sinks, n_seq, t)
            ob, sp = _rwkv_prompt(zb, p, n_seq, t)
            hp = _proj_residual(oa, ob, w_out[:A_Q], w_out[A_Q:], hp, g_post, tm=512)
            kp_l.append(kp.reshape(n_seq, wp, A_KV_HEADS, HEAD_DIM))
            vp_l.append(vp.reshape(n_seq, wp, A_KV_HEADS, HEAD_DIM))
            sp_l.append(sp)
            shp_l.append(zb.reshape(n_seq, t, -1)[:, -1])
            za, zb = _norm_proj(hs, g_pre, w_in, (A_PROJ, w_in.shape[1] - A_PROJ), tm=512)
            wb = cache_win_k.shape[2]
            oa, kn, vn = _attn_sample(za, cache_win_k[j].reshape(n_dec, wb, A_KV),
                                      cache_win_v[j].reshape(n_dec, wb, A_KV), cos_s, sin_s, sinks)
            vecs, gate = _rwkv_sample_prep(zb, state_shift[j], p)
            pairs = n_dec * B_HEADS
            vecs_t = vecs.reshape(6, pairs, B_HEAD_DIM).transpose(0, 2, 1)
            s_new, o_t = _wkv_step(state_wkv[j].reshape(pairs, B_HEAD_DIM * B_HEAD_DIM), vecs_t)
            ob = _rwkv_sample_finish(o_t.T.reshape(n_dec, B_WIDTH), vecs, gate, p)
            hs = _proj_residual(oa, ob, w_out[:A_Q], w_out[A_Q:], hs, g_post, tm=512)
            ks_l.append(kn.reshape(n_dec, 1, A_KV_HEADS, HEAD_DIM))
            vs_l.append(vn.reshape(n_dec, 1, A_KV_HEADS, HEAD_DIM))
            ss_l.append(s_new.reshape(n_dec, B_HEADS, B_HEAD_DIM, B_HEAD_DIM))
            shs_l.append(zb)
        else:
            w_in = w_in_odd[j].astype(BF16)
            w_out = w_out_odd[j].astype(BF16)
            lng, lnb = row(sgu_ln_gain[j]), row(sgu_ln_bias[j])
            chunk = min(t, CHUNK)
            hd = w_out.shape[0] // C_HEADS
            hp, = _gmlp(hp, g_pre, w_in, lng, lnb, sgu_w[j][:, :chunk, :chunk], sgu_b[j][:, :chunk].T,
                        w_out, g_post, chunk=chunk, tm=2 * chunk)
            hs, vsg = _gmlp(hs, g_pre, w_in, lng, lnb, row(jnp.repeat(sgu_w[j][:, 0, 0], hd)),
                            row(jnp.repeat(sgu_b[j][:, 0], hd)), w_out, g_post, chunk=1, tm=512)
            vsgu_l.append(vsg.reshape(n_dec, 1, -1))
        f_pre, f_post = row(norm_ffn_pre[layer]), row(norm_ffn_post[layer])
        wg, wu, wd = (w[layer].astype(BF16) for w in (ffn_w_gate, ffn_w_up, ffn_w_down))
        cw, cb = ffn_conv_w[layer], row(ffn_conv_b[layer])
        hp, cp = _ffn(hp, f_pre, wg, wu, cw, cb, wd, f_post, tm=256, seq_len=t)
        hs, gp = _ffn(hs, f_pre, wg, wu, cw, cb, wd, f_post, tm=256, state=state_ffn_conv[layer])
        cp_l.append(cp)
        cs_l.append(jnp.stack([state_ffn_conv[layer][:, 1], gp], axis=1))

    return (hp.reshape(n_seq, t, d), hs.reshape(n_dec, 1, d),
            jnp.stack(kp_l), jnp.stack(vp_l), jnp.stack(ks_l), jnp.stack(vs_l),
            jnp.stack(sp_l), jnp.stack(ss_l), jnp.stack(shp_l), jnp.stack(shs_l),
            jnp.stack(vsgu_l), jnp.stack(cp_l), jnp.stack(cs_l))
```

```python
import functools

import jax
import jax.numpy as jnp
from jax import lax
from jax.experimental import pallas as pl
from jax.experimental.pallas import tpu as pltpu

F32 = jnp.float32
BF16 = jnp.bfloat16

PAST_LEN = 8192
HEAD_DIM = 64
A_HEADS = 8
A_KV_HEADS = 2
A_GROUP = A_HEADS // A_KV_HEADS
WINDOW = 128
ROPE_THETA = 10000.0
B_HEAD_DIM = 64
B_HEADS = 8
B_WIDTH = B_HEADS * B_HEAD_DIM
DECAY_RANK = 64
ICLR_RANK = 64
DECAY_SCALE = 0.606531
GN_EPS = 64e-5
CHUNK = 128
C_HEADS = 8
RMS_EPS = 1e-6
LN_EPS = 1e-5
NEG = -1e30
A_Q = A_HEADS * HEAD_DIM
A_KV = A_KV_HEADS * HEAD_DIM
A_PROJ = A_Q + 2 * A_KV

WKV_CHUNK = 64
WKV_CHUNKS_PER_STEP = 4
MXU_LANES = 256
HEADS_PER_GROUP = MXU_LANES // B_HEAD_DIM
V7X_VMEM_LIMIT_BYTES = 56 * 1024 * 1024

NT_DIMS = (((1,), (1,)), ((), ()))
TN_DIMS = (((0,), (0,)), ((), ()))


def _params(n_axes=1):
    return pltpu.CompilerParams(dimension_semantics=("arbitrary",) * n_axes,
                                vmem_limit_bytes=V7X_VMEM_LIMIT_BYTES)


def _rmsnorm(x, g):
    return x * lax.rsqrt(jnp.mean(x * x, axis=-1, keepdims=True) + RMS_EPS) * g


def _bdot(a, b):
    return jnp.dot(a.astype(BF16), b.astype(BF16), preferred_element_type=F32)


def _const_spec(shape):
    return pl.BlockSpec(shape, lambda *_: (0,) * len(shape))


def _norm_proj_kernel(x_ref, g_ref, w_ref, *out_refs, splits):
    xn = _rmsnorm(x_ref[...], g_ref[...])
    z = jnp.dot(xn.astype(BF16), w_ref[...], preferred_element_type=F32)
    off = 0
    for o_ref, n in zip(out_refs, splits):
        o_ref[...] = z[:, off:off + n]
        off += n


def _norm_proj(x, g, w, splits, tm):
    rows, d = x.shape
    n = w.shape[1]
    tm = min(tm, rows)
    return pl.pallas_call(
        functools.partial(_norm_proj_kernel, splits=splits),
        grid=(rows // tm,),
        in_specs=[pl.BlockSpec((tm, d), lambda i: (i, 0)), _const_spec((1, d)), _const_spec((d, n))],
        out_specs=[pl.BlockSpec((tm, s), lambda i: (i, 0)) for s in splits],
        out_shape=[jax.ShapeDtypeStruct((rows, s), F32) for s in splits],
        compiler_params=_params(),
        name="norm_proj",
    )(x, g, w)


def _rope_tables(pos):
    half = HEAD_DIM // 2
    inv = ROPE_THETA ** (-jnp.arange(half, dtype=F32) / half)
    ang = pos.astype(F32)[:, None] * inv[None, :]
    cos, sin = jnp.cos(ang), jnp.sin(ang)
    cos2 = jnp.concatenate([cos, cos], axis=-1)
    sin2 = jnp.concatenate([-sin, sin], axis=-1)
    return jnp.tile(cos2, (1, 2)), jnp.tile(sin2, (1, 2))


def _rope(x, cos, sin):
    width = x.shape[1]
    half = HEAD_DIM // 2
    lane = lax.broadcasted_iota(jnp.int32, x.shape, 1)
    upper = pltpu.roll(x, width - half, 1)
    lower = pltpu.roll(x, half, 1)
    rot = jnp.where((lane % HEAD_DIM) < half, upper, lower)
    return x * cos + rot * sin


def _tile_lanes(t, reps):
    return jnp.concatenate([t] * reps, axis=1)


def _attn_prompt_kernel(cur_ref, prev_ref, cos_ref, sin_ref, cosp_ref, sinp_ref, sink_ref,
                        o_ref, k_out_ref, v_out_ref):
    blk = pl.program_id(1)
    w = cur_ref.shape[0]
    cur = cur_ref[...]
    prev = prev_ref[...]
    cos, sin = cos_ref[...], sin_ref[...]
    q = _rope(cur[:, :A_Q], _tile_lanes(cos, A_HEADS // 2), _tile_lanes(sin, A_HEADS // 2))
    k = _rope(cur[:, A_Q:A_Q + A_KV], cos, sin)
    v = cur[:, A_Q + A_KV:A_PROJ]
    kp = _rope(prev[:, A_Q:A_Q + A_KV], cosp_ref[...], sinp_ref[...])
    vp = prev[:, A_Q + A_KV:A_PROJ]
    k_out_ref[0] = k
    v_out_ref[0] = v

    qi = lax.broadcasted_iota(jnp.int32, (w, 2 * w), 0)
    kj = lax.broadcasted_iota(jnp.int32, (w, 2 * w), 1)
    diff = w + qi - kj
    mask = (diff >= 0) & (diff <= WINDOW) & ((blk > 0) | (kj >= w))
    kk_all = jnp.concatenate([kp, k], axis=0).astype(BF16)
    vv_all = jnp.concatenate([vp, v], axis=0).astype(BF16)
    outs = []
    for h in range(A_HEADS):
        kv = h // A_GROUP
        qh = q[:, h * HEAD_DIM:(h + 1) * HEAD_DIM].astype(BF16)
        kh = kk_all[:, kv * HEAD_DIM:(kv + 1) * HEAD_DIM]
        vh = vv_all[:, kv * HEAD_DIM:(kv + 1) * HEAD_DIM]
        s = lax.dot_general(qh, kh, NT_DIMS, preferred_element_type=F32) * (HEAD_DIM ** -0.5)
        s = jnp.where(mask, s, NEG)
        sink = sink_ref[:, h:h + 1]
        m = jnp.maximum(jnp.max(s, axis=-1, keepdims=True), sink)
        p = jnp.exp(s - m)
        p = p / (jnp.sum(p, axis=-1, keepdims=True) + jnp.exp(sink - m))
        outs.append(jnp.dot(p.astype(BF16), vh, preferred_element_type=F32))
    o_ref[...] = jnp.concatenate(outs, axis=1).astype(o_ref.dtype)


def _attn_prompt(za, cos, sin, sinks, n_seq, t):
    w = min(WINDOW, t)
    nb = t // w
    cur_map = lambda b, i: (b * nb + i, 0)
    prev_map = lambda b, i: (b * nb + jnp.maximum(i - 1, 0), 0)
    tab_cur = lambda b, i: (i, 0)
    tab_prev = lambda b, i: (jnp.maximum(i - 1, 0), 0)
    return pl.pallas_call(
        _attn_prompt_kernel,
        grid=(n_seq, nb),
        in_specs=[pl.BlockSpec((w, A_PROJ), cur_map), pl.BlockSpec((w, A_PROJ), prev_map),
                  pl.BlockSpec((w, A_KV), tab_cur), pl.BlockSpec((w, A_KV), tab_cur),
                  pl.BlockSpec((w, A_KV), tab_prev), pl.BlockSpec((w, A_KV), tab_prev),
                  _const_spec((1, A_HEADS))],
        out_specs=[pl.BlockSpec((w, A_Q), cur_map),
                   pl.BlockSpec((1, w, A_KV), lambda b, i: (b, 0, 0)),
                   pl.BlockSpec((1, w, A_KV), lambda b, i: (b, 0, 0))],
        out_shape=[jax.ShapeDtypeStruct((n_seq * t, A_Q), BF16),
                   jax.ShapeDtypeStruct((n_seq, w, A_KV), F32),
                   jax.ShapeDtypeStruct((n_seq, w, A_KV), F32)],
        compiler_params=_params(2),
        name="attn_prompt",
    )(za, za, cos, sin, cos, sin, sinks)


def _attn_sample_kernel(za_ref, ck_ref, cv_ref, cos_ref, sin_ref, sink_ref, o_ref, kn_ref, vn_ref):
    sb, wb = ck_ref.shape[0], ck_ref.shape[1]
    za = za_ref[...]
    cos, sin = cos_ref[...], sin_ref[...]
    q = _rope(za[:, :A_Q], _tile_lanes(cos, A_HEADS // 2), _tile_lanes(sin, A_HEADS // 2))
    k = _rope(za[:, A_Q:A_Q + A_KV], cos, sin)
    v = za[:, A_Q + A_KV:A_PROJ]
    kn_ref[...] = k
    vn_ref[...] = v
    ck = ck_ref[...].reshape(sb * wb, A_KV).astype(BF16)
    cv = cv_ref[...].reshape(sb * wb, A_KV).astype(BF16)
    row = lax.broadcasted_iota(jnp.int32, (sb, sb * wb), 0)
    col = lax.broadcasted_iota(jnp.int32, (sb, sb * wb), 1)
    own = (col >= row * wb) & (col < (row + 1) * wb)
    outs = []
    for h in range(A_HEADS):
        kv = h // A_GROUP
        sl = slice(kv * HEAD_DIM, (kv + 1) * HEAD_DIM)
        qh = q[:, h * HEAD_DIM:(h + 1) * HEAD_DIM]
        s = lax.dot_general(qh.astype(BF16), ck[:, sl], NT_DIMS, preferred_element_type=F32)
        s = jnp.where(own, s * (HEAD_DIM ** -0.5), NEG)
        s_new = jnp.sum(qh * k[:, sl], axis=-1, keepdims=True) * (HEAD_DIM ** -0.5)
        sink = sink_ref[:, h:h + 1]
        m = jnp.maximum(jnp.maximum(jnp.max(s, axis=-1, keepdims=True), s_new), sink)
        p = jnp.exp(s - m)
        p_new = jnp.exp(s_new - m)
        den = jnp.sum(p, axis=-1, keepdims=True) + p_new + jnp.exp(sink - m)
        o = jnp.dot((p / den).astype(BF16), cv[:, sl], preferred_element_type=F32)
        outs.append(o + (p_new / den) * v[:, sl])
    o_ref[...] = jnp.concatenate(outs, axis=1).astype(o_ref.dtype)


def _attn_sample(za, cache_k, cache_v, cos, sin, sinks):
    n, wb = cache_k.shape[0], cache_k.shape[1]
    sb = min(16, n)
    return pl.pallas_call(
        _attn_sample_kernel,
        grid=(n // sb,),
        in_specs=[pl.BlockSpec((sb, A_PROJ), lambda i: (i, 0)),
                  pl.BlockSpec((sb, wb, A_KV), lambda i: (i, 0, 0)),
                  pl.BlockSpec((sb, wb, A_KV), lambda i: (i, 0, 0)),
                  _const_spec((1, A_KV)), _const_spec((1, A_KV)), _const_spec((1, A_HEADS))],
        out_specs=[pl.BlockSpec((sb, A_Q), lambda i: (i, 0)),
                   pl.BlockSpec((sb, A_KV), lambda i: (i, 0)),
                   pl.BlockSpec((sb, A_KV), lambda i: (i, 0))],
        out_shape=[jax.ShapeDtypeStruct((n, A_Q), BF16),
                   jax.ShapeDtypeStruct((n, A_KV), F32),
                   jax.ShapeDtypeStruct((n, A_KV), F32)],
        compiler_params=_params(),
        name="attn_sample",
    )(za, cache_k, cache_v, cos, sin, sinks)


RWKV_PARAM_NAMES = ("mu", "w0", "w2", "a0", "a2", "g2", "k_k", "k_a", "r_k", "gn_g", "gn_b", "seg")


def _rwkv_param_specs(p):
    return [_const_spec(p[name].shape) for name in RWKV_PARAM_NAMES]


def _rwkv_prep(zb, prev, p):
    o1, o2, o3 = B_WIDTH, 2 * B_WIDTH, 3 * B_WIDTH
    o4, o5 = o3 + DECAY_RANK, o3 + DECAY_RANK + ICLR_RANK
    zs = zb + (prev - zb) * p["mu"]
    r, k, v = zs[:, :o1], zs[:, o1:o2], zs[:, o2:o3]
    wd, ad, gd = zs[:, o3:o4], zs[:, o4:o5], zs[:, o5:]
    lw = -DECAY_SCALE * jax.nn.sigmoid(p["w0"] + _bdot(jnp.tanh(wd), p["w2"]))
    a = jax.nn.sigmoid(p["a0"] + _bdot(ad, p["a2"]))
    g = _bdot(jax.nn.sigmoid(gd), p["g2"])
    kk = k * p["k_k"]
    kk = kk / jnp.maximum(jnp.sqrt(_segsum(kk * kk, p["seg"])), 1e-12)
    k = k * (1.0 + (a - 1.0) * p["k_a"])
    return r, lw, k, v, kk, kk * a, g


def _split2(x):
    hi = x.astype(BF16)
    return hi, (x - hi.astype(F32)).astype(BF16)


def _segsum(x, seg):
    outs = []
    for g0 in range(0, x.shape[1], MXU_LANES):
        hi, lo = _split2(x[:, g0:g0 + MXU_LANES])
        outs.append(jnp.dot(hi, seg, preferred_element_type=F32) + jnp.dot(lo, seg, preferred_element_type=F32))
    return jnp.concatenate(outs, axis=1)


def _rwkv_finish(o, r, k, v, g, p):
    seg = p["seg"]
    mean = _segsum(o, seg) * (1.0 / B_HEAD_DIM)
    d = o - mean
    var = _segsum(d * d, seg) * (1.0 / B_HEAD_DIM)
    o = d * lax.rsqrt(var + GN_EPS) * p["gn_g"] + p["gn_b"]
    bonus = _segsum(r * k * p["r_k"], seg) * v
    return (o + bonus) * g


def _split3(x):
    hi = x.astype(BF16)
    r1 = x - hi.astype(F32)
    mid = r1.astype(BF16)
    return hi, mid, (r1 - mid.astype(F32)).astype(BF16)


def _mm(a, b):
    return jnp.dot(a.astype(BF16), b.astype(BF16), preferred_element_type=F32)


def _head_block_diag(x, bd_mask):
    tiled = jnp.concatenate([x.astype(BF16)] * HEADS_PER_GROUP, axis=0)
    return jnp.where(bd_mask, tiled, jnp.zeros_like(tiled))


def _rwkv_chunk_kernel(zb_ref, *refs, n_sub):
    n_p = len(RWKV_PARAM_NAMES)
    p = {name: ref[...] for name, ref in zip(RWKV_PARAM_NAMES, refs[:n_p])}
    o_ref, s_out_ref, carry_ref, state_ref = refs[n_p:]
    n_rows = zb_ref.shape[0]
    c = n_rows // n_sub
    d = B_HEAD_DIM

    @pl.when(pl.program_id(1) == 0)
    def _():
        carry_ref[...] = jnp.zeros_like(carry_ref)
        state_ref[...] = jnp.zeros_like(state_ref)

    zb = zb_ref[...]
    rows = lax.broadcasted_iota(jnp.int32, zb.shape, 0)
    prev = jnp.where(rows == 0, carry_ref[0:1, :], pltpu.roll(zb, 1, 0))
    carry_ref[0:1, :] = zb[n_rows - 1:n_rows, :]
    r, lw, k, v, kk, b, g = _rwkv_prep(zb, prev, p)

    ti = lax.broadcasted_iota(jnp.int32, (c, MXU_LANES), 0)
    si = lax.broadcasted_iota(jnp.int32, (c, MXU_LANES), 1) % d
    lane_head = lax.broadcasted_iota(jnp.int32, (c, MXU_LANES), 1) // d
    incl, strict = ti >= si, ti > si
    eye = (ti == si).astype(F32)
    bd_mask = (lax.broadcasted_iota(jnp.int32, (MXU_LANES, MXU_LANES), 0) // c
               == lax.broadcasted_iota(jnp.int32, (MXU_LANES, MXU_LANES), 1) // d)
    tri = (lax.broadcasted_iota(jnp.int32, (c, c), 0) >= lax.broadcasted_iota(jnp.int32, (c, c), 1)).astype(BF16)
    bd = functools.partial(_head_block_diag, bd_mask=bd_mask)

    items = []
    for sub in range(n_sub):
        rs = slice(sub * c, (sub + 1) * c)
        lw_c = lw[rs]
        gcum = sum(jnp.dot(tri, part, preferred_element_type=F32) for part in _split3(lw_c))
        g_last = gcum[c - 1:c, :]
        e_neg = jnp.exp(-gcum)
        e_hat = jnp.exp(g_last - gcum)
        kap = (kk[rs] * jnp.exp(gcum - lw_c)).astype(BF16)
        rt = r[rs] * jnp.exp(gcum)
        kt, bt = (k[rs] * e_neg).astype(BF16), (b[rs] * e_neg).astype(BF16)
        khat, bhat = (k[rs] * e_hat).astype(BF16), (b[rs] * e_hat).astype(BF16)
        dec_last = jnp.exp(g_last)
        v_c = v[rs].astype(BF16)
        for g0 in range(0, B_WIDTH, MXU_LANES):
            gl = slice(g0, g0 + MXU_LANES)
            items.append(dict(g0=g0, kap=kap[:, gl], rt=rt[:, gl], kt=kt[:, gl], bt=bt[:, gl], khat=khat[:, gl],
                              bhat=bhat[:, gl], v=v_c[:, gl], dec=dec_last[:, gl]))
    for it in items:
        lhs = jnp.concatenate([it["kap"], it["rt"].astype(BF16)], axis=0)
        it["m_k"] = lax.dot_general(lhs, bd(it["kt"]), NT_DIMS, preferred_element_type=F32)
        it["m_b"] = lax.dot_general(lhs, bd(it["bt"]), NT_DIMS, preferred_element_type=F32)
    for it in items:
        it["a_k"] = jnp.where(strict, it["m_k"][:c], 0.0)
        it["p_k"] = jnp.where(incl, it["m_k"][c:], 0.0)
        it["p_b"] = jnp.where(incl, it["m_b"][c:], 0.0).astype(BF16)
        it["pw"] = -jnp.where(strict, it["m_b"][:c], 0.0)
        it["pw_bd"] = bd(it["pw"])
        it["t_inv"] = eye + it["pw"]
    span = 2
    while span < c:
        for it in items:
            it["pw"] = _mm(it["pw"], it["pw_bd"])
        for it in items:
            it["pw_bd"] = bd(it["pw"])
            it["t_inv"] = it["t_inv"] + _mm(it["t_inv"], it["pw_bd"])
        span *= 2
    for it in items:
        it["v_bd"] = bd(it["v"])
        it["akv"] = _mm(it["a_k"], it["v_bd"])
    for it in items:
        it["k_p"] = _mm(it["t_inv"], bd(it["kap"]))
        it["u0"] = _mm(it["t_inv"], bd(it["akv"]))
    for it in items:
        it["w_bd"] = jnp.where(bd_mask, lax.dot_general(it["k_p"].astype(BF16), it["bhat"], TN_DIMS,
                                                        preferred_element_type=F32), 0.0).astype(BF16)
        s1_full = lax.dot_general(jnp.concatenate([it["v"], it["u0"].astype(BF16)], axis=0),
                                  jnp.concatenate([it["khat"], -it["bhat"]], axis=0),
                                  TN_DIMS, preferred_element_type=F32)
        it["s1"] = sum(jnp.where(lane_head == h, s1_full[h * d:(h + 1) * d], 0.0) for h in range(HEADS_PER_GROUP))
        it["r_p"] = (it["rt"] - _mm(it["p_b"], bd(it["k_p"]))).astype(BF16)
        it["o0"] = _mm(it["p_k"], it["v_bd"]) - _mm(it["p_b"], bd(it["u0"]))
    state = {g0: state_ref[:, g0:g0 + MXU_LANES] for g0 in range(0, B_WIDTH, MXU_LANES)}
    o_parts = []
    for it in items:
        s0 = state[it["g0"]]
        o_parts.append(lax.dot_general(it["r_p"], bd(s0), NT_DIMS, preferred_element_type=F32) + it["o0"])
        state[it["g0"]] = s0 * it["dec"] - _mm(s0, it["w_bd"]) + it["s1"]
    for g0, s_new in state.items():
        state_ref[:, g0:g0 + MXU_LANES] = s_new
    n_groups = B_WIDTH // MXU_LANES
    o = jnp.concatenate([jnp.concatenate(o_parts[i:i + n_groups], axis=1)
                         for i in range(0, len(o_parts), n_groups)], axis=0)
    o_ref[...] = _rwkv_finish(o, r, k, v, g, p).astype(o_ref.dtype)
    for h in range(B_HEADS):
        s_out_ref[0, h] = state_ref[:, h * d:(h + 1) * d]


def _rwkv_prompt(zb, p, n_seq, t):
    c = WKV_CHUNK
    assert c == B_HEAD_DIM and t % c == 0, "head packing puts WKV_CHUNK time steps where a head's lanes go"
    n_sub = WKV_CHUNKS_PER_STEP if t % (c * WKV_CHUNKS_PER_STEP) == 0 else 1
    tm = c * n_sub
    nc = t // tm
    width = zb.shape[1]
    return pl.pallas_call(
        functools.partial(_rwkv_chunk_kernel, n_sub=n_sub),
        grid=(n_seq, nc),
        in_specs=[pl.BlockSpec((tm, width), lambda b, i: (b * nc + i, 0))] + _rwkv_param_specs(p),
        out_specs=[pl.BlockSpec((tm, B_WIDTH), lambda b, i: (b * nc + i, 0)),
                   pl.BlockSpec((1, B_HEADS, B_HEAD_DIM, B_HEAD_DIM), lambda b, i: (b, 0, 0, 0))],
        out_shape=[jax.ShapeDtypeStruct((n_seq * t, B_WIDTH), BF16),
                   jax.ShapeDtypeStruct((n_seq, B_HEADS, B_HEAD_DIM, B_HEAD_DIM), F32)],
        scratch_shapes=[pltpu.VMEM((8, width), F32),
                        pltpu.VMEM((B_HEAD_DIM, B_WIDTH), F32)],
        compiler_params=_params(2),
        name="rwkv_prompt",
    )(zb, *[p[name] for name in RWKV_PARAM_NAMES])


def _rwkv_sample_prep_kernel(zb_ref, shift_ref, *refs):
    n_p = len(RWKV_PARAM_NAMES)
    p = {name: ref[...] for name, ref in zip(RWKV_PARAM_NAMES, refs[:n_p])}
    vec_ref, g_ref = refs[n_p:]
    r, lw, k, v, kk, b, g = _rwkv_prep(zb_ref[...], shift_ref[...], p)
    for idx, val in enumerate((r, jnp.exp(lw), k, v, kk, b)):
        vec_ref[idx] = val
    g_ref[...] = g


def _rwkv_sample_prep(zb, shift, p):
    n, width = zb.shape
    return pl.pallas_call(
        _rwkv_sample_prep_kernel,
        grid=(1,),
        in_specs=[_const_spec((n, width)), _const_spec((n, width))] + _rwkv_param_specs(p),
        out_specs=[_const_spec((6, n, B_WIDTH)), _const_spec((n, B_WIDTH))],
        out_shape=[jax.ShapeDtypeStruct((6, n, B_WIDTH), F32), jax.ShapeDtypeStruct((n, B_WIDTH), F32)],
        compiler_params=_params(),
        name="rwkv_sample_prep",
    )(zb, shift, *[p[name] for name in RWKV_PARAM_NAMES])


def _wkv_step_kernel(s_ref, vec_ref, s_out_ref, o_ref, st_ref):
    d = B_HEAD_DIM
    st_ref[...] = s_ref[...].T
    r, w, k, v, kk, b = (vec_ref[i] for i in range(6))
    for i in range(d):
        rows = slice(i * d, (i + 1) * d)
        s_i = st_ref[rows, :]
        sa = -jnp.sum(s_i * kk, axis=0, keepdims=True)
        s_i = s_i * w + sa * b + v[i:i + 1, :] * k
        st_ref[rows, :] = s_i
        o_ref[i:i + 1, :] = jnp.sum(s_i * r, axis=0, keepdims=True)
    s_out_ref[...] = st_ref[...].T


def _wkv_step(state2d, vecs_t):
    pairs, dd = state2d.shape
    d = B_HEAD_DIM
    pb = min(128, pairs)
    return pl.pallas_call(
        _wkv_step_kernel,
        grid=(pairs // pb,),
        in_specs=[pl.BlockSpec((pb, dd), lambda i: (i, 0)), pl.BlockSpec((6, d, pb), lambda i: (0, 0, i))],
        out_specs=[pl.BlockSpec((pb, dd), lambda i: (i, 0)), pl.BlockSpec((d, pb), lambda i: (0, i))],
        out_shape=[jax.ShapeDtypeStruct((pairs, dd), F32), jax.ShapeDtypeStruct((d, pairs), F32)],
        scratch_shapes=[pltpu.VMEM((dd, pb), F32)],
        compiler_params=_params(),
        name="wkv_step",
    )(state2d, vecs_t)


def _rwkv_sample_finish_kernel(o_ref, vec_ref, g_ref, *refs):
    n_p = len(RWKV_PARAM_NAMES)
    p = {name: ref[...] for name, ref in zip(RWKV_PARAM_NAMES, refs[:n_p])}
    out_ref = refs[n_p]
    out_ref[...] = _rwkv_finish(o_ref[...], vec_ref[0], vec_ref[2], vec_ref[3], g_ref[...], p).astype(out_ref.dtype)


def _rwkv_sample_finish(o, vecs, g, p):
    n = o.shape[0]
    return pl.pallas_call(
        _rwkv_sample_finish_kernel,
        grid=(1,),
        in_specs=[_const_spec((n, B_WIDTH)), _const_spec((6, n, B_WIDTH)), _const_spec((n, B_WIDTH))]
        + _rwkv_param_specs(p),
        out_specs=_const_spec((n, B_WIDTH)),
        out_shape=jax.ShapeDtypeStruct((n, B_WIDTH), BF16),
        compiler_params=_params(),
        name="rwkv_sample_finish",
    )(o, vecs, g, *[p[name] for name in RWKV_PARAM_NAMES])


def _proj_residual_kernel(a_ref, b_ref, wa_ref, wb_ref, h_ref, g_ref, o_ref):
    acc = jnp.dot(a_ref[...], wa_ref[...], preferred_element_type=F32)
    acc = acc + jnp.dot(b_ref[...], wb_ref[...], preferred_element_type=F32)
    o_ref[...] = h_ref[...] + _rmsnorm(acc, g_ref[...])


def _proj_residual(a, b, wa, wb, h, g, tm):
    rows, d = h.shape
    tm = min(tm, rows)
    ka, kb = a.shape[1], b.shape[1]
    return pl.pallas_call(
        _proj_residual_kernel,
        grid=(rows // tm,),
        in_specs=[pl.BlockSpec((tm, ka), lambda i: (i, 0)), pl.BlockSpec((tm, kb), lambda i: (i, 0)),
                  _const_spec((ka, d)), _const_spec((kb, d)),
                  pl.BlockSpec((tm, d), lambda i: (i, 0)), _const_spec((1, d))],
        out_specs=pl.BlockSpec((tm, d), lambda i: (i, 0)),
        out_shape=jax.ShapeDtypeStruct((rows, d), F32),
        compiler_params=_params(),
        name="proj_residual",
    )(a, b, wa, wb, h, g)


def _gelu_erf(x):
    return 0.5 * x * (1.0 + lax.erf(x * 0.7071067811865476))


def _gelu_tanh(x):
    return 0.5 * x * (1.0 + jnp.tanh(0.7978845608028654 * (x + 0.044715 * (x * x * x))))


def _layernorm(x, g, b):
    mu = jnp.mean(x, axis=-1, keepdims=True)
    d = x - mu
    var = jnp.mean(d * d, axis=-1, keepdims=True)
    return d * lax.rsqrt(var + LN_EPS) * g + b


def _gmlp_kernel(h_ref, gpre_ref, win_ref, lng_ref, lnb_ref, ws_ref, bs_ref, wout_ref, gpost_ref,
                 o_ref, *v_refs, chunk):
    x = h_ref[...]
    tm = x.shape[0]
    width = wout_ref.shape[0]
    hd = width // C_HEADS
    xn = _rmsnorm(x, gpre_ref[...])
    z = _gelu_erf(jnp.dot(xn.astype(BF16), win_ref[...], preferred_element_type=F32))
    u = z[:, :width]
    v = _layernorm(z[:, width:], lng_ref[...], lnb_ref[...])
    if chunk == 1:
        mixed = v * ws_ref[...] + bs_ref[...]
        v_refs[0][...] = v
    else:
        ri = lax.broadcasted_iota(jnp.int32, (chunk, chunk), 0)
        ci = lax.broadcasted_iota(jnp.int32, (chunk, chunk), 1)
        vb = v.astype(BF16)
        bias = bs_ref[...]
        cols = []
        for hh in range(C_HEADS):
            wtri = jnp.where(ri >= ci, ws_ref[hh], 0.0).astype(BF16)
            parts = [jnp.dot(wtri, vb[c0:c0 + chunk, hh * hd:(hh + 1) * hd], preferred_element_type=F32)
                     + bias[:, hh:hh + 1] for c0 in range(0, tm, chunk)]
            cols.append(jnp.concatenate(parts, axis=0) if len(parts) > 1 else parts[0])
        mixed = jnp.concatenate(cols, axis=1)
    y = u * mixed
    out = jnp.dot(y.astype(BF16), wout_ref[...], preferred_element_type=F32)
    o_ref[...] = x + _rmsnorm(out, gpost_ref[...])


def _gmlp(h, gpre, win, lng, lnb, ws, bs, wout, gpost, chunk, tm):
    rows, d = h.shape
    tm = min(tm, rows)
    width = wout.shape[0]
    out_specs = [pl.BlockSpec((tm, d), lambda i: (i, 0))]
    out_shape = [jax.ShapeDtypeStruct((rows, d), F32)]
    if chunk == 1:
        out_specs.append(pl.BlockSpec((tm, width), lambda i: (i, 0)))
        out_shape.append(jax.ShapeDtypeStruct((rows, width), F32))
    return pl.pallas_call(
        functools.partial(_gmlp_kernel, chunk=chunk),
        grid=(rows // tm,),
        in_specs=[pl.BlockSpec((tm, d), lambda i: (i, 0)), _const_spec((1, d)), _const_spec(win.shape),
                  _const_spec((1, width)), _const_spec((1, width)), _const_spec(ws.shape),
                  _const_spec(bs.shape), _const_spec(wout.shape), _const_spec((1, d))],
        out_specs=out_specs,
        out_shape=out_shape,
        compiler_params=_params(),
        name="gmlp",
    )(h, gpre, win, lng, lnb, ws, bs, wout, gpost)


def _ffn_kernel(h_ref, gpre_ref, wg_ref, wu_ref, cw_ref, cb_ref, wd_ref, gpost_ref, *refs,
                carried, tiles_per_seq):
    x = h_ref[...]
    tm = x.shape[0]
    xn = _rmsnorm(x, gpre_ref[...]).astype(BF16)
    gp = jnp.dot(xn, wg_ref[...], preferred_element_type=F32)
    up = jnp.dot(xn, wu_ref[...], preferred_element_type=F32)
    if carried:
        o_ref, tail_ref, carry_ref = refs

        @pl.when(pl.program_id(0) % tiles_per_seq == 0)
        def _():
            carry_ref[...] = jnp.zeros_like(carry_ref)

        rows = lax.broadcasted_iota(jnp.int32, gp.shape, 0)
        c2, c1 = carry_ref[0:1, :], carry_ref[1:2, :]
        x1 = jnp.where(rows == 0, c1, pltpu.roll(gp, 1, 0))
        x2 = jnp.where(rows == 0, c2, jnp.where(rows == 1, c1, pltpu.roll(gp, 2, 0)))
        carry_ref[0:2, :] = gp[tm - 2:tm, :]
        tail_ref[0] = gp[tm - 2:tm, :]
    else:
        s2_ref, s1_ref, o_ref, gp_ref = refs
        x2, x1 = s2_ref[...], s1_ref[...]
        gp_ref[...] = gp
    cw = cw_ref[...]
    conv = cb_ref[...] + cw[2:3, :] * gp + cw[0:1, :] * x2 + cw[1:2, :] * x1
    hid = _gelu_tanh(conv) * up
    out = jnp.dot(hid.astype(BF16), wd_ref[...], preferred_element_type=F32)
    o_ref[...] = x + _rmsnorm(out, gpost_ref[...])


def _ffn(h, gpre, wg, wu, cw, cb, wd, gpost, tm, seq_len=None, state=None):
    rows, d = h.shape
    f = wg.shape[1]
    tm = min(tm, rows)
    carried = state is None
    row_spec = pl.BlockSpec((tm, d), lambda i: (i, 0))
    in_specs = [row_spec, _const_spec((1, d)), _const_spec((d, f)), _const_spec((d, f)),
                _const_spec(cw.shape), _const_spec((1, f)), _const_spec((f, d)), _const_spec((1, d))]
    args = [h, gpre, wg, wu, cw, cb, wd, gpost]
    if carried:
        tm = min(tm, seq_len)
        tiles_per_seq = seq_len // tm
        row_spec = pl.BlockSpec((tm, d), lambda i: (i, 0))
        in_specs[0] = row_spec
        out_specs = [row_spec, pl.BlockSpec((1, 2, f), lambda i: (i // tiles_per_seq, 0, 0))]
        out_shape = [jax.ShapeDtypeStruct((rows, d), F32),
                     jax.ShapeDtypeStruct((rows // seq_len, 2, f), F32)]
        scratch = [pltpu.VMEM((8, f), F32)]
    else:
        tiles_per_seq = 1
        in_specs += [pl.BlockSpec((tm, f), lambda i: (i, 0))] * 2
        args += [state[:, 0], state[:, 1]]
        out_specs = [row_spec, pl.BlockSpec((tm, f), lambda i: (i, 0))]
        out_shape = [jax.ShapeDtypeStruct((rows, d), F32), jax.ShapeDtypeStruct((rows, f), F32)]
        scratch = []
    return pl.pallas_call(
        functools.partial(_ffn_kernel, carried=carried, tiles_per_seq=tiles_per_seq),
        grid=(rows // tm,),
        in_specs=in_specs,
        out_specs=out_specs,
        out_shape=out_shape,
        scratch_shapes=scratch,
        compiler_params=_params(),
        name="conv_ffn",
    )(*args)


def kernel(x_prompt, x_sample, cache_win_k, cache_win_v, state_wkv, state_shift, state_ffn_conv,
           norm_mix_pre, norm_mix_post, norm_ffn_pre, norm_ffn_post,
           w_in_even, attn_sinks, shift_mu, decay_w0, decay_w2, iclr_a0, iclr_a2, gate_g2,
           key_k, key_a, bonus_r_k, gn_gain, gn_bias, w_out_even,
           w_in_odd, sgu_ln_gain, sgu_ln_bias, sgu_w, sgu_b, w_out_odd,
           ffn_w_gate, ffn_w_up, ffn_conv_w, ffn_conv_b, ffn_w_down):
    n_seq, t, d = x_prompt.shape
    n_dec, dec_t, _ = x_sample.shape
    assert dec_t == 1, "the sample path is written for one new token per sequence"
    depth = norm_mix_pre.shape[0]
    wp = min(WINDOW, t)
    row = lambda a: a.reshape(1, -1)

    hp = x_prompt.reshape(n_seq * t, d)
    hs = x_sample.reshape(n_dec, d)
    cos_p, sin_p = _rope_tables(jnp.arange(t, dtype=jnp.int32))
    cos_s, sin_s = _rope_tables(PAST_LEN + jnp.arange(1, dtype=jnp.int32))
    seg = jnp.kron(jnp.eye(HEADS_PER_GROUP, dtype=F32), jnp.ones((B_HEAD_DIM, B_HEAD_DIM), F32)).astype(BF16)

    kp_l, vp_l, ks_l, vs_l, sp_l, ss_l, shp_l, shs_l, vsgu_l, cp_l, cs_l = ([] for _ in range(11))
    for layer in range(depth):
        j = layer // 2
        g_pre, g_post = row(norm_mix_pre[layer]), row(norm_mix_post[layer])
        if layer % 2 == 0:
            w_in = w_in_even[j].astype(BF16)
            w_out = w_out_even[j].astype(BF16)
            sinks = row(attn_sinks[j])
            p = {"mu": row(shift_mu[j]), "w0": row(decay_w0[j]), "w2": decay_w2[j].astype(BF16),
                 "a0": row(iclr_a0[j]), "a2": iclr_a2[j].astype(BF16), "g2": gate_g2[j].astype(BF16),
                 "k_k": row(key_k[j]), "k_a": row(key_a[j]), "r_k": row(bonus_r_k[j]),
                 "gn_g": row(gn_gain[j]), "gn_b": row(gn_bias[j]), "seg": seg}
            za, zb = _norm_proj(hp, g_pre, w_in, (A_PROJ, w_in.shape[1] - A_PROJ), tm=512)
            oa, kp, vp = _attn_prompt(za, cos_p, sin_p, sinks, n_seq, t)
            ob, sp = _rwkv_prompt(zb, p, n_seq, t)
            hp = _proj_residual(oa, ob, w_out[:A_Q], w_out[A_Q:], hp, g_post, tm=512)
            kp_l.append(kp.reshape(n_seq, wp, A_KV_HEADS, HEAD_DIM))
            vp_l.append(vp.reshape(n_seq, wp, A_KV_HEADS, HEAD_DIM))
            sp_l.append(sp)
            shp_l.append(zb.reshape(n_seq, t, -1)[:, -1])
            za, zb = _norm_proj(hs, g_pre, w_in, (A_PROJ, w_in.shape[1] - A_PROJ), tm=512)
            wb = cache_win_k.shape[2]
            oa, kn, vn = _attn_sample(za, cache_win_k[j].reshape(n_dec, wb, A_KV),
                                      cache_win_v[j].reshape(n_dec, wb, A_KV), cos_s, sin_s, sinks)
            vecs, gate = _rwkv_sample_prep(zb, state_shift[j], p)
            pairs = n_dec * B_HEADS
            vecs_t = vecs.reshape(6, pairs, B_HEAD_DIM).transpose(0, 2, 1)
            s_new, o_t = _wkv_step(state_wkv[j].reshape(pairs, B_HEAD_DIM * B_HEAD_DIM), vecs_t)
            ob = _rwkv_sample_finish(o_t.T.reshape(n_dec, B_WIDTH), vecs, gate, p)
            hs = _proj_residual(oa, ob, w_out[:A_Q], w_out[A_Q:], hs, g_post, tm=512)
            ks_l.append(kn.reshape(n_dec, 1, A_KV_HEADS, HEAD_DIM))
            vs_l.append(vn.reshape(n_dec, 1, A_KV_HEADS, HEAD_DIM))
            ss_l.append(s_new.reshape(n_dec, B_HEADS, B_HEAD_DIM, B_HEAD_DIM))
            shs_l.append(zb)
        else:
            w_in = w_in_odd[j].astype(BF16)
            w_out = w_out_odd[j].astype(BF16)
            lng, lnb = row(sgu_ln_gain[j]), row(sgu_ln_bias[j])
            chunk = min(t, CHUNK)
            hd = w_out.shape[0] // C_HEADS
            hp, = _gmlp(hp, g_pre, w_in, lng, lnb, sgu_w[j][:, :chunk, :chunk], sgu_b[j][:, :chunk].T,
                        w_out, g_post, chunk=chunk, tm=2 * chunk)
            hs, vsg = _gmlp(hs, g_pre, w_in, lng, lnb, row(jnp.repeat(sgu_w[j][:, 0, 0], hd)),
                            row(jnp.repeat(sgu_b[j][:, 0], hd)), w_out, g_post, chunk=1, tm=512)
            vsgu_l.append(vsg.reshape(n_dec, 1, -1))
        f_pre, f_post = row(norm_ffn_pre[layer]), row(norm_ffn_post[layer])
        wg, wu, wd = (w[layer].astype(BF16) for w in (ffn_w_gate, ffn_w_up, ffn_w_down))
        cw, cb = ffn_conv_w[layer], row(ffn_conv_b[layer])
        hp, cp = _ffn(hp, f_pre, wg, wu, cw, cb, wd, f_post, tm=256, seq_len=t)
        hs, gp = _ffn(hs, f_pre, wg, wu, cw, cb, wd, f_post, tm=256, state=state_ffn_conv[layer])
        cp_l.append(cp)
        cs_l.append(jnp.stack([state_ffn_conv[layer][:, 1], gp], axis=1))

    return (hp.reshape(n_seq, t, d), hs.reshape(n_dec, 1, d),
            jnp.stack(kp_l), jnp.stack(vp_l), jnp.stack(ks_l), jnp.stack(vs_l),
            jnp.stack(sp_l), jnp.stack(ss_l), jnp.stack(shp_l), jnp.stack(shs_l),
            jnp.stack(vsgu_l), jnp.stack(cp_l), jnp.stack(cs_l))
```

```python
import functools

import jax
import jax.numpy as jnp
from jax import lax
from jax.experimental import pallas as pl
from jax.experimental.pallas import tpu as pltpu

F32 = jnp.float32
BF16 = jnp.bfloat16

PAST_LEN = 8192
HEAD_DIM = 64
A_HEADS = 8
A_KV_HEADS = 2
A_GROUP = A_HEADS // A_KV_HEADS
WINDOW = 128
ROPE_THETA = 10000.0
B_HEAD_DIM = 64
B_HEADS = 8
B_WIDTH = B_HEADS * B_HEAD_DIM
DECAY_RANK = 64
ICLR_RANK = 64
DECAY_SCALE = 0.606531
GN_EPS = 64e-5
CHUNK = 128
C_HEADS = 8
RMS_EPS = 1e-6
LN_EPS = 1e-5
NEG = -1e30
A_Q = A_HEADS * HEAD_DIM
A_KV = A_KV_HEADS * HEAD_DIM
A_PROJ = A_Q + 2 * A_KV

WKV_CHUNK = 64
WKV_CHUNKS_PER_STEP = 4
ATTN_BLOCKS_PER_STEP = 2
FFN_COL_BLOCK = 256
MXU_LANES = 256
HEADS_PER_GROUP = MXU_LANES // B_HEAD_DIM
V7X_VMEM_LIMIT_BYTES = 56 * 1024 * 1024

NT_DIMS = (((1,), (1,)), ((), ()))
TN_DIMS = (((0,), (0,)), ((), ()))


def _params(n_axes=1):
    return pltpu.CompilerParams(dimension_semantics=("arbitrary",) * n_axes,
                                vmem_limit_bytes=V7X_VMEM_LIMIT_BYTES)


def _rmsnorm(x, g):
    return x * lax.rsqrt(jnp.mean(x * x, axis=-1, keepdims=True) + RMS_EPS) * g


def _bdot(a, b):
    return jnp.dot(a.astype(BF16), b.astype(BF16), preferred_element_type=F32)


def _const_spec(shape):
    return pl.BlockSpec(shape, lambda *_: (0,) * len(shape))


def _norm_proj_kernel(x_ref, g_ref, w_ref, *out_refs, splits):
    xn = _rmsnorm(x_ref[...], g_ref[...])
    z = jnp.dot(xn.astype(BF16), w_ref[...], preferred_element_type=F32)
    off = 0
    for o_ref, n in zip(out_refs, splits):
        o_ref[...] = z[:, off:off + n]
        off += n


def _norm_proj(x, g, w, splits, tm):
    rows, d = x.shape
    n = w.shape[1]
    tm = min(tm, rows)
    return pl.pallas_call(
        functools.partial(_norm_proj_kernel, splits=splits),
        grid=(rows // tm,),
        in_specs=[pl.BlockSpec((tm, d), lambda i: (i, 0)), _const_spec((1, d)), _const_spec((d, n))],
        out_specs=[pl.BlockSpec((tm, s), lambda i: (i, 0)) for s in splits],
        out_shape=[jax.ShapeDtypeStruct((rows, s), F32) for s in splits],
        compiler_params=_params(),
        name="norm_proj",
    )(x, g, w)


def _rope_tables(pos):
    half = HEAD_DIM // 2
    inv = ROPE_THETA ** (-jnp.arange(half, dtype=F32) / half)
    ang = pos.astype(F32)[:, None] * inv[None, :]
    cos, sin = jnp.cos(ang), jnp.sin(ang)
    cos2 = jnp.concatenate([cos, cos], axis=-1)
    sin2 = jnp.concatenate([-sin, sin], axis=-1)
    return jnp.tile(cos2, (1, 2)), jnp.tile(sin2, (1, 2))


def _rope(x, cos, sin):
    width = x.shape[1]
    half = HEAD_DIM // 2
    lane = lax.broadcasted_iota(jnp.int32, x.shape, 1)
    upper = pltpu.roll(x, width - half, 1)
    lower = pltpu.roll(x, half, 1)
    rot = jnp.where((lane % HEAD_DIM) < half, upper, lower)
    return x * cos + rot * sin


def _tile_lanes(t, reps):
    return jnp.concatenate([t] * reps, axis=1)


def _attn_prompt_kernel(cur_ref, prev_ref, cos_ref, sin_ref, cosp_ref, sinp_ref, sink_ref,
                        o_ref, k_out_ref, v_out_ref):
    w = prev_ref.shape[0]
    n_blk = cur_ref.shape[0] // w
    cur = cur_ref[...]
    prev = prev_ref[...]
    cos, sin = cos_ref[...], sin_ref[...]
    q = (_rope(cur[:, :A_Q], _tile_lanes(cos, A_HEADS // 2), _tile_lanes(sin, A_HEADS // 2))
         * (HEAD_DIM ** -0.5)).astype(BF16)
    k = _rope(cur[:, A_Q:A_Q + A_KV], cos, sin)
    v = cur[:, A_Q + A_KV:A_PROJ]
    kp = _rope(prev[:, A_Q:A_Q + A_KV], cosp_ref[...], sinp_ref[...])
    vp = prev[:, A_Q + A_KV:A_PROJ]
    k_out_ref[0] = k[(n_blk - 1) * w:, :]
    v_out_ref[0] = v[(n_blk - 1) * w:, :]

    qi = lax.broadcasted_iota(jnp.int32, (w, 2 * w), 0)
    kj = lax.broadcasted_iota(jnp.int32, (w, 2 * w), 1)
    diff = w + qi - kj
    band = (diff >= 0) & (diff <= WINDOW)
    first_band = band & ((pl.program_id(1) > 0) | (kj >= w))
    kk_all = jnp.concatenate([kp, k], axis=0).astype(BF16)
    vv_all = jnp.concatenate([vp, v], axis=0).astype(BF16)

    items = [(j, h) for j in range(n_blk) for h in range(A_HEADS)]
    kv_lanes = lambda h: slice((h // A_GROUP) * HEAD_DIM, (h // A_GROUP + 1) * HEAD_DIM)
    s = [lax.dot_general(q[j * w:(j + 1) * w, h * HEAD_DIM:(h + 1) * HEAD_DIM],
                         kk_all[j * w:(j + 2) * w, kv_lanes(h)], NT_DIMS, preferred_element_type=F32)
         for j, h in items]
    s = [jnp.where(first_band if j == 0 else band, x, NEG) for (j, h), x in zip(items, s)]
    sink = [sink_ref[:, h:h + 1] for j, h in items]
    m = [jnp.maximum(jnp.max(x, axis=-1, keepdims=True), sk) for x, sk in zip(s, sink)]
    p = [jnp.exp(x - mx) for x, mx in zip(s, m)]
    den = [jnp.sum(x, axis=-1, keepdims=True) + jnp.exp(sk - mx) for x, sk, mx in zip(p, sink, m)]
    o = [jnp.dot(x.astype(BF16), vv_all[j * w:(j + 2) * w, kv_lanes(h)], preferred_element_type=F32)
         for (j, h), x in zip(items, p)]
    o = [x / dn for x, dn in zip(o, den)]
    o_ref[...] = jnp.concatenate(
        [jnp.concatenate(o[j * A_HEADS:(j + 1) * A_HEADS], axis=1) for j in range(n_blk)], axis=0).astype(o_ref.dtype)


def _attn_prompt(za, cos, sin, sinks, n_seq, t):
    w = min(WINDOW, t)
    n_blk = ATTN_BLOCKS_PER_STEP if t % (w * ATTN_BLOCKS_PER_STEP) == 0 else 1
    tm = n_blk * w
    nt = t // tm
    cur_map = lambda b, i: (b * nt + i, 0)
    prev_map = lambda b, i: ((b * nt + i) * n_blk - jnp.minimum(i, 1), 0)
    tab_cur = lambda b, i: (i, 0)
    tab_prev = lambda b, i: (i * n_blk - jnp.minimum(i, 1), 0)
    return pl.pallas_call(
        _attn_prompt_kernel,
        grid=(n_seq, nt),
        in_specs=[pl.BlockSpec((tm, A_PROJ), cur_map), pl.BlockSpec((w, A_PROJ), prev_map),
                  pl.BlockSpec((tm, A_KV), tab_cur), pl.BlockSpec((tm, A_KV), tab_cur),
                  pl.BlockSpec((w, A_KV), tab_prev), pl.BlockSpec((w, A_KV), tab_prev),
                  _const_spec((1, A_HEADS))],
        out_specs=[pl.BlockSpec((tm, A_Q), cur_map),
                   pl.BlockSpec((1, w, A_KV), lambda b, i: (b, 0, 0)),
                   pl.BlockSpec((1, w, A_KV), lambda b, i: (b, 0, 0))],
        out_shape=[jax.ShapeDtypeStruct((n_seq * t, A_Q), BF16),
                   jax.ShapeDtypeStruct((n_seq, w, A_KV), F32),
                   jax.ShapeDtypeStruct((n_seq, w, A_KV), F32)],
        compiler_params=_params(2),
        name="attn_prompt",
    )(za, za, cos, sin, cos, sin, sinks)


def _attn_sample_kernel(za_ref, ck_ref, cv_ref, cos_ref, sin_ref, sink_ref, o_ref, kn_ref, vn_ref):
    sb, wb = ck_ref.shape[0], ck_ref.shape[1]
    za = za_ref[...]
    cos, sin = cos_ref[...], sin_ref[...]
    q = _rope(za[:, :A_Q], _tile_lanes(cos, A_HEADS // 2), _tile_lanes(sin, A_HEADS // 2))
    k = _rope(za[:, A_Q:A_Q + A_KV], cos, sin)
    v = za[:, A_Q + A_KV:A_PROJ]
    kn_ref[...] = k
    vn_ref[...] = v
    ck = ck_ref[...].reshape(sb * wb, A_KV).astype(BF16)
    cv = cv_ref[...].reshape(sb * wb, A_KV).astype(BF16)
    row = lax.broadcasted_iota(jnp.int32, (sb, sb * wb), 0)
    col = lax.broadcasted_iota(jnp.int32, (sb, sb * wb), 1)
    own = (col >= row * wb) & (col < (row + 1) * wb)
    outs = []
    for h in range(A_HEADS):
        kv = h // A_GROUP
        sl = slice(kv * HEAD_DIM, (kv + 1) * HEAD_DIM)
        qh = q[:, h * HEAD_DIM:(h + 1) * HEAD_DIM]
        s = lax.dot_general(qh.astype(BF16), ck[:, sl], NT_DIMS, preferred_element_type=F32)
        s = jnp.where(own, s * (HEAD_DIM ** -0.5), NEG)
        s_new = jnp.sum(qh * k[:, sl], axis=-1, keepdims=True) * (HEAD_DIM ** -0.5)
        sink = sink_ref[:, h:h + 1]
        m = jnp.maximum(jnp.maximum(jnp.max(s, axis=-1, keepdims=True), s_new), sink)
        p = jnp.exp(s - m)
        p_new = jnp.exp(s_new - m)
        den = jnp.sum(p, axis=-1, keepdims=True) + p_new + jnp.exp(sink - m)
        o = jnp.dot((p / den).astype(BF16), cv[:, sl], preferred_element_type=F32)
        outs.append(o + (p_new / den) * v[:, sl])
    o_ref[...] = jnp.concatenate(outs, axis=1).astype(o_ref.dtype)


def _attn_sample(za, cache_k, cache_v, cos, sin, sinks):
    n, wb = cache_k.shape[0], cache_k.shape[1]
    sb = min(16, n)
    return pl.pallas_call(
        _attn_sample_kernel,
        grid=(n // sb,),
        in_specs=[pl.BlockSpec((sb, A_PROJ), lambda i: (i, 0)),
                  pl.BlockSpec((sb, wb, A_KV), lambda i: (i, 0, 0)),
                  pl.BlockSpec((sb, wb, A_KV), lambda i: (i, 0, 0)),
                  _const_spec((1, A_KV)), _const_spec((1, A_KV)), _const_spec((1, A_HEADS))],
        out_specs=[pl.BlockSpec((sb, A_Q), lambda i: (i, 0)),
                   pl.BlockSpec((sb, A_KV), lambda i: (i, 0)),
                   pl.BlockSpec((sb, A_KV), lambda i: (i, 0))],
        out_shape=[jax.ShapeDtypeStruct((n, A_Q), BF16),
                   jax.ShapeDtypeStruct((n, A_KV), F32),
                   jax.ShapeDtypeStruct((n, A_KV), F32)],
        compiler_params=_params(),
        name="attn_sample",
    )(za, cache_k, cache_v, cos, sin, sinks)


RWKV_PARAM_NAMES = ("mu", "w0", "w2", "a0", "a2", "g2", "k_k", "k_a", "r_k", "gn_g", "gn_b", "seg")


def _rwkv_param_specs(p):
    return [_const_spec(p[name].shape) for name in RWKV_PARAM_NAMES]


def _rwkv_prep(zb, prev, p):
    o1, o2, o3 = B_WIDTH, 2 * B_WIDTH, 3 * B_WIDTH
    o4, o5 = o3 + DECAY_RANK, o3 + DECAY_RANK + ICLR_RANK
    zs = zb + (prev - zb) * p["mu"]
    r, k, v = zs[:, :o1], zs[:, o1:o2], zs[:, o2:o3]
    wd, ad, gd = zs[:, o3:o4], zs[:, o4:o5], zs[:, o5:]
    lw = -DECAY_SCALE * jax.nn.sigmoid(p["w0"] + _bdot(jnp.tanh(wd), p["w2"]))
    a = jax.nn.sigmoid(p["a0"] + _bdot(ad, p["a2"]))
    g = _bdot(jax.nn.sigmoid(gd), p["g2"])
    kk = k * p["k_k"]
    kk = kk / jnp.maximum(jnp.sqrt(_segsum(kk * kk, p["seg"])), 1e-12)
    k = k * (1.0 + (a - 1.0) * p["k_a"])
    return r, lw, k, v, kk, kk * a, g


def _split2(x):
    hi = x.astype(BF16)
    return hi, (x - hi.astype(F32)).astype(BF16)


def _segsum(x, seg):
    outs = []
    for g0 in range(0, x.shape[1], MXU_LANES):
        hi, lo = _split2(x[:, g0:g0 + MXU_LANES])
        outs.append(jnp.dot(hi, seg, preferred_element_type=F32) + jnp.dot(lo, seg, preferred_element_type=F32))
    return jnp.concatenate(outs, axis=1)


def _rwkv_finish(o, r, k, v, g, p):
    seg = p["seg"]
    mean = _segsum(o, seg) * (1.0 / B_HEAD_DIM)
    d = o - mean
    var = _segsum(d * d, seg) * (1.0 / B_HEAD_DIM)
    o = d * lax.rsqrt(var + GN_EPS) * p["gn_g"] + p["gn_b"]
    bonus = _segsum(r * k * p["r_k"], seg) * v
    return (o + bonus) * g


def _split3(x):
    hi = x.astype(BF16)
    r1 = x - hi.astype(F32)
    mid = r1.astype(BF16)
    return hi, mid, (r1 - mid.astype(F32)).astype(BF16)


def _mm(a, b):
    return jnp.dot(a.astype(BF16), b.astype(BF16), preferred_element_type=F32)


def _head_block_diag(x, bd_mask):
    tiled = jnp.concatenate([x.astype(BF16)] * HEADS_PER_GROUP, axis=0)
    return jnp.where(bd_mask, tiled, jnp.zeros_like(tiled))


def _rwkv_chunk_kernel(zb_ref, *refs, n_sub):
    n_p = len(RWKV_PARAM_NAMES)
    p = {name: ref[...] for name, ref in zip(RWKV_PARAM_NAMES, refs[:n_p])}
    o_ref, s_out_ref, carry_ref, state_ref = refs[n_p:]
    n_rows = zb_ref.shape[0]
    c = n_rows // n_sub
    d = B_HEAD_DIM

    @pl.when(pl.program_id(1) == 0)
    def _():
        carry_ref[...] = jnp.zeros_like(carry_ref)
        state_ref[...] = jnp.zeros_like(state_ref)

    zb = zb_ref[...]
    rows = lax.broadcasted_iota(jnp.int32, zb.shape, 0)
    prev = jnp.where(rows == 0, carry_ref[0:1, :], pltpu.roll(zb, 1, 0))
    carry_ref[0:1, :] = zb[n_rows - 1:n_rows, :]
    r, lw, k, v, kk, b, g = _rwkv_prep(zb, prev, p)

    ti = lax.broadcasted_iota(jnp.int32, (c, MXU_LANES), 0)
    si = lax.broadcasted_iota(jnp.int32, (c, MXU_LANES), 1) % d
    lane_head = lax.broadcasted_iota(jnp.int32, (c, MXU_LANES), 1) // d
    incl, strict = ti >= si, ti > si
    eye = (ti == si).astype(F32)
    bd_mask = (lax.broadcasted_iota(jnp.int32, (MXU_LANES, MXU_LANES), 0) // c
               == lax.broadcasted_iota(jnp.int32, (MXU_LANES, MXU_LANES), 1) // d)
    tri = (lax.broadcasted_iota(jnp.int32, (c, c), 0) >= lax.broadcasted_iota(jnp.int32, (c, c), 1)).astype(BF16)
    bd = functools.partial(_head_block_diag, bd_mask=bd_mask)

    items = []
    for sub in range(n_sub):
        rs = slice(sub * c, (sub + 1) * c)
        lw_c = lw[rs]
        gcum = sum(jnp.dot(tri, part, preferred_element_type=F32) for part in _split3(lw_c))
        g_last = gcum[c - 1:c, :]
        e_neg = jnp.exp(-gcum)
        e_hat = jnp.exp(g_last - gcum)
        kap = (kk[rs] * jnp.exp(gcum - lw_c)).astype(BF16)
        rt = r[rs] * jnp.exp(gcum)
        kt, bt = (k[rs] * e_neg).astype(BF16), (b[rs] * e_neg).astype(BF16)
        khat, bhat = (k[rs] * e_hat).astype(BF16), (b[rs] * e_hat).astype(BF16)
        dec_last = jnp.exp(g_last)
        v_c = v[rs].astype(BF16)
        for g0 in range(0, B_WIDTH, MXU_LANES):
            gl = slice(g0, g0 + MXU_LANES)
            items.append(dict(g0=g0, kap=kap[:, gl], rt=rt[:, gl], kt=kt[:, gl], bt=bt[:, gl], khat=khat[:, gl],
                              bhat=bhat[:, gl], v=v_c[:, gl], dec=dec_last[:, gl]))
    for it in items:
        lhs = jnp.concatenate([it["kap"], it["rt"].astype(BF16)], axis=0)
        it["m_k"] = lax.dot_general(lhs, bd(it["kt"]), NT_DIMS, preferred_element_type=F32)
        it["m_b"] = lax.dot_general(lhs, bd(it["bt"]), NT_DIMS, preferred_element_type=F32)
    for it in items:
        it["a_k"] = jnp.where(strict, it["m_k"][:c], 0.0)
        it["p_k"] = jnp.where(incl, it["m_k"][c:], 0.0)
        it["p_b"] = jnp.where(incl, it["m_b"][c:], 0.0).astype(BF16)
        it["pw"] = -jnp.where(strict, it["m_b"][:c], 0.0)
        it["pw_bd"] = bd(it["pw"])
        it["t_inv"] = eye + it["pw"]
    span = 2
    while span < c:
        for it in items:
            it["pw"] = _mm(it["pw"], it["pw_bd"])
        for it in items:
            it["pw_bd"] = bd(it["pw"])
            it["t_inv"] = it["t_inv"] + _mm(it["t_inv"], it["pw_bd"])
        span *= 2
    for it in items:
        it["v_bd"] = bd(it["v"])
        it["akv"] = _mm(it["a_k"], it["v_bd"])
    for it in items:
        it["k_p"] = _mm(it["t_inv"], bd(it["kap"]))
        it["u0"] = _mm(it["t_inv"], bd(it["akv"]))
    for it in items:
        it["w_bd"] = jnp.where(bd_mask, lax.dot_general(it["k_p"].astype(BF16), it["bhat"], TN_DIMS,
                                                        preferred_element_type=F32), 0.0).astype(BF16)
        s1_full = lax.dot_general(jnp.concatenate([it["v"], it["u0"].astype(BF16)], axis=0),
                                  jnp.concatenate([it["khat"], -it["bhat"]], axis=0),
                                  TN_DIMS, preferred_element_type=F32)
        it["s1"] = sum(jnp.where(lane_head == h, s1_full[h * d:(h + 1) * d], 0.0) for h in range(HEADS_PER_GROUP))
        it["r_p"] = (it["rt"] - _mm(it["p_b"], bd(it["k_p"]))).astype(BF16)
        it["o0"] = _mm(it["p_k"], it["v_bd"]) - _mm(it["p_b"], bd(it["u0"]))
    state = {g0: state_ref[:, g0:g0 + MXU_LANES] for g0 in range(0, B_WIDTH, MXU_LANES)}
    o_parts = []
    for it in items:
        s0 = state[it["g0"]]
        o_parts.append(lax.dot_general(it["r_p"], bd(s0), NT_DIMS, preferred_element_type=F32) + it["o0"])
        state[it["g0"]] = s0 * it["dec"] - _mm(s0, it["w_bd"]) + it["s1"]
    for g0, s_new in state.items():
        state_ref[:, g0:g0 + MXU_LANES] = s_new
    n_groups = B_WIDTH // MXU_LANES
    o = jnp.concatenate([jnp.concatenate(o_parts[i:i + n_groups], axis=1)
                         for i in range(0, len(o_parts), n_groups)], axis=0)
    o_ref[...] = _rwkv_finish(o, r, k, v, g, p).astype(o_ref.dtype)
    for h in range(B_HEADS):
        s_out_ref[0, h] = state_ref[:, h * d:(h + 1) * d]


def _rwkv_prompt(zb, p, n_seq, t):
    c = WKV_CHUNK
    assert c == B_HEAD_DIM and t % c == 0, "head packing puts WKV_CHUNK time steps where a head's lanes go"
    n_sub = WKV_CHUNKS_PER_STEP if t % (c * WKV_CHUNKS_PER_STEP) == 0 else 1
    tm = c * n_sub
    nc = t // tm
    width = zb.shape[1]
    return pl.pallas_call(
        functools.partial(_rwkv_chunk_kernel, n_sub=n_sub),
        grid=(n_seq, nc),
        in_specs=[pl.BlockSpec((tm, width), lambda b, i: (b * nc + i, 0))] + _rwkv_param_specs(p),
        out_specs=[pl.BlockSpec((tm, B_WIDTH), lambda b, i: (b * nc + i, 0)),
                   pl.BlockSpec((1, B_HEADS, B_HEAD_DIM, B_HEAD_DIM), lambda b, i: (b, 0, 0, 0))],
        out_shape=[jax.ShapeDtypeStruct((n_seq * t, B_WIDTH), BF16),
                   jax.ShapeDtypeStruct((n_seq, B_HEADS, B_HEAD_DIM, B_HEAD_DIM), F32)],
        scratch_shapes=[pltpu.VMEM((8, width), F32),
                        pltpu.VMEM((B_HEAD_DIM, B_WIDTH), F32)],
        compiler_params=_params(2),
        name="rwkv_prompt",
    )(zb, *[p[name] for name in RWKV_PARAM_NAMES])


def _rwkv_sample_prep_kernel(zb_ref, shift_ref, *refs):
    n_p = len(RWKV_PARAM_NAMES)
    p = {name: ref[...] for name, ref in zip(RWKV_PARAM_NAMES, refs[:n_p])}
    vec_ref, g_ref = refs[n_p:]
    r, lw, k, v, kk, b, g = _rwkv_prep(zb_ref[...], shift_ref[...], p)
    for idx, val in enumerate((r, jnp.exp(lw), k, v, kk, b)):
        vec_ref[idx] = val
    g_ref[...] = g


def _rwkv_sample_prep(zb, shift, p):
    n, width = zb.shape
    return pl.pallas_call(
        _rwkv_sample_prep_kernel,
        grid=(1,),
        in_specs=[_const_spec((n, width)), _const_spec((n, width))] + _rwkv_param_specs(p),
        out_specs=[_const_spec((6, n, B_WIDTH)), _const_spec((n, B_WIDTH))],
        out_shape=[jax.ShapeDtypeStruct((6, n, B_WIDTH), F32), jax.ShapeDtypeStruct((n, B_WIDTH), F32)],
        compiler_params=_params(),
        name="rwkv_sample_prep",
    )(zb, shift, *[p[name] for name in RWKV_PARAM_NAMES])


def _wkv_step_kernel(s_ref, vec_ref, s_out_ref, o_ref, st_ref):
    d = B_HEAD_DIM
    st_ref[...] = s_ref[...].T
    r, w, k, v, kk, b = (vec_ref[i] for i in range(6))
    for i in range(d):
        rows = slice(i * d, (i + 1) * d)
        s_i = st_ref[rows, :]
        sa = -jnp.sum(s_i * kk, axis=0, keepdims=True)
        s_i = s_i * w + sa * b + v[i:i + 1, :] * k
        st_ref[rows, :] = s_i
        o_ref[i:i + 1, :] = jnp.sum(s_i * r, axis=0, keepdims=True)
    s_out_ref[...] = st_ref[...].T


def _wkv_step(state2d, vecs_t):
    pairs, dd = state2d.shape
    d = B_HEAD_DIM
    pb = min(128, pairs)
    return pl.pallas_call(
        _wkv_step_kernel,
        grid=(pairs // pb,),
        in_specs=[pl.BlockSpec((pb, dd), lambda i: (i, 0)), pl.BlockSpec((6, d, pb), lambda i: (0, 0, i))],
        out_specs=[pl.BlockSpec((pb, dd), lambda i: (i, 0)), pl.BlockSpec((d, pb), lambda i: (0, i))],
        out_shape=[jax.ShapeDtypeStruct((pairs, dd), F32), jax.ShapeDtypeStruct((d, pairs), F32)],
        scratch_shapes=[pltpu.VMEM((dd, pb), F32)],
        compiler_params=_params(),
        name="wkv_step",
    )(state2d, vecs_t)


def _rwkv_sample_finish_kernel(o_ref, vec_ref, g_ref, *refs):
    n_p = len(RWKV_PARAM_NAMES)
    p = {name: ref[...] for name, ref in zip(RWKV_PARAM_NAMES, refs[:n_p])}
    out_ref = refs[n_p]
    out_ref[...] = _rwkv_finish(o_ref[...], vec_ref[0], vec_ref[2], vec_ref[3], g_ref[...], p).astype(out_ref.dtype)


def _rwkv_sample_finish(o, vecs, g, p):
    n = o.shape[0]
    return pl.pallas_call(
        _rwkv_sample_finish_kernel,
        grid=(1,),
        in_specs=[_const_spec((n, B_WIDTH)), _const_spec((6, n, B_WIDTH)), _const_spec((n, B_WIDTH))]
        + _rwkv_param_specs(p),
        out_specs=_const_spec((n, B_WIDTH)),
        out_shape=jax.ShapeDtypeStruct((n, B_WIDTH), BF16),
        compiler_params=_params(),
        name="rwkv_sample_finish",
    )(o, vecs, g, *[p[name] for name in RWKV_PARAM_NAMES])


def _proj_residual_kernel(a_ref, b_ref, wa_ref, wb_ref, h_ref, g_ref, o_ref):
    acc = jnp.dot(a_ref[...], wa_ref[...], preferred_element_type=F32)
    acc = acc + jnp.dot(b_ref[...], wb_ref[...], preferred_element_type=F32)
    o_ref[...] = h_ref[...] + _rmsnorm(acc, g_ref[...])


def _proj_residual(a, b, wa, wb, h, g, tm):
    rows, d = h.shape
    tm = min(tm, rows)
    ka, kb = a.shape[1], b.shape[1]
    return pl.pallas_call(
        _proj_residual_kernel,
        grid=(rows // tm,),
        in_specs=[pl.BlockSpec((tm, ka), lambda i: (i, 0)), pl.BlockSpec((tm, kb), lambda i: (i, 0)),
                  _const_spec((ka, d)), _const_spec((kb, d)),
                  pl.BlockSpec((tm, d), lambda i: (i, 0)), _const_spec((1, d))],
        out_specs=pl.BlockSpec((tm, d), lambda i: (i, 0)),
        out_shape=jax.ShapeDtypeStruct((rows, d), F32),
        compiler_params=_params(),
        name="proj_residual",
    )(a, b, wa, wb, h, g)


def _gelu_erf(x):
    return 0.5 * x * (1.0 + lax.erf(x * 0.7071067811865476))


def _gelu_tanh(x):
    c = 0.7978845608028654
    return x * (0.5 + 0.5 * jnp.tanh(x * (c + (0.044715 * c) * (x * x))))


def _layernorm(x, g, b):
    mu = jnp.mean(x, axis=-1, keepdims=True)
    d = x - mu
    var = jnp.mean(d * d, axis=-1, keepdims=True)
    return d * lax.rsqrt(var + LN_EPS) * g + b


def _gmlp_kernel(h_ref, gpre_ref, win_ref, lng_ref, lnb_ref, ws_ref, bs_ref, wout_ref, gpost_ref,
                 o_ref, *v_refs, chunk):
    x = h_ref[...]
    tm = x.shape[0]
    width = wout_ref.shape[0]
    hd = width // C_HEADS
    xn = _rmsnorm(x, gpre_ref[...])
    z = _gelu_erf(jnp.dot(xn.astype(BF16), win_ref[...], preferred_element_type=F32))
    u = z[:, :width]
    v = _layernorm(z[:, width:], lng_ref[...], lnb_ref[...])
    if chunk == 1:
        mixed = v * ws_ref[...] + bs_ref[...]
        v_refs[0][...] = v
    else:
        ri = lax.broadcasted_iota(jnp.int32, (chunk, chunk), 0)
        ci = lax.broadcasted_iota(jnp.int32, (chunk, chunk), 1)
        vb = v.astype(BF16)
        bias = bs_ref[...]
        cols = []
        for hh in range(C_HEADS):
            wtri = jnp.where(ri >= ci, ws_ref[hh], 0.0).astype(BF16)
            parts = [jnp.dot(wtri, vb[c0:c0 + chunk, hh * hd:(hh + 1) * hd], preferred_element_type=F32)
                     + bias[:, hh:hh + 1] for c0 in range(0, tm, chunk)]
            cols.append(jnp.concatenate(parts, axis=0) if len(parts) > 1 else parts[0])
        mixed = jnp.concatenate(cols, axis=1)
    y = u * mixed
    out = jnp.dot(y.astype(BF16), wout_ref[...], preferred_element_type=F32)
    o_ref[...] = x + _rmsnorm(out, gpost_ref[...])


def _gmlp(h, gpre, win, lng, lnb, ws, bs, wout, gpost, chunk, tm):
    rows, d = h.shape
    tm = min(tm, rows)
    width = wout.shape[0]
    out_specs = [pl.BlockSpec((tm, d), lambda i: (i, 0))]
    out_shape = [jax.ShapeDtypeStruct((rows, d), F32)]
    if chunk == 1:
        out_specs.append(pl.BlockSpec((tm, width), lambda i: (i, 0)))
        out_shape.append(jax.ShapeDtypeStruct((rows, width), F32))
    return pl.pallas_call(
        functools.partial(_gmlp_kernel, chunk=chunk),
        grid=(rows // tm,),
        in_specs=[pl.BlockSpec((tm, d), lambda i: (i, 0)), _const_spec((1, d)), _const_spec(win.shape),
                  _const_spec((1, width)), _const_spec((1, width)), _const_spec(ws.shape),
                  _const_spec(bs.shape), _const_spec(wout.shape), _const_spec((1, d))],
        out_specs=out_specs,
        out_shape=out_shape,
        compiler_params=_params(),
        name="gmlp",
    )(h, gpre, win, lng, lnb, ws, bs, wout, gpost)


def _ffn_kernel(h_ref, gpre_ref, wg_ref, wu_ref, cw_ref, cb_ref, wd_ref, gpost_ref, *refs,
                carried, tiles_per_seq):
    x = h_ref[...]
    tm = x.shape[0]
    f = wg_ref.shape[1]
    xn = _rmsnorm(x, gpre_ref[...]).astype(BF16)
    if carried:
        o_ref, tail_ref, carry_ref = refs

        @pl.when(pl.program_id(0) % tiles_per_seq == 0)
        def _():
            carry_ref[...] = jnp.zeros_like(carry_ref)
    else:
        s2_ref, s1_ref, o_ref, gp_ref = refs
    cw = cw_ref[...]
    cb = cb_ref[...]

    def in_proj(c0, c1):
        return (jnp.dot(xn, wg_ref[:, c0:c1], preferred_element_type=F32),
                jnp.dot(xn, wu_ref[:, c0:c1], preferred_element_type=F32))

    bounds = [(c0, min(c0 + FFN_COL_BLOCK, f)) for c0 in range(0, f, FFN_COL_BLOCK)]
    nxt = in_proj(*bounds[0])
    out = None
    for idx, (c0, c1) in enumerate(bounds):
        gp, up = nxt
        if idx + 1 < len(bounds):
            nxt = in_proj(*bounds[idx + 1])
        if carried:
            rows = lax.broadcasted_iota(jnp.int32, gp.shape, 0)
            t2, t1 = carry_ref[0:1, c0:c1], carry_ref[1:2, c0:c1]
            x1 = jnp.where(rows == 0, t1, pltpu.roll(gp, 1, 0))
            x2 = jnp.where(rows == 0, t2, jnp.where(rows == 1, t1, pltpu.roll(gp, 2, 0)))
            carry_ref[0:2, c0:c1] = gp[tm - 2:tm, :]
            tail_ref[0, :, c0:c1] = gp[tm - 2:tm, :]
        else:
            x2, x1 = s2_ref[:, c0:c1], s1_ref[:, c0:c1]
            gp_ref[:, c0:c1] = gp
        conv = cb[:, c0:c1] + cw[2:3, c0:c1] * gp + cw[0:1, c0:c1] * x2 + cw[1:2, c0:c1] * x1
        hid = (_gelu_tanh(conv) * up).astype(BF16)
        part = jnp.dot(hid, wd_ref[c0:c1, :], preferred_element_type=F32)
        out = part if out is None else out + part
    o_ref[...] = x + _rmsnorm(out, gpost_ref[...])


def _ffn(h, gpre, wg, wu, cw, cb, wd, gpost, tm, seq_len=None, state=None):
    rows, d = h.shape
    f = wg.shape[1]
    tm = min(tm, rows)
    carried = state is None
    row_spec = pl.BlockSpec((tm, d), lambda i: (i, 0))
    in_specs = [row_spec, _const_spec((1, d)), _const_spec((d, f)), _const_spec((d, f)),
                _const_spec(cw.shape), _const_spec((1, f)), _const_spec((f, d)), _const_spec((1, d))]
    args = [h, gpre, wg, wu, cw, cb, wd, gpost]
    if carried:
        tm = min(tm, seq_len)
        tiles_per_seq = seq_len // tm
        row_spec = pl.BlockSpec((tm, d), lambda i: (i, 0))
        in_specs[0] = row_spec
        out_specs = [row_spec, pl.BlockSpec((1, 2, f), lambda i: (i // tiles_per_seq, 0, 0))]
        out_shape = [jax.ShapeDtypeStruct((rows, d), F32),
                     jax.ShapeDtypeStruct((rows // seq_len, 2, f), F32)]
        scratch = [pltpu.VMEM((8, f), F32)]
    else:
        tiles_per_seq = 1
        in_specs += [pl.BlockSpec((tm, f), lambda i: (i, 0))] * 2
        args += [state[:, 0], state[:, 1]]
        out_specs = [row_spec, pl.BlockSpec((tm, f), lambda i: (i, 0))]
        out_shape = [jax.ShapeDtypeStruct((rows, d), F32), jax.ShapeDtypeStruct((rows, f), F32)]
        scratch = []
    return pl.pallas_call(
        functools.partial(_ffn_kernel, carried=carried, tiles_per_seq=tiles_per_seq),
        grid=(rows // tm,),
        in_specs=in_specs,
        out_specs=out_specs,
        out_shape=out_shape,
        scratch_shapes=scratch,
        compiler_params=_params(),
        name="conv_ffn",
    )(*args)


def kernel(x_prompt, x_sample, cache_win_k, cache_win_v, state_wkv, state_shift, state_ffn_conv,
           norm_mix_pre, norm_mix_post, norm_ffn_pre, norm_ffn_post,
           w_in_even, attn_sinks, shift_mu, decay_w0, decay_w2, iclr_a0, iclr_a2, gate_g2,
           key_k, key_a, bonus_r_k, gn_gain, gn_bias, w_out_even,
           w_in_odd, sgu_ln_gain, sgu_ln_bias, sgu_w, sgu_b, w_out_odd,
           ffn_w_gate, ffn_w_up, ffn_conv_w, ffn_conv_b, ffn_w_down):
    n_seq, t, d = x_prompt.shape
    n_dec, dec_t, _ = x_sample.shape
    assert dec_t == 1, "the sample path is written for one new token per sequence"
    depth = norm_mix_pre.shape[0]
    wp = min(WINDOW, t)
    row = lambda a: a.reshape(1, -1)

    hp = x_prompt.reshape(n_seq * t, d)
    hs = x_sample.reshape(n_dec, d)
    cos_p, sin_p = _rope_tables(jnp.arange(t, dtype=jnp.int32))
    cos_s, sin_s = _rope_tables(PAST_LEN + jnp.arange(1, dtype=jnp.int32))
    seg = jnp.kron(jnp.eye(HEADS_PER_GROUP, dtype=F32), jnp.ones((B_HEAD_DIM, B_HEAD_DIM), F32)).astype(BF16)

    kp_l, vp_l, ks_l, vs_l, sp_l, ss_l, shp_l, shs_l, vsgu_l, cp_l, cs_l = ([] for _ in range(11))
    for layer in range(depth):
        j = layer // 2
        g_pre, g_post = row(norm_mix_pre[layer]), row(norm_mix_post[layer])
        if layer % 2 == 0:
            w_in = w_in_even[j].astype(BF16)
            w_out = w_out_even[j].astype(BF16)
            sinks = row(attn_sinks[j])
            p = {"mu": row(shift_mu[j]), "w0": row(decay_w0[j]), "w2": decay_w2[j].astype(BF16),
                 "a0": row(iclr_a0[j]), "a2": iclr_a2[j].astype(BF16), "g2": gate_g2[j].astype(BF16),
                 "k_k": row(key_k[j]), "k_a": row(key_a[j]), "r_k": row(bonus_r_k[j]),
                 "gn_g": row(gn_gain[j]), "gn_b": row(gn_bias[j]), "seg": seg}
            za, zb = _norm_proj(hp, g_pre, w_in, (A_PROJ, w_in.shape[1] - A_PROJ), tm=512)
            oa, kp, vp = _attn_prompt(za, cos_p, sin_p, sinks, n_seq, t)
            ob, sp = _rwkv_prompt(zb, p, n_seq, t)
            hp = _proj_residual(oa, ob, w_out[:A_Q], w_out[A_Q:], hp, g_post, tm=512)
            kp_l.append(kp.reshape(n_seq, wp, A_KV_HEADS, HEAD_DIM))
            vp_l.append(vp.reshape(n_seq, wp, A_KV_HEADS, HEAD_DIM))
            sp_l.append(sp)
            shp_l.append(zb.reshape(n_seq, t, -1)[:, -1])
            za, zb = _norm_proj(hs, g_pre, w_in, (A_PROJ, w_in.shape[1] - A_PROJ), tm=512)
            wb = cache_win_k.shape[2]
            oa, kn, vn = _attn_sample(za, cache_win_k[j].reshape(n_dec, wb, A_KV),
                                      cache_win_v[j].reshape(n_dec, wb, A_KV), cos_s, sin_s, sinks)
            vecs, gate = _rwkv_sample_prep(zb, state_shift[j], p)
            pairs = n_dec * B_HEADS
            vecs_t = vecs.reshape(6, pairs, B_HEAD_DIM).transpose(0, 2, 1)
            s_new, o_t = _wkv_step(state_wkv[j].reshape(pairs, B_HEAD_DIM * B_HEAD_DIM), vecs_t)
            ob = _rwkv_sample_finish(o_t.T.reshape(n_dec, B_WIDTH), vecs, gate, p)
            hs = _proj_residual(oa, ob, w_out[:A_Q], w_out[A_Q:], hs, g_post, tm=512)
            ks_l.append(kn.reshape(n_dec, 1, A_KV_HEADS, HEAD_DIM))
            vs_l.append(vn.reshape(n_dec, 1, A_KV_HEADS, HEAD_DIM))
            ss_l.append(s_new.reshape(n_dec, B_HEADS, B_HEAD_DIM, B_HEAD_DIM))
            shs_l.append(zb)
        else:
            w_in = w_in_odd[j].astype(BF16)
            w_out = w_out_odd[j].astype(BF16)
            lng, lnb = row(sgu_ln_gain[j]), row(sgu_ln_bias[j])
            chunk = min(t, CHUNK)
            hd = w_out.shape[0] // C_HEADS
            hp, = _gmlp(hp, g_pre, w_in, lng, lnb, sgu_w[j][:, :chunk, :chunk], sgu_b[j][:, :chunk].T,
                        w_out, g_post, chunk=chunk, tm=2 * chunk)
            hs, vsg = _gmlp(hs, g_pre, w_in, lng, lnb, row(jnp.repeat(sgu_w[j][:, 0, 0], hd)),
                            row(jnp.repeat(sgu_b[j][:, 0], hd)), w_out, g_post, chunk=1, tm=512)
            vsgu_l.append(vsg.reshape(n_dec, 1, -1))
        f_pre, f_post = row(norm_ffn_pre[layer]), row(norm_ffn_post[layer])
        wg, wu, wd = (w[layer].astype(BF16) for w in (ffn_w_gate, ffn_w_up, ffn_w_down))
        cw, cb = ffn_conv_w[layer], row(ffn_conv_b[layer])
        hp, cp = _ffn(hp, f_pre, wg, wu, cw, cb, wd, f_post, tm=256, seq_len=t)
        hs, gp = _ffn(hs, f_pre, wg, wu, cw, cb, wd, f_post, tm=256, state=state_ffn_conv[layer])
        cp_l.append(cp)
        cs_l.append(jnp.stack([state_ffn_conv[layer][:, 1], gp], axis=1))

    return (hp.reshape(n_seq, t, d), hs.reshape(n_dec, 1, d),
            jnp.stack(kp_l), jnp.stack(vp_l), jnp.stack(ks_l), jnp.stack(vs_l),
            jnp.stack(sp_l), jnp.stack(ss_l), jnp.stack(shp_l), jnp.stack(shs_l),
            jnp.stack(vsgu_l), jnp.stack(cp_l), jnp.stack(cs_l))
```

```python
import functools

import jax
import jax.numpy as jnp
from jax import lax
from jax.experimental import pallas as pl
from jax.experimental.pallas import tpu as pltpu

F32 = jnp.float32
BF16 = jnp.bfloat16

PAST_LEN = 8192
HEAD_DIM = 64
A_HEADS = 8
A_KV_HEADS = 2
A_GROUP = A_HEADS // A_KV_HEADS
WINDOW = 128
ROPE_THETA = 10000.0
B_HEAD_DIM = 64
B_HEADS = 8
B_WIDTH = B_HEADS * B_HEAD_DIM
DECAY_RANK = 64
ICLR_RANK = 64
DECAY_SCALE = 0.606531
GN_EPS = 64e-5
CHUNK = 128
C_HEADS = 8
RMS_EPS = 1e-6
LN_EPS = 1e-5
NEG = -1e30
A_Q = A_HEADS * HEAD_DIM
A_KV = A_KV_HEADS * HEAD_DIM
A_PROJ = A_Q + 2 * A_KV

WKV_CHUNK = 64
WKV_CHUNKS_PER_STEP = 4
ATTN_BLOCKS_PER_STEP = 4
FFN_COL_BLOCK = 256
ROW_TILE = 512
FFN_ROW_TILE = 256
GMLP_CHUNKS_PER_STEP = 2
MXU_LANES = 256
HEADS_PER_GROUP = MXU_LANES // B_HEAD_DIM
V7X_VMEM_LIMIT_BYTES = 56 * 1024 * 1024

NT_DIMS = (((1,), (1,)), ((), ()))
TN_DIMS = (((0,), (0,)), ((), ()))


def _params(n_axes=1):
    return pltpu.CompilerParams(dimension_semantics=("arbitrary",) * n_axes,
                                vmem_limit_bytes=V7X_VMEM_LIMIT_BYTES)


def _rmsnorm(x, g):
    return x * lax.rsqrt(jnp.mean(x * x, axis=-1, keepdims=True) + RMS_EPS) * g


def _bdot(a, b):
    return jnp.dot(a.astype(BF16), b.astype(BF16), preferred_element_type=F32)


def _const_spec(shape):
    return pl.BlockSpec(shape, lambda *_: (0,) * len(shape))


def _norm_proj_kernel(x_ref, g_ref, w_ref, *out_refs, splits):
    xn = _rmsnorm(x_ref[...], g_ref[...])
    z = jnp.dot(xn.astype(BF16), w_ref[...], preferred_element_type=F32)
    off = 0
    for o_ref, n in zip(out_refs, splits):
        o_ref[...] = z[:, off:off + n]
        off += n


def _norm_proj(x, g, w, splits, tm):
    rows, d = x.shape
    n = w.shape[1]
    tm = min(tm, rows)
    return pl.pallas_call(
        functools.partial(_norm_proj_kernel, splits=splits),
        grid=(rows // tm,),
        in_specs=[pl.BlockSpec((tm, d), lambda i: (i, 0)), _const_spec((1, d)), _const_spec((d, n))],
        out_specs=[pl.BlockSpec((tm, s), lambda i: (i, 0)) for s in splits],
        out_shape=[jax.ShapeDtypeStruct((rows, s), F32) for s in splits],
        compiler_params=_params(),
        name="norm_proj",
    )(x, g, w)


def _rope_tables(pos):
    half = HEAD_DIM // 2
    inv = ROPE_THETA ** (-jnp.arange(half, dtype=F32) / half)
    ang = pos.astype(F32)[:, None] * inv[None, :]
    cos, sin = jnp.cos(ang), jnp.sin(ang)
    cos2 = jnp.concatenate([cos, cos], axis=-1)
    sin2 = jnp.concatenate([-sin, sin], axis=-1)
    return jnp.tile(cos2, (1, 2)), jnp.tile(sin2, (1, 2))


def _rope(x, cos, sin):
    width = x.shape[1]
    half = HEAD_DIM // 2
    lane = lax.broadcasted_iota(jnp.int32, x.shape, 1)
    upper = pltpu.roll(x, width - half, 1)
    lower = pltpu.roll(x, half, 1)
    rot = jnp.where((lane % HEAD_DIM) < half, upper, lower)
    return x * cos + rot * sin


def _tile_lanes(t, reps):
    return jnp.concatenate([t] * reps, axis=1)


def _attn_prompt_kernel(cur_ref, prev_ref, cos_ref, sin_ref, cosp_ref, sinp_ref, sink_ref,
                        o_ref, k_out_ref, v_out_ref):
    w = prev_ref.shape[0]
    n_blk = cur_ref.shape[0] // w
    cur = cur_ref[...]
    prev = prev_ref[...]
    cos, sin = cos_ref[...], sin_ref[...]
    q = (_rope(cur[:, :A_Q], _tile_lanes(cos, A_HEADS // 2), _tile_lanes(sin, A_HEADS // 2))
         * (HEAD_DIM ** -0.5)).astype(BF16)
    k = _rope(cur[:, A_Q:A_Q + A_KV], cos, sin)
    v = cur[:, A_Q + A_KV:A_PROJ]
    kp = _rope(prev[:, A_Q:A_Q + A_KV], cosp_ref[...], sinp_ref[...])
    vp = prev[:, A_Q + A_KV:A_PROJ]
    k_out_ref[0] = k[(n_blk - 1) * w:, :]
    v_out_ref[0] = v[(n_blk - 1) * w:, :]

    qi = lax.broadcasted_iota(jnp.int32, (w, 2 * w), 0)
    kj = lax.broadcasted_iota(jnp.int32, (w, 2 * w), 1)
    diff = w + qi - kj
    band = (diff >= 0) & (diff <= WINDOW)
    first_band = band & ((pl.program_id(1) > 0) | (kj >= w))
    kk_all = jnp.concatenate([kp, k], axis=0).astype(BF16)
    vv_all = jnp.concatenate([vp, v], axis=0).astype(BF16)

    items = [(j, h) for j in range(n_blk) for h in range(A_HEADS)]
    kv_lanes = lambda h: slice((h // A_GROUP) * HEAD_DIM, (h // A_GROUP + 1) * HEAD_DIM)
    s = [lax.dot_general(q[j * w:(j + 1) * w, h * HEAD_DIM:(h + 1) * HEAD_DIM],
                         kk_all[j * w:(j + 2) * w, kv_lanes(h)], NT_DIMS, preferred_element_type=F32)
         for j, h in items]
    s = [jnp.where(first_band if j == 0 else band, x, NEG) for (j, h), x in zip(items, s)]
    sink = [sink_ref[:, h:h + 1] for j, h in items]
    m = [jnp.maximum(jnp.max(x, axis=-1, keepdims=True), sk) for x, sk in zip(s, sink)]
    p = [jnp.exp(x - mx) for x, mx in zip(s, m)]
    den = [jnp.sum(x, axis=-1, keepdims=True) + jnp.exp(sk - mx) for x, sk, mx in zip(p, sink, m)]
    o = [jnp.dot(x.astype(BF16), vv_all[j * w:(j + 2) * w, kv_lanes(h)], preferred_element_type=F32)
         for (j, h), x in zip(items, p)]
    o = [x / dn for x, dn in zip(o, den)]
    o_ref[...] = jnp.concatenate(
        [jnp.concatenate(o[j * A_HEADS:(j + 1) * A_HEADS], axis=1) for j in range(n_blk)], axis=0).astype(o_ref.dtype)


def _attn_prompt(za, cos, sin, sinks, n_seq, t):
    w = min(WINDOW, t)
    n_blk = ATTN_BLOCKS_PER_STEP if t % (w * ATTN_BLOCKS_PER_STEP) == 0 else 1
    tm = n_blk * w
    nt = t // tm
    cur_map = lambda b, i: (b * nt + i, 0)
    prev_map = lambda b, i: ((b * nt + i) * n_blk - jnp.minimum(i, 1), 0)
    tab_cur = lambda b, i: (i, 0)
    tab_prev = lambda b, i: (i * n_blk - jnp.minimum(i, 1), 0)
    return pl.pallas_call(
        _attn_prompt_kernel,
        grid=(n_seq, nt),
        in_specs=[pl.BlockSpec((tm, A_PROJ), cur_map), pl.BlockSpec((w, A_PROJ), prev_map),
                  pl.BlockSpec((tm, A_KV), tab_cur), pl.BlockSpec((tm, A_KV), tab_cur),
                  pl.BlockSpec((w, A_KV), tab_prev), pl.BlockSpec((w, A_KV), tab_prev),
                  _const_spec((1, A_HEADS))],
        out_specs=[pl.BlockSpec((tm, A_Q), cur_map),
                   pl.BlockSpec((1, w, A_KV), lambda b, i: (b, 0, 0)),
                   pl.BlockSpec((1, w, A_KV), lambda b, i: (b, 0, 0))],
        out_shape=[jax.ShapeDtypeStruct((n_seq * t, A_Q), BF16),
                   jax.ShapeDtypeStruct((n_seq, w, A_KV), F32),
                   jax.ShapeDtypeStruct((n_seq, w, A_KV), F32)],
        compiler_params=_params(2),
        name="attn_prompt",
    )(za, za, cos, sin, cos, sin, sinks)


def _attn_sample_kernel(za_ref, ck_ref, cv_ref, cos_ref, sin_ref, sink_ref, o_ref, kn_ref, vn_ref):
    sb, wb = ck_ref.shape[0], ck_ref.shape[1]
    za = za_ref[...]
    cos, sin = cos_ref[...], sin_ref[...]
    q = _rope(za[:, :A_Q], _tile_lanes(cos, A_HEADS // 2), _tile_lanes(sin, A_HEADS // 2))
    k = _rope(za[:, A_Q:A_Q + A_KV], cos, sin)
    v = za[:, A_Q + A_KV:A_PROJ]
    kn_ref[...] = k
    vn_ref[...] = v
    ck = ck_ref[...].reshape(sb * wb, A_KV).astype(BF16)
    cv = cv_ref[...].reshape(sb * wb, A_KV).astype(BF16)
    rows = A_GROUP * sb
    row_seq = lax.broadcasted_iota(jnp.int32, (rows, sb * wb), 0) % sb
    col = lax.broadcasted_iota(jnp.int32, (rows, sb * wb), 1)
    own = (col >= row_seq * wb) & (col < (row_seq + 1) * wb)
    kvs = range(A_KV_HEADS)
    sl = [slice(kv * HEAD_DIM, (kv + 1) * HEAD_DIM) for kv in kvs]
    heads = [[kv * A_GROUP + g for g in range(A_GROUP)] for kv in kvs]
    qs = [jnp.concatenate([q[:, h * HEAD_DIM:(h + 1) * HEAD_DIM] for h in heads[kv]], axis=0) * (HEAD_DIM ** -0.5)
          for kv in kvs]
    sink = [jnp.concatenate([jnp.broadcast_to(sink_ref[:, h:h + 1], (sb, 1)) for h in heads[kv]], axis=0)
            for kv in kvs]
    s = [lax.dot_general(qs[kv].astype(BF16), ck[:, sl[kv]], NT_DIMS, preferred_element_type=F32) for kv in kvs]
    s = [jnp.where(own, x, NEG) for x in s]
    k_rep = [jnp.concatenate([k[:, sl[kv]]] * A_GROUP, axis=0) for kv in kvs]
    v_rep = [jnp.concatenate([v[:, sl[kv]]] * A_GROUP, axis=0) for kv in kvs]
    s_new = [jnp.sum(qs[kv] * k_rep[kv], axis=-1, keepdims=True) for kv in kvs]
    m = [jnp.maximum(jnp.maximum(jnp.max(s[kv], axis=-1, keepdims=True), s_new[kv]), sink[kv]) for kv in kvs]
    p = [jnp.exp(s[kv] - m[kv]) for kv in kvs]
    p_new = [jnp.exp(s_new[kv] - m[kv]) for kv in kvs]
    den = [jnp.sum(p[kv], axis=-1, keepdims=True) + p_new[kv] + jnp.exp(sink[kv] - m[kv]) for kv in kvs]
    o = [jnp.dot(p[kv].astype(BF16), cv[:, sl[kv]], preferred_element_type=F32) for kv in kvs]
    o = [(o[kv] + p_new[kv] * v_rep[kv]) / den[kv] for kv in kvs]
    o_ref[...] = jnp.concatenate([o[kv][g * sb:(g + 1) * sb] for kv in kvs for g in range(A_GROUP)],
                                 axis=1).astype(o_ref.dtype)


def _attn_sample(za, cache_k, cache_v, cos, sin, sinks):
    n, wb = cache_k.shape[0], cache_k.shape[1]
    sb = min(16, n)
    return pl.pallas_call(
        _attn_sample_kernel,
        grid=(n // sb,),
        in_specs=[pl.BlockSpec((sb, A_PROJ), lambda i: (i, 0)),
                  pl.BlockSpec((sb, wb, A_KV), lambda i: (i, 0, 0)),
                  pl.BlockSpec((sb, wb, A_KV), lambda i: (i, 0, 0)),
                  _const_spec((1, A_KV)), _const_spec((1, A_KV)), _const_spec((1, A_HEADS))],
        out_specs=[pl.BlockSpec((sb, A_Q), lambda i: (i, 0)),
                   pl.BlockSpec((sb, A_KV), lambda i: (i, 0)),
                   pl.BlockSpec((sb, A_KV), lambda i: (i, 0))],
        out_shape=[jax.ShapeDtypeStruct((n, A_Q), BF16),
                   jax.ShapeDtypeStruct((n, A_KV), F32),
                   jax.ShapeDtypeStruct((n, A_KV), F32)],
        compiler_params=_params(),
        name="attn_sample",
    )(za, cache_k, cache_v, cos, sin, sinks)


RWKV_PARAM_NAMES = ("mu", "w0", "w2", "a0", "a2", "g2", "k_k", "k_a", "r_k", "gn_g", "gn_b", "seg")


def _rwkv_param_specs(p):
    return [_const_spec(p[name].shape) for name in RWKV_PARAM_NAMES]


def _rwkv_prep(zb, prev, p):
    o1, o2, o3 = B_WIDTH, 2 * B_WIDTH, 3 * B_WIDTH
    o4, o5 = o3 + DECAY_RANK, o3 + DECAY_RANK + ICLR_RANK
    zs = zb + (prev - zb) * p["mu"]
    r, k, v = zs[:, :o1], zs[:, o1:o2], zs[:, o2:o3]
    wd, ad, gd = zs[:, o3:o4], zs[:, o4:o5], zs[:, o5:]
    lw = -DECAY_SCALE * jax.nn.sigmoid(p["w0"] + _bdot(jnp.tanh(wd), p["w2"]))
    a = jax.nn.sigmoid(p["a0"] + _bdot(ad, p["a2"]))
    g = _bdot(jax.nn.sigmoid(gd), p["g2"])
    kk = k * p["k_k"]
    kk = kk / jnp.maximum(jnp.sqrt(_segsum(kk * kk, p["seg"])), 1e-12)
    k = k * (1.0 + (a - 1.0) * p["k_a"])
    return r, lw, k, v, kk, kk * a, g


def _split2(x):
    hi = x.astype(BF16)
    return hi, (x - hi.astype(F32)).astype(BF16)


def _segsum(x, seg):
    outs = []
    for g0 in range(0, x.shape[1], MXU_LANES):
        hi, lo = _split2(x[:, g0:g0 + MXU_LANES])
        outs.append(jnp.dot(hi, seg, preferred_element_type=F32) + jnp.dot(lo, seg, preferred_element_type=F32))
    return jnp.concatenate(outs, axis=1)


def _rwkv_finish(o, r, k, v, g, p):
    seg = p["seg"]
    mean = _segsum(o, seg) * (1.0 / B_HEAD_DIM)
    d = o - mean
    var = _segsum(d * d, seg) * (1.0 / B_HEAD_DIM)
    o = d * lax.rsqrt(var + GN_EPS) * p["gn_g"] + p["gn_b"]
    bonus = _segsum(r * k * p["r_k"], seg) * v
    return (o + bonus) * g


def _split3(x):
    hi = x.astype(BF16)
    r1 = x - hi.astype(F32)
    mid = r1.astype(BF16)
    return hi, mid, (r1 - mid.astype(F32)).astype(BF16)


def _mm(a, b):
    return jnp.dot(a.astype(BF16), b.astype(BF16), preferred_element_type=F32)


def _head_block_diag(x, bd_mask):
    tiled = jnp.concatenate([x.astype(BF16)] * HEADS_PER_GROUP, axis=0)
    return jnp.where(bd_mask, tiled, jnp.zeros_like(tiled))


def _rwkv_chunk_kernel(zb_ref, *refs, n_sub):
    n_p = len(RWKV_PARAM_NAMES)
    p = {name: ref[...] for name, ref in zip(RWKV_PARAM_NAMES, refs[:n_p])}
    o_ref, s_out_ref, carry_ref, state_ref = refs[n_p:]
    n_rows = zb_ref.shape[0]
    c = n_rows // n_sub
    d = B_HEAD_DIM

    @pl.when(pl.program_id(1) == 0)
    def _():
        carry_ref[...] = jnp.zeros_like(carry_ref)
        state_ref[...] = jnp.zeros_like(state_ref)

    zb = zb_ref[...]
    rows = lax.broadcasted_iota(jnp.int32, zb.shape, 0)
    prev = jnp.where(rows == 0, carry_ref[0:1, :], pltpu.roll(zb, 1, 0))
    carry_ref[0:1, :] = zb[n_rows - 1:n_rows, :]
    r, lw, k, v, kk, b, g = _rwkv_prep(zb, prev, p)

    ti = lax.broadcasted_iota(jnp.int32, (c, MXU_LANES), 0)
    si = lax.broadcasted_iota(jnp.int32, (c, MXU_LANES), 1) % d
    lane_head = lax.broadcasted_iota(jnp.int32, (c, MXU_LANES), 1) // d
    incl, strict = ti >= si, ti > si
    eye = (ti == si).astype(F32)
    bd_mask = (lax.broadcasted_iota(jnp.int32, (MXU_LANES, MXU_LANES), 0) // c
               == lax.broadcasted_iota(jnp.int32, (MXU_LANES, MXU_LANES), 1) // d)
    tri = (lax.broadcasted_iota(jnp.int32, (c, c), 0) >= lax.broadcasted_iota(jnp.int32, (c, c), 1)).astype(BF16)
    bd = functools.partial(_head_block_diag, bd_mask=bd_mask)

    items = []
    for sub in range(n_sub):
        rs = slice(sub * c, (sub + 1) * c)
        lw_c = lw[rs]
        gcum = sum(jnp.dot(tri, part, preferred_element_type=F32) for part in _split3(lw_c))
        g_last = gcum[c - 1:c, :]
        e_neg = jnp.exp(-gcum)
        e_hat = jnp.exp(g_last - gcum)
        kap = (kk[rs] * jnp.exp(gcum - lw_c)).astype(BF16)
        rt = r[rs] * jnp.exp(gcum)
        kt, bt = (k[rs] * e_neg).astype(BF16), (b[rs] * e_neg).astype(BF16)
        khat, bhat = (k[rs] * e_hat).astype(BF16), (b[rs] * e_hat).astype(BF16)
        dec_last = jnp.exp(g_last)
        v_c = v[rs].astype(BF16)
        for g0 in range(0, B_WIDTH, MXU_LANES):
            gl = slice(g0, g0 + MXU_LANES)
            items.append(dict(g0=g0, kap=kap[:, gl], rt=rt[:, gl], kt=kt[:, gl], bt=bt[:, gl], khat=khat[:, gl],
                              bhat=bhat[:, gl], v=v_c[:, gl], dec=dec_last[:, gl]))
    for it in items:
        lhs = jnp.concatenate([it["kap"], it["rt"].astype(BF16)], axis=0)
        it["m_k"] = lax.dot_general(lhs, bd(it["kt"]), NT_DIMS, preferred_element_type=F32)
        it["m_b"] = lax.dot_general(lhs, bd(it["bt"]), NT_DIMS, preferred_element_type=F32)
    for it in items:
        it["a_k"] = jnp.where(strict, it["m_k"][:c], 0.0)
        it["p_k"] = jnp.where(incl, it["m_k"][c:], 0.0)
        it["p_b"] = jnp.where(incl, it["m_b"][c:], 0.0).astype(BF16)
        it["pw"] = -jnp.where(strict, it["m_b"][:c], 0.0)
        it["pw_bd"] = bd(it["pw"])
        it["t_inv"] = eye + it["pw"]
    span = 2
    while span < c:
        for it in items:
            it["pw"] = _mm(it["pw"], it["pw_bd"])
        for it in items:
            it["pw_bd"] = bd(it["pw"])
            it["t_inv"] = it["t_inv"] + _mm(it["t_inv"], it["pw_bd"])
        span *= 2
    for it in items:
        it["v_bd"] = bd(it["v"])
        it["akv"] = _mm(it["a_k"], it["v_bd"])
    for it in items:
        it["k_p"] = _mm(it["t_inv"], bd(it["kap"]))
        it["u0"] = _mm(it["t_inv"], bd(it["akv"]))
    for it in items:
        it["w_bd"] = jnp.where(bd_mask, lax.dot_general(it["k_p"].astype(BF16), it["bhat"], TN_DIMS,
                                                        preferred_element_type=F32), 0.0).astype(BF16)
        s1_full = lax.dot_general(jnp.concatenate([it["v"], it["u0"].astype(BF16)], axis=0),
                                  jnp.concatenate([it["khat"], -it["bhat"]], axis=0),
                                  TN_DIMS, preferred_element_type=F32)
        it["s1"] = sum(jnp.where(lane_head == h, s1_full[h * d:(h + 1) * d], 0.0) for h in range(HEADS_PER_GROUP))
        it["r_p"] = (it["rt"] - _mm(it["p_b"], bd(it["k_p"]))).astype(BF16)
        it["o0"] = _mm(it["p_k"], it["v_bd"]) - _mm(it["p_b"], bd(it["u0"]))
    state = {g0: state_ref[:, g0:g0 + MXU_LANES] for g0 in range(0, B_WIDTH, MXU_LANES)}
    o_parts = []
    for it in items:
        s0 = state[it["g0"]]
        o_parts.append(lax.dot_general(it["r_p"], bd(s0), NT_DIMS, preferred_element_type=F32) + it["o0"])
        state[it["g0"]] = s0 * it["dec"] - _mm(s0, it["w_bd"]) + it["s1"]
    for g0, s_new in state.items():
        state_ref[:, g0:g0 + MXU_LANES] = s_new
    n_groups = B_WIDTH // MXU_LANES
    o = jnp.concatenate([jnp.concatenate(o_parts[i:i + n_groups], axis=1)
                         for i in range(0, len(o_parts), n_groups)], axis=0)
    o_ref[...] = _rwkv_finish(o, r, k, v, g, p).astype(o_ref.dtype)
    for h in range(B_HEADS):
        s_out_ref[0, h] = state_ref[:, h * d:(h + 1) * d]


def _rwkv_prompt(zb, p, n_seq, t):
    c = WKV_CHUNK
    assert c == B_HEAD_DIM and t % c == 0, "head packing puts WKV_CHUNK time steps where a head's lanes go"
    n_sub = WKV_CHUNKS_PER_STEP if t % (c * WKV_CHUNKS_PER_STEP) == 0 else 1
    tm = c * n_sub
    nc = t // tm
    width = zb.shape[1]
    return pl.pallas_call(
        functools.partial(_rwkv_chunk_kernel, n_sub=n_sub),
        grid=(n_seq, nc),
        in_specs=[pl.BlockSpec((tm, width), lambda b, i: (b * nc + i, 0))] + _rwkv_param_specs(p),
        out_specs=[pl.BlockSpec((tm, B_WIDTH), lambda b, i: (b * nc + i, 0)),
                   pl.BlockSpec((1, B_HEADS, B_HEAD_DIM, B_HEAD_DIM), lambda b, i: (b, 0, 0, 0))],
        out_shape=[jax.ShapeDtypeStruct((n_seq * t, B_WIDTH), BF16),
                   jax.ShapeDtypeStruct((n_seq, B_HEADS, B_HEAD_DIM, B_HEAD_DIM), F32)],
        scratch_shapes=[pltpu.VMEM((8, width), F32),
                        pltpu.VMEM((B_HEAD_DIM, B_WIDTH), F32)],
        compiler_params=_params(2),
        name="rwkv_prompt",
    )(zb, *[p[name] for name in RWKV_PARAM_NAMES])


def _segsum3(x, seg):
    outs = []
    for g0 in range(0, x.shape[1], MXU_LANES):
        outs.append(sum(jnp.dot(part, seg, preferred_element_type=F32) for part in _split3(x[:, g0:g0 + MXU_LANES])))
    return jnp.concatenate(outs, axis=1)


def _rwkv_sample_kernel(zb_ref, shift_ref, s_ref, *refs):
    n_p = len(RWKV_PARAM_NAMES)
    p = {name: ref[...] for name, ref in zip(RWKV_PARAM_NAMES, refs[:n_p])}
    o_ref, s_out_ref = refs[n_p:]
    nb = zb_ref.shape[0]
    d = B_HEAD_DIM
    seg = p["seg"]
    r, lw, k, v, kk, b, g = _rwkv_prep(zb_ref[...], shift_ref[...], p)

    def per_row(x):
        return jnp.concatenate([jnp.broadcast_to(x[n:n + 1, :], (d, x.shape[1])) for n in range(nb)], axis=0)

    s0 = jnp.concatenate([s_ref[:, h].reshape(nb * d, d) for h in range(B_HEADS)], axis=1)
    eye = (lax.broadcasted_iota(jnp.int32, s0.shape, 0) % d
           == lax.broadcasted_iota(jnp.int32, s0.shape, 1) % d).astype(F32)
    sa = _segsum(s0 * per_row(kk), seg)
    v_rows = _segsum3(eye * per_row(v), seg)
    s_new = s0 * per_row(jnp.exp(lw)) - sa * per_row(b) + v_rows * per_row(k)
    o_rows = _segsum(s_new * per_row(r), seg) * eye
    o = jnp.sum(o_rows.reshape(nb, d, B_WIDTH), axis=1)
    for h in range(B_HEADS):
        s_out_ref[:, h] = s_new[:, h * d:(h + 1) * d].reshape(nb, d, d)
    o_ref[...] = _rwkv_finish(o, r, k, v, g, p).astype(o_ref.dtype)


def _rwkv_sample(zb, shift, state, p):
    n, width = zb.shape
    nb = min(8, n)
    d = B_HEAD_DIM
    state_spec = pl.BlockSpec((nb, B_HEADS, d, d), lambda i: (i, 0, 0, 0))
    return pl.pallas_call(
        _rwkv_sample_kernel,
        grid=(n // nb,),
        in_specs=[pl.BlockSpec((nb, width), lambda i: (i, 0)), pl.BlockSpec((nb, width), lambda i: (i, 0)),
                  state_spec] + _rwkv_param_specs(p),
        out_specs=[pl.BlockSpec((nb, B_WIDTH), lambda i: (i, 0)), state_spec],
        out_shape=[jax.ShapeDtypeStruct((n, B_WIDTH), BF16), jax.ShapeDtypeStruct(state.shape, F32)],
        compiler_params=_params(),
        name="rwkv_sample",
    )(zb, shift, state, *[p[name] for name in RWKV_PARAM_NAMES])


def _proj_residual_kernel(a_ref, b_ref, wa_ref, wb_ref, h_ref, g_ref, o_ref):
    acc = jnp.dot(a_ref[...], wa_ref[...], preferred_element_type=F32)
    acc = acc + jnp.dot(b_ref[...], wb_ref[...], preferred_element_type=F32)
    o_ref[...] = h_ref[...] + _rmsnorm(acc, g_ref[...])


def _proj_residual(a, b, wa, wb, h, g, tm):
    rows, d = h.shape
    tm = min(tm, rows)
    ka, kb = a.shape[1], b.shape[1]
    return pl.pallas_call(
        _proj_residual_kernel,
        grid=(rows // tm,),
        in_specs=[pl.BlockSpec((tm, ka), lambda i: (i, 0)), pl.BlockSpec((tm, kb), lambda i: (i, 0)),
                  _const_spec((ka, d)), _const_spec((kb, d)),
                  pl.BlockSpec((tm, d), lambda i: (i, 0)), _const_spec((1, d))],
        out_specs=pl.BlockSpec((tm, d), lambda i: (i, 0)),
        out_shape=jax.ShapeDtypeStruct((rows, d), F32),
        compiler_params=_params(),
        name="proj_residual",
    )(a, b, wa, wb, h, g)


def _gelu_erf(x):
    return 0.5 * x * (1.0 + lax.erf(x * 0.7071067811865476))


def _gelu_tanh(x):
    c = 0.7978845608028654
    return x * (0.5 + 0.5 * jnp.tanh(x * (c + (0.044715 * c) * (x * x))))


def _layernorm(x, g, b):
    mu = jnp.mean(x, axis=-1, keepdims=True)
    d = x - mu
    var = jnp.mean(d * d, axis=-1, keepdims=True)
    return d * lax.rsqrt(var + LN_EPS) * g + b


def _gmlp_kernel(h_ref, gpre_ref, win_ref, lng_ref, lnb_ref, ws_ref, bs_ref, wout_ref, gpost_ref,
                 o_ref, *v_refs, chunk):
    x = h_ref[...]
    tm = x.shape[0]
    width = wout_ref.shape[0]
    hd = width // C_HEADS
    xn = _rmsnorm(x, gpre_ref[...])
    z = _gelu_erf(jnp.dot(xn.astype(BF16), win_ref[...], preferred_element_type=F32))
    u = z[:, :width]
    v = _layernorm(z[:, width:], lng_ref[...], lnb_ref[...])
    if chunk == 1:
        mixed = v * ws_ref[...] + bs_ref[...]
        v_refs[0][...] = v
    else:
        ri = lax.broadcasted_iota(jnp.int32, (chunk, chunk), 0)
        ci = lax.broadcasted_iota(jnp.int32, (chunk, chunk), 1)
        vb = v.astype(BF16)
        bias = bs_ref[...]
        cols = []
        for hh in range(C_HEADS):
            wtri = jnp.where(ri >= ci, ws_ref[hh], 0.0).astype(BF16)
            parts = [jnp.dot(wtri, vb[c0:c0 + chunk, hh * hd:(hh + 1) * hd], preferred_element_type=F32)
                     + bias[:, hh:hh + 1] for c0 in range(0, tm, chunk)]
            cols.append(jnp.concatenate(parts, axis=0) if len(parts) > 1 else parts[0])
        mixed = jnp.concatenate(cols, axis=1)
    y = u * mixed
    out = jnp.dot(y.astype(BF16), wout_ref[...], preferred_element_type=F32)
    o_ref[...] = x + _rmsnorm(out, gpost_ref[...])


def _gmlp(h, gpre, win, lng, lnb, ws, bs, wout, gpost, chunk, tm):
    rows, d = h.shape
    tm = min(tm, rows)
    width = wout.shape[0]
    out_specs = [pl.BlockSpec((tm, d), lambda i: (i, 0))]
    out_shape = [jax.ShapeDtypeStruct((rows, d), F32)]
    if chunk == 1:
        out_specs.append(pl.BlockSpec((tm, width), lambda i: (i, 0)))
        out_shape.append(jax.ShapeDtypeStruct((rows, width), F32))
    return pl.pallas_call(
        functools.partial(_gmlp_kernel, chunk=chunk),
        grid=(rows // tm,),
        in_specs=[pl.BlockSpec((tm, d), lambda i: (i, 0)), _const_spec((1, d)), _const_spec(win.shape),
                  _const_spec((1, width)), _const_spec((1, width)), _const_spec(ws.shape),
                  _const_spec(bs.shape), _const_spec(wout.shape), _const_spec((1, d))],
        out_specs=out_specs,
        out_shape=out_shape,
        compiler_params=_params(),
        name="gmlp",
    )(h, gpre, win, lng, lnb, ws, bs, wout, gpost)


def _ffn_kernel(h_ref, gpre_ref, wg_ref, wu_ref, cw_ref, cb_ref, wd_ref, gpost_ref, *refs,
                carried, tiles_per_seq):
    x = h_ref[...]
    tm = x.shape[0]
    f = wg_ref.shape[1]
    xn = _rmsnorm(x, gpre_ref[...]).astype(BF16)
    if carried:
        o_ref, tail_ref, carry_ref = refs

        @pl.when(pl.program_id(0) % tiles_per_seq == 0)
        def _():
            carry_ref[...] = jnp.zeros_like(carry_ref)
    else:
        s2_ref, s1_ref, o_ref, gp_ref = refs
    cw = cw_ref[...]
    cb = cb_ref[...]

    def in_proj(c0, c1):
        return (jnp.dot(xn, wg_ref[:, c0:c1], preferred_element_type=F32),
                jnp.dot(xn, wu_ref[:, c0:c1], preferred_element_type=F32))

    bounds = [(c0, min(c0 + FFN_COL_BLOCK, f)) for c0 in range(0, f, FFN_COL_BLOCK)]
    nxt = in_proj(*bounds[0])
    out = None
    for idx, (c0, c1) in enumerate(bounds):
        gp, up = nxt
        if idx + 1 < len(bounds):
            nxt = in_proj(*bounds[idx + 1])
        if carried:
            rows = lax.broadcasted_iota(jnp.int32, gp.shape, 0)
            t2, t1 = carry_ref[0:1, c0:c1], carry_ref[1:2, c0:c1]
            x1 = jnp.where(rows == 0, t1, pltpu.roll(gp, 1, 0))
            x2 = jnp.where(rows == 0, t2, jnp.where(rows == 1, t1, pltpu.roll(gp, 2, 0)))
            carry_ref[0:2, c0:c1] = gp[tm - 2:tm, :]
            tail_ref[0, :, c0:c1] = gp[tm - 2:tm, :]
        else:
            x2, x1 = s2_ref[:, c0:c1], s1_ref[:, c0:c1]
            gp_ref[:, c0:c1] = gp
        conv = cb[:, c0:c1] + cw[2:3, c0:c1] * gp + cw[0:1, c0:c1] * x2 + cw[1:2, c0:c1] * x1
        hid = (_gelu_tanh(conv) * up).astype(BF16)
        part = jnp.dot(hid, wd_ref[c0:c1, :], preferred_element_type=F32)
        out = part if out is None else out + part
    o_ref[...] = x + _rmsnorm(out, gpost_ref[...])


def _ffn(h, gpre, wg, wu, cw, cb, wd, gpost, tm, seq_len=None, state=None):
    rows, d = h.shape
    f = wg.shape[1]
    tm = min(tm, rows)
    carried = state is None
    row_spec = pl.BlockSpec((tm, d), lambda i: (i, 0))
    in_specs = [row_spec, _const_spec((1, d)), _const_spec((d, f)), _const_spec((d, f)),
                _const_spec(cw.shape), _const_spec((1, f)), _const_spec((f, d)), _const_spec((1, d))]
    args = [h, gpre, wg, wu, cw, cb, wd, gpost]
    if carried:
        tm = min(tm, seq_len)
        tiles_per_seq = seq_len // tm
        row_spec = pl.BlockSpec((tm, d), lambda i: (i, 0))
        in_specs[0] = row_spec
        out_specs = [row_spec, pl.BlockSpec((1, 2, f), lambda i: (i // tiles_per_seq, 0, 0))]
        out_shape = [jax.ShapeDtypeStruct((rows, d), F32),
                     jax.ShapeDtypeStruct((rows // seq_len, 2, f), F32)]
        scratch = [pltpu.VMEM((8, f), F32)]
    else:
        tiles_per_seq = 1
        in_specs += [pl.BlockSpec((tm, f), lambda i: (i, 0))] * 2
        args += [state[:, 0], state[:, 1]]
        out_specs = [row_spec, pl.BlockSpec((tm, f), lambda i: (i, 0))]
        out_shape = [jax.ShapeDtypeStruct((rows, d), F32), jax.ShapeDtypeStruct((rows, f), F32)]
        scratch = []
    return pl.pallas_call(
        functools.partial(_ffn_kernel, carried=carried, tiles_per_seq=tiles_per_seq),
        grid=(rows // tm,),
        in_specs=in_specs,
        out_specs=out_specs,
        out_shape=out_shape,
        scratch_shapes=scratch,
        compiler_params=_params(),
        name="conv_ffn",
    )(*args)


def kernel(x_prompt, x_sample, cache_win_k, cache_win_v, state_wkv, state_shift, state_ffn_conv,
           norm_mix_pre, norm_mix_post, norm_ffn_pre, norm_ffn_post,
           w_in_even, attn_sinks, shift_mu, decay_w0, decay_w2, iclr_a0, iclr_a2, gate_g2,
           key_k, key_a, bonus_r_k, gn_gain, gn_bias, w_out_even,
           w_in_odd, sgu_ln_gain, sgu_ln_bias, sgu_w, sgu_b, w_out_odd,
           ffn_w_gate, ffn_w_up, ffn_conv_w, ffn_conv_b, ffn_w_down):
    n_seq, t, d = x_prompt.shape
    n_dec, dec_t, _ = x_sample.shape
    assert dec_t == 1, "the sample path is written for one new token per sequence"
    depth = norm_mix_pre.shape[0]
    wp = min(WINDOW, t)
    row = lambda a: a.reshape(1, -1)

    hp = x_prompt.reshape(n_seq * t, d)
    hs = x_sample.reshape(n_dec, d)
    cos_p, sin_p = _rope_tables(jnp.arange(t, dtype=jnp.int32))
    cos_s, sin_s = _rope_tables(PAST_LEN + jnp.arange(1, dtype=jnp.int32))
    seg = jnp.kron(jnp.eye(HEADS_PER_GROUP, dtype=F32), jnp.ones((B_HEAD_DIM, B_HEAD_DIM), F32)).astype(BF16)

    kp_l, vp_l, ks_l, vs_l, sp_l, ss_l, shp_l, shs_l, vsgu_l, cp_l, cs_l = ([] for _ in range(11))
    (w_in_even, w_out_even, decay_w2, iclr_a2, gate_g2, w_in_odd, w_out_odd, ffn_w_gate, ffn_w_up,
     ffn_w_down) = (w.astype(BF16) for w in (w_in_even, w_out_even, decay_w2, iclr_a2, gate_g2, w_in_odd,
                                             w_out_odd, ffn_w_gate, ffn_w_up, ffn_w_down))
    for layer in range(depth):
        j = layer // 2
        g_pre, g_post = row(norm_mix_pre[layer]), row(norm_mix_post[layer])
        if layer % 2 == 0:
            w_in = w_in_even[j]
            w_out = w_out_even[j]
            sinks = row(attn_sinks[j])
            p = {"mu": row(shift_mu[j]), "w0": row(decay_w0[j]), "w2": decay_w2[j],
                 "a0": row(iclr_a0[j]), "a2": iclr_a2[j], "g2": gate_g2[j],
                 "k_k": row(key_k[j]), "k_a": row(key_a[j]), "r_k": row(bonus_r_k[j]),
                 "gn_g": row(gn_gain[j]), "gn_b": row(gn_bias[j]), "seg": seg}
            za, zb = _norm_proj(hp, g_pre, w_in, (A_PROJ, w_in.shape[1] - A_PROJ), tm=ROW_TILE)
            oa, kp, vp = _attn_prompt(za, cos_p, sin_p, sinks, n_seq, t)
            ob, sp = _rwkv_prompt(zb, p, n_seq, t)
            hp = _proj_residual(oa, ob, w_out[:A_Q], w_out[A_Q:], hp, g_post, tm=ROW_TILE)
            kp_l.append(kp.reshape(n_seq, wp, A_KV_HEADS, HEAD_DIM))
            vp_l.append(vp.reshape(n_seq, wp, A_KV_HEADS, HEAD_DIM))
            sp_l.append(sp)
            shp_l.append(zb.reshape(n_seq, t, -1)[:, -1])
            za, zb = _norm_proj(hs, g_pre, w_in, (A_PROJ, w_in.shape[1] - A_PROJ), tm=ROW_TILE)
            wb = cache_win_k.shape[2]
            oa, kn, vn = _attn_sample(za, cache_win_k[j].reshape(n_dec, wb, A_KV),
                                      cache_win_v[j].reshape(n_dec, wb, A_KV), cos_s, sin_s, sinks)
            ob, s_new = _rwkv_sample(zb, state_shift[j], state_wkv[j], p)
            hs = _proj_residual(oa, ob, w_out[:A_Q], w_out[A_Q:], hs, g_post, tm=ROW_TILE)
            ks_l.append(kn.reshape(n_dec, 1, A_KV_HEADS, HEAD_DIM))
            vs_l.append(vn.reshape(n_dec, 1, A_KV_HEADS, HEAD_DIM))
            ss_l.append(s_new)
            shs_l.append(zb)
        else:
            w_in = w_in_odd[j]
            w_out = w_out_odd[j]
            lng, lnb = row(sgu_ln_gain[j]), row(sgu_ln_bias[j])
            chunk = min(t, CHUNK)
            hd = w_out.shape[0] // C_HEADS
            hp, = _gmlp(hp, g_pre, w_in, lng, lnb, sgu_w[j][:, :chunk, :chunk], sgu_b[j][:, :chunk].T,
                        w_out, g_post, chunk=chunk, tm=GMLP_CHUNKS_PER_STEP * chunk)
            hs, vsg = _gmlp(hs, g_pre, w_in, lng, lnb, row(jnp.repeat(sgu_w[j][:, 0, 0], hd)),
                            row(jnp.repeat(sgu_b[j][:, 0], hd)), w_out, g_post, chunk=1, tm=ROW_TILE)
            vsgu_l.append(vsg.reshape(n_dec, 1, -1))
        f_pre, f_post = row(norm_ffn_pre[layer]), row(norm_ffn_post[layer])
        wg, wu, wd = ffn_w_gate[layer], ffn_w_up[layer], ffn_w_down[layer]
        cw, cb = ffn_conv_w[layer], row(ffn_conv_b[layer])
        hp, cp = _ffn(hp, f_pre, wg, wu, cw, cb, wd, f_post, tm=FFN_ROW_TILE, seq_len=t)
        hs, gp = _ffn(hs, f_pre, wg, wu, cw, cb, wd, f_post, tm=FFN_ROW_TILE, state=state_ffn_conv[layer])
        cp_l.append(cp)
        cs_l.append(jnp.stack([state_ffn_conv[layer][:, 1], gp], axis=1))

    return (hp.reshape(n_seq, t, d), hs.reshape(n_dec, 1, d),
            jnp.stack(kp_l), jnp.stack(vp_l), jnp.stack(ks_l), jnp.stack(vs_l),
            jnp.stack(sp_l), jnp.stack(ss_l), jnp.stack(shp_l), jnp.stack(shs_l),
            jnp.stack(vsgu_l), jnp.stack(cp_l), jnp.stack(cs_l))
```

```python
import functools

import jax
import jax.numpy as jnp
from jax import lax
from jax.experimental import pallas as pl
from jax.experimental.pallas import tpu as pltpu

F32 = jnp.float32
BF16 = jnp.bfloat16

PAST_LEN = 8192
HEAD_DIM = 64
A_HEADS = 8
A_KV_HEADS = 2
A_GROUP = A_HEADS // A_KV_HEADS
WINDOW = 128
ROPE_THETA = 10000.0
B_HEAD_DIM = 64
B_HEADS = 8
B_WIDTH = B_HEADS * B_HEAD_DIM
DECAY_RANK = 64
ICLR_RANK = 64
DECAY_SCALE = 0.606531
GN_EPS = 64e-5
CHUNK = 128
C_HEADS = 8
RMS_EPS = 1e-6
LN_EPS = 1e-5
NEG = -1e30
A_Q = A_HEADS * HEAD_DIM
A_KV = A_KV_HEADS * HEAD_DIM
A_PROJ = A_Q + 2 * A_KV

WKV_CHUNK = 64
WKV_CHUNKS_PER_STEP = 4
ATTN_BLOCKS_PER_STEP = 4
WKV_SAMPLE_HEADS_PER_STEP = 2
FFN_COL_BLOCK = 256
ROW_TILE = 512
FFN_ROW_TILE = 256
GMLP_CHUNKS_PER_STEP = 2
MXU_LANES = 256
HEADS_PER_GROUP = MXU_LANES // B_HEAD_DIM
V7X_VMEM_LIMIT_BYTES = 56 * 1024 * 1024

NT_DIMS = (((1,), (1,)), ((), ()))
TN_DIMS = (((0,), (0,)), ((), ()))


def _params(n_axes=1):
    return pltpu.CompilerParams(dimension_semantics=("arbitrary",) * n_axes,
                                vmem_limit_bytes=V7X_VMEM_LIMIT_BYTES)


def _rmsnorm(x, g):
    return x * lax.rsqrt(jnp.mean(x * x, axis=-1, keepdims=True) + RMS_EPS) * g


def _bdot(a, b):
    return jnp.dot(a.astype(BF16), b.astype(BF16), preferred_element_type=F32)


def _const_spec(shape):
    return pl.BlockSpec(shape, lambda *_: (0,) * len(shape))


class _Layer:
    def __init__(self, arr, layer):
        self.arr, self.layer = arr, layer

    @property
    def shape(self):
        return self.arr.shape[1:]


def _spec(x):
    if isinstance(x, _Layer):
        idx = (x.layer,) + (0,) * len(x.shape)
        return pl.BlockSpec((None,) + x.shape, lambda *_: idx)
    return _const_spec(x.shape)


def _arr(x):
    return x.arr if isinstance(x, _Layer) else x


def _norm_proj_kernel(x_ref, g_ref, w_ref, *out_refs, splits):
    xn = _rmsnorm(x_ref[...], g_ref[...])
    z = jnp.dot(xn.astype(BF16), w_ref[...], preferred_element_type=F32)
    off = 0
    for o_ref, n in zip(out_refs, splits):
        o_ref[...] = z[:, off:off + n]
        off += n


def _norm_proj(x, g, w, splits, tm):
    rows, d = x.shape
    n = w.shape[1]
    tm = min(tm, rows)
    return pl.pallas_call(
        functools.partial(_norm_proj_kernel, splits=splits),
        grid=(rows // tm,),
        in_specs=[pl.BlockSpec((tm, d), lambda i: (i, 0)), _spec(g), _spec(w)],
        out_specs=[pl.BlockSpec((tm, s), lambda i: (i, 0)) for s in splits],
        out_shape=[jax.ShapeDtypeStruct((rows, s), F32) for s in splits],
        compiler_params=_params(),
        name="norm_proj",
    )(x, _arr(g), _arr(w))


def _rope_tables(pos):
    half = HEAD_DIM // 2
    inv = ROPE_THETA ** (-jnp.arange(half, dtype=F32) / half)
    ang = pos.astype(F32)[:, None] * inv[None, :]
    cos, sin = jnp.cos(ang), jnp.sin(ang)
    cos2 = jnp.concatenate([cos, cos], axis=-1)
    sin2 = jnp.concatenate([-sin, sin], axis=-1)
    return jnp.tile(cos2, (1, 2)), jnp.tile(sin2, (1, 2))


def _rope(x, cos, sin):
    width = x.shape[1]
    half = HEAD_DIM // 2
    lane = lax.broadcasted_iota(jnp.int32, x.shape, 1)
    upper = pltpu.roll(x, width - half, 1)
    lower = pltpu.roll(x, half, 1)
    rot = jnp.where((lane % HEAD_DIM) < half, upper, lower)
    return x * cos + rot * sin


def _tile_lanes(t, reps):
    return jnp.concatenate([t] * reps, axis=1)


def _attn_prompt_kernel(cur_ref, prev_ref, cos_ref, sin_ref, cosp_ref, sinp_ref, sink_ref,
                        o_ref, k_out_ref, v_out_ref):
    w = prev_ref.shape[0]
    n_blk = cur_ref.shape[0] // w
    cur = cur_ref[...]
    prev = prev_ref[...]
    cos, sin = cos_ref[...], sin_ref[...]
    q = (_rope(cur[:, :A_Q], _tile_lanes(cos, A_HEADS // 2), _tile_lanes(sin, A_HEADS // 2))
         * (HEAD_DIM ** -0.5)).astype(BF16)
    k = _rope(cur[:, A_Q:A_Q + A_KV], cos, sin)
    v = cur[:, A_Q + A_KV:A_PROJ]
    kp = _rope(prev[:, A_Q:A_Q + A_KV], cosp_ref[...], sinp_ref[...])
    vp = prev[:, A_Q + A_KV:A_PROJ]
    k_out_ref[0] = k[(n_blk - 1) * w:, :]
    v_out_ref[0] = v[(n_blk - 1) * w:, :]

    qi = lax.broadcasted_iota(jnp.int32, (w, 2 * w), 0)
    kj = lax.broadcasted_iota(jnp.int32, (w, 2 * w), 1)
    diff = w + qi - kj
    band = (diff >= 0) & (diff <= WINDOW)
    first_band = band & ((pl.program_id(1) > 0) | (kj >= w))
    kk_all = jnp.concatenate([kp, k], axis=0).astype(BF16)
    vv_all = jnp.concatenate([vp, v], axis=0).astype(BF16)

    items = [(j, h) for j in range(n_blk) for h in range(A_HEADS)]
    kv_lanes = lambda h: slice((h // A_GROUP) * HEAD_DIM, (h // A_GROUP + 1) * HEAD_DIM)
    s = [lax.dot_general(q[j * w:(j + 1) * w, h * HEAD_DIM:(h + 1) * HEAD_DIM],
                         kk_all[j * w:(j + 2) * w, kv_lanes(h)], NT_DIMS, preferred_element_type=F32)
         for j, h in items]
    s = [jnp.where(first_band if j == 0 else band, x, NEG) for (j, h), x in zip(items, s)]
    sink = [sink_ref[:, h:h + 1] for j, h in items]
    m = [jnp.maximum(jnp.max(x, axis=-1, keepdims=True), sk) for x, sk in zip(s, sink)]
    p = [jnp.exp(x - mx) for x, mx in zip(s, m)]
    den = [jnp.sum(x, axis=-1, keepdims=True) + jnp.exp(sk - mx) for x, sk, mx in zip(p, sink, m)]
    o = [jnp.dot(x.astype(BF16), vv_all[j * w:(j + 2) * w, kv_lanes(h)], preferred_element_type=F32)
         for (j, h), x in zip(items, p)]
    o = [x / dn for x, dn in zip(o, den)]
    o_ref[...] = jnp.concatenate(
        [jnp.concatenate(o[j * A_HEADS:(j + 1) * A_HEADS], axis=1) for j in range(n_blk)], axis=0).astype(o_ref.dtype)


def _attn_prompt(za, cos, sin, sinks, n_seq, t):
    w = min(WINDOW, t)
    n_blk = ATTN_BLOCKS_PER_STEP if t % (w * ATTN_BLOCKS_PER_STEP) == 0 else 1
    tm = n_blk * w
    nt = t // tm
    cur_map = lambda b, i: (b * nt + i, 0)
    prev_map = lambda b, i: ((b * nt + i) * n_blk - jnp.minimum(i, 1), 0)
    tab_cur = lambda b, i: (i, 0)
    tab_prev = lambda b, i: (i * n_blk - jnp.minimum(i, 1), 0)
    return pl.pallas_call(
        _attn_prompt_kernel,
        grid=(n_seq, nt),
        in_specs=[pl.BlockSpec((tm, A_PROJ), cur_map), pl.BlockSpec((w, A_PROJ), prev_map),
                  pl.BlockSpec((tm, A_KV), tab_cur), pl.BlockSpec((tm, A_KV), tab_cur),
                  pl.BlockSpec((w, A_KV), tab_prev), pl.BlockSpec((w, A_KV), tab_prev),
                  _spec(sinks)],
        out_specs=[pl.BlockSpec((tm, A_Q), cur_map),
                   pl.BlockSpec((1, w, A_KV), lambda b, i: (b, 0, 0)),
                   pl.BlockSpec((1, w, A_KV), lambda b, i: (b, 0, 0))],
        out_shape=[jax.ShapeDtypeStruct((n_seq * t, A_Q), BF16),
                   jax.ShapeDtypeStruct((n_seq, w, A_KV), F32),
                   jax.ShapeDtypeStruct((n_seq, w, A_KV), F32)],
        compiler_params=_params(2),
        name="attn_prompt",
    )(za, za, cos, sin, cos, sin, _arr(sinks))


def _attn_sample_kernel(za_ref, ck_ref, cv_ref, cos_ref, sin_ref, sink_ref, o_ref, kn_ref, vn_ref):
    sb, wb = ck_ref.shape[0], ck_ref.shape[2]
    za = za_ref[...]
    cos, sin = cos_ref[...], sin_ref[...]
    q = _rope(za[:, :A_Q], _tile_lanes(cos, A_HEADS // 2), _tile_lanes(sin, A_HEADS // 2))
    k = _rope(za[:, A_Q:A_Q + A_KV], cos, sin)
    v = za[:, A_Q + A_KV:A_PROJ]
    kn_ref[...] = k
    vn_ref[...] = v
    rows = A_GROUP * sb
    row_seq = lax.broadcasted_iota(jnp.int32, (rows, sb * wb), 0) % sb
    col = lax.broadcasted_iota(jnp.int32, (rows, sb * wb), 1)
    own = (col >= row_seq * wb) & (col < (row_seq + 1) * wb)
    kvs = range(A_KV_HEADS)
    sl = [slice(kv * HEAD_DIM, (kv + 1) * HEAD_DIM) for kv in kvs]
    heads = [[kv * A_GROUP + g for g in range(A_GROUP)] for kv in kvs]
    kt = [jnp.concatenate([ck_ref[b, sl[kv], :] for b in range(sb)], axis=1).astype(BF16) for kv in kvs]
    vt = [jnp.concatenate([cv_ref[b, sl[kv], :] for b in range(sb)], axis=1).astype(BF16) for kv in kvs]
    qs =[jnp.concatenate([q[:, h * HEAD_DIM:(h + 1) * HEAD_DIM] for h in heads[kv]], axis=0) * (HEAD_DIM ** -0.5)
          for kv in kvs]
    sink = [jnp.concatenate([jnp.broadcast_to(sink_ref[:, h:h + 1], (sb, 1)) for h in heads[kv]], axis=0)
            for kv in kvs]
    s = [jnp.dot(qs[kv].astype(BF16), kt[kv], preferred_element_type=F32) for kv in kvs]
    s = [jnp.where(own, x, NEG) for x in s]
    k_rep = [jnp.concatenate([k[:, sl[kv]]] * A_GROUP, axis=0) for kv in kvs]
    v_rep = [jnp.concatenate([v[:, sl[kv]]] * A_GROUP, axis=0) for kv in kvs]
    s_new = [jnp.sum(qs[kv] * k_rep[kv], axis=-1, keepdims=True) for kv in kvs]
    m = [jnp.maximum(jnp.maximum(jnp.max(s[kv], axis=-1, keepdims=True), s_new[kv]), sink[kv]) for kv in kvs]
    p = [jnp.exp(s[kv] - m[kv]) for kv in kvs]
    p_new = [jnp.exp(s_new[kv] - m[kv]) for kv in kvs]
    den = [jnp.sum(p[kv], axis=-1, keepdims=True) + p_new[kv] + jnp.exp(sink[kv] - m[kv]) for kv in kvs]
    o = [lax.dot_general(p[kv].astype(BF16), vt[kv], NT_DIMS, preferred_element_type=F32) for kv in kvs]
    o = [(o[kv] + p_new[kv] * v_rep[kv]) / den[kv] for kv in kvs]
    o_ref[...] = jnp.concatenate([o[kv][g * sb:(g + 1) * sb] for kv in kvs for g in range(A_GROUP)],
                                 axis=1).astype(o_ref.dtype)


def _attn_sample(za, cache_kt, cache_vt, layer, cos, sin, sinks):
    n, wb = cache_kt.shape[1], cache_kt.shape[3]
    sb = min(16, n)
    cache_spec = pl.BlockSpec((None, sb, A_KV, wb), lambda i: (layer, i, 0, 0))
    return pl.pallas_call(
        _attn_sample_kernel,
        grid=(n // sb,),
        in_specs=[pl.BlockSpec((sb, A_PROJ), lambda i: (i, 0)), cache_spec, cache_spec,
                  _const_spec((1, A_KV)), _const_spec((1, A_KV)), _spec(sinks)],
        out_specs=[pl.BlockSpec((sb, A_Q), lambda i: (i, 0)),
                   pl.BlockSpec((sb, A_KV), lambda i: (i, 0)),
                   pl.BlockSpec((sb, A_KV), lambda i: (i, 0))],
        out_shape=[jax.ShapeDtypeStruct((n, A_Q), BF16),
                   jax.ShapeDtypeStruct((n, A_KV), F32),
                   jax.ShapeDtypeStruct((n, A_KV), F32)],
        compiler_params=_params(),
        name="attn_sample",
    )(za, cache_kt, cache_vt, cos, sin, _arr(sinks))


RWKV_PARAM_NAMES = ("mu", "w0", "w2", "a0", "a2", "g2", "k_k", "k_a", "r_k", "gn_g", "gn_b", "seg")


def _rwkv_param_specs(p):
    return [_spec(p[name]) for name in RWKV_PARAM_NAMES]


def _rwkv_param_args(p):
    return [_arr(p[name]) for name in RWKV_PARAM_NAMES]


def _rwkv_prep(zb, prev, p):
    o1, o2, o3 = B_WIDTH, 2 * B_WIDTH, 3 * B_WIDTH
    o4, o5 = o3 + DECAY_RANK, o3 + DECAY_RANK + ICLR_RANK
    zs = zb + (prev - zb) * p["mu"]
    r, k, v = zs[:, :o1], zs[:, o1:o2], zs[:, o2:o3]
    wd, ad, gd = zs[:, o3:o4], zs[:, o4:o5], zs[:, o5:]
    lw = -DECAY_SCALE * jax.nn.sigmoid(p["w0"] + _bdot(jnp.tanh(wd), p["w2"]))
    a = jax.nn.sigmoid(p["a0"] + _bdot(ad, p["a2"]))
    g = _bdot(jax.nn.sigmoid(gd), p["g2"])
    kk = k * p["k_k"]
    kk = kk / jnp.maximum(jnp.sqrt(_segsum(kk * kk, p["seg"])), 1e-12)
    k = k * (1.0 + (a - 1.0) * p["k_a"])
    return r, lw, k, v, kk, kk * a, g


def _split2(x):
    hi = x.astype(BF16)
    return hi, (x - hi.astype(F32)).astype(BF16)


def _segsum(x, seg):
    outs = []
    for g0 in range(0, x.shape[1], MXU_LANES):
        hi, lo = _split2(x[:, g0:g0 + MXU_LANES])
        outs.append(jnp.dot(hi, seg, preferred_element_type=F32) + jnp.dot(lo, seg, preferred_element_type=F32))
    return jnp.concatenate(outs, axis=1)


def _rwkv_finish(o, r, k, v, g, p):
    seg = p["seg"]
    mean = _segsum(o, seg) * (1.0 / B_HEAD_DIM)
    d = o - mean
    var = _segsum(d * d, seg) * (1.0 / B_HEAD_DIM)
    o = d * lax.rsqrt(var + GN_EPS) * p["gn_g"] + p["gn_b"]
    bonus = _segsum(r * k * p["r_k"], seg) * v
    return (o + bonus) * g


def _split3(x):
    hi = x.astype(BF16)
    r1 = x - hi.astype(F32)
    mid = r1.astype(BF16)
    return hi, mid, (r1 - mid.astype(F32)).astype(BF16)


def _mm(a, b):
    return jnp.dot(a.astype(BF16), b.astype(BF16), preferred_element_type=F32)


def _head_block_diag(x, bd_mask):
    tiled = jnp.concatenate([x.astype(BF16)] * HEADS_PER_GROUP, axis=0)
    return jnp.where(bd_mask, tiled, jnp.zeros_like(tiled))


def _rwkv_chunk_kernel(zb_ref, *refs, n_sub):
    n_p = len(RWKV_PARAM_NAMES)
    p = {name: ref[...] for name, ref in zip(RWKV_PARAM_NAMES, refs[:n_p])}
    o_ref, s_out_ref, carry_ref, state_ref = refs[n_p:]
    n_rows = zb_ref.shape[0]
    c = n_rows // n_sub
    d = B_HEAD_DIM

    @pl.when(pl.program_id(1) == 0)
    def _():
        carry_ref[...] = jnp.zeros_like(carry_ref)
        state_ref[...] = jnp.zeros_like(state_ref)

    zb = zb_ref[...]
    rows = lax.broadcasted_iota(jnp.int32, zb.shape, 0)
    prev = jnp.where(rows == 0, carry_ref[0:1, :], pltpu.roll(zb, 1, 0))
    carry_ref[0:1, :] = zb[n_rows - 1:n_rows, :]
    r, lw, k, v, kk, b, g = _rwkv_prep(zb, prev, p)

    ti = lax.broadcasted_iota(jnp.int32, (c, MXU_LANES), 0)
    si = lax.broadcasted_iota(jnp.int32, (c, MXU_LANES), 1) % d
    lane_head = lax.broadcasted_iota(jnp.int32, (c, MXU_LANES), 1) // d
    incl, strict = ti >= si, ti > si
    eye = (ti == si).astype(F32)
    bd_mask = (lax.broadcasted_iota(jnp.int32, (MXU_LANES, MXU_LANES), 0) // c
               == lax.broadcasted_iota(jnp.int32, (MXU_LANES, MXU_LANES), 1) // d)
    tri = (lax.broadcasted_iota(jnp.int32, (c, c), 0) >= lax.broadcasted_iota(jnp.int32, (c, c), 1)).astype(BF16)
    bd = functools.partial(_head_block_diag, bd_mask=bd_mask)

    items = []
    for sub in range(n_sub):
        rs = slice(sub * c, (sub + 1) * c)
        lw_c = lw[rs]
        gcum = sum(jnp.dot(tri, part, preferred_element_type=F32) for part in _split3(lw_c))
        g_last = gcum[c - 1:c, :]
        e_neg = jnp.exp(-gcum)
        e_hat = jnp.exp(g_last - gcum)
        kap = (kk[rs] * jnp.exp(gcum - lw_c)).astype(BF16)
        rt = r[rs] * jnp.exp(gcum)
        kt, bt = (k[rs] * e_neg).astype(BF16), (b[rs] * e_neg).astype(BF16)
        khat, bhat = (k[rs] * e_hat).astype(BF16), (b[rs] * e_hat).astype(BF16)
        dec_last = jnp.exp(g_last)
        v_c = v[rs].astype(BF16)
        for g0 in range(0, B_WIDTH, MXU_LANES):
            gl = slice(g0, g0 + MXU_LANES)
            items.append(dict(g0=g0, kap=kap[:, gl], rt=rt[:, gl], kt=kt[:, gl], bt=bt[:, gl], khat=khat[:, gl],
                              bhat=bhat[:, gl], v=v_c[:, gl], dec=dec_last[:, gl]))
    for it in items:
        lhs = jnp.concatenate([it["kap"], it["rt"].astype(BF16)], axis=0)
        it["m_k"] = lax.dot_general(lhs, bd(it["kt"]), NT_DIMS, preferred_element_type=F32)
        it["m_b"] = lax.dot_general(lhs, bd(it["bt"]), NT_DIMS, preferred_element_type=F32)
    for it in items:
        it["a_k"] = jnp.where(strict, it["m_k"][:c], 0.0)
        it["p_k"] = jnp.where(incl, it["m_k"][c:], 0.0)
        it["p_b"] = jnp.where(incl, it["m_b"][c:], 0.0).astype(BF16)
        it["pw"] = -jnp.where(strict, it["m_b"][:c], 0.0)
        it["pw_bd"] = bd(it["pw"])
        it["t_inv"] = eye + it["pw"]
    span = 2
    while span < c:
        for it in items:
            it["pw"] = _mm(it["pw"], it["pw_bd"])
        for it in items:
            it["pw_bd"] = bd(it["pw"])
            it["t_inv"] = it["t_inv"] + _mm(it["t_inv"], it["pw_bd"])
        span *= 2
    for it in items:
        it["v_bd"] = bd(it["v"])
        it["akv"] = _mm(it["a_k"], it["v_bd"])
    for it in items:
        it["k_p"] = _mm(it["t_inv"], bd(it["kap"]))
        it["u0"] = _mm(it["t_inv"], bd(it["akv"]))
    for it in items:
        it["w_bd"] = jnp.where(bd_mask, lax.dot_general(it["k_p"].astype(BF16), it["bhat"], TN_DIMS,
                                                        preferred_element_type=F32), 0.0).astype(BF16)
        s1_full = lax.dot_general(jnp.concatenate([it["v"], it["u0"].astype(BF16)], axis=0),
                                  jnp.concatenate([it["khat"], -it["bhat"]], axis=0),
                                  TN_DIMS, preferred_element_type=F32)
        it["s1"] = sum(jnp.where(lane_head == h, s1_full[h * d:(h + 1) * d], 0.0) for h in range(HEADS_PER_GROUP))
        it["r_p"] = (it["rt"] - _mm(it["p_b"], bd(it["k_p"]))).astype(BF16)
        it["o0"] = _mm(it["p_k"], it["v_bd"]) - _mm(it["p_b"], bd(it["u0"]))
    state = {g0: state_ref[:, g0:g0 + MXU_LANES] for g0 in range(0, B_WIDTH, MXU_LANES)}
    o_parts = []
    for it in items:
        s0 = state[it["g0"]]
        o_parts.append(lax.dot_general(it["r_p"], bd(s0), NT_DIMS, preferred_element_type=F32) + it["o0"])
        state[it["g0"]] = s0 * it["dec"] - _mm(s0, it["w_bd"]) + it["s1"]
    for g0, s_new in state.items():
        state_ref[:, g0:g0 + MXU_LANES] = s_new
    n_groups = B_WIDTH // MXU_LANES
    o = jnp.concatenate([jnp.concatenate(o_parts[i:i + n_groups], axis=1)
                         for i in range(0, len(o_parts), n_groups)], axis=0)
    o_ref[...] = _rwkv_finish(o, r, k, v, g, p).astype(o_ref.dtype)
    for h in range(B_HEADS):
        s_out_ref[0, h] = state_ref[:, h * d:(h + 1) * d]


def _rwkv_prompt(zb, p, n_seq, t):
    c = WKV_CHUNK
    assert c == B_HEAD_DIM and t % c == 0, "head packing puts WKV_CHUNK time steps where a head's lanes go"
    n_sub = WKV_CHUNKS_PER_STEP if t % (c * WKV_CHUNKS_PER_STEP) == 0 else 1
    tm = c * n_sub
    nc = t // tm
    width = zb.shape[1]
    return pl.pallas_call(
        functools.partial(_rwkv_chunk_kernel, n_sub=n_sub),
        grid=(n_seq, nc),
        in_specs=[pl.BlockSpec((tm, width), lambda b, i: (b * nc + i, 0))] + _rwkv_param_specs(p),
        out_specs=[pl.BlockSpec((tm, B_WIDTH), lambda b, i: (b * nc + i, 0)),
                   pl.BlockSpec((1, B_HEADS, B_HEAD_DIM, B_HEAD_DIM), lambda b, i: (b, 0, 0, 0))],
        out_shape=[jax.ShapeDtypeStruct((n_seq * t, B_WIDTH), BF16),
                   jax.ShapeDtypeStruct((n_seq, B_HEADS, B_HEAD_DIM, B_HEAD_DIM), F32)],
        scratch_shapes=[pltpu.VMEM((8, width), F32),
                        pltpu.VMEM((B_HEAD_DIM, B_WIDTH), F32)],
        compiler_params=_params(2),
        name="rwkv_prompt",
    )(zb, *_rwkv_param_args(p))


def _rwkv_sample_kernel(zb_ref, shift_ref, s_ref, s_all_ref, *refs, heads_per_step):
    del s_all_ref
    n_p = len(RWKV_PARAM_NAMES)
    p_refs = refs[:n_p]
    o_ref, s_out_ref, vec_ref, keep_ref, ot_ref = refs[n_p:]
    d = B_HEAD_DIM
    step = pl.program_id(0)

    @pl.when(step == 0)
    def _():
        p = {name: ref[...] for name, ref in zip(RWKV_PARAM_NAMES, p_refs)}
        r, lw, k, v, kk, b, g = _rwkv_prep(zb_ref[...], shift_ref[...], p)
        for idx, val in enumerate((r, jnp.exp(lw), k, v, kk, b)):
            vec_ref[idx] = val.T
        for idx, val in enumerate((r, k, v, g)):
            keep_ref[idx] = val

    for hh in range(heads_per_step):
        base = pl.multiple_of((step * heads_per_step + hh) * d, d)
        r_h, w_h, k_h, kk_h, b_h = (vec_ref[idx, pl.ds(base, d), :] for idx in (0, 1, 2, 4, 5))

        def value_row(i, carry, hh=hh, base=base, r_h=r_h, w_h=w_h, k_h=k_h, kk_h=kk_h, b_h=b_h):
            rows = pl.ds(pl.multiple_of(hh * d * d + i * d, d), d)
            s_i = s_ref[rows, :]
            sa = jnp.sum(s_i * kk_h, axis=0, keepdims=True)
            s_i = s_i * w_h - sa * b_h + vec_ref[3, pl.ds(base + i, 1), :] * k_h
            s_out_ref[rows, :] = s_i
            ot_ref[pl.ds(base + i, 1), :] = jnp.sum(s_i * r_h, axis=0, keepdims=True)
            return carry

        lax.fori_loop(0, d, value_row, 0, unroll=4)

    @pl.when(step == pl.num_programs(0) - 1)
    def _():
        p = {name: ref[...] for name, ref in zip(RWKV_PARAM_NAMES, p_refs)}
        o_ref[...] = _rwkv_finish(ot_ref[...].T, keep_ref[0], keep_ref[1], keep_ref[2], keep_ref[3],
                                  p).astype(o_ref.dtype)


def _rwkv_sample(zb, shift_all, state_t, s_all, layer, p):
    n, width = zb.shape
    d = B_HEAD_DIM
    hps = WKV_SAMPLE_HEADS_PER_STEP
    state_spec = pl.BlockSpec((None, hps * d * d, n), lambda i: (layer, i, 0))
    return pl.pallas_call(
        functools.partial(_rwkv_sample_kernel, heads_per_step=hps),
        grid=(B_HEADS // hps,),
        in_specs=[_const_spec((n, width)), pl.BlockSpec((None, n, width), lambda i: (layer, 0, 0)),
                  state_spec, pl.BlockSpec(memory_space=pl.ANY)] + _rwkv_param_specs(p),
        out_specs=[_const_spec((n, B_WIDTH)), state_spec],
        out_shape=[jax.ShapeDtypeStruct((n, B_WIDTH), BF16), jax.ShapeDtypeStruct(s_all.shape, F32)],
        input_output_aliases={3: 1},
        scratch_shapes=[pltpu.VMEM((6, B_WIDTH, n), F32), pltpu.VMEM((4, n, B_WIDTH), F32),
                        pltpu.VMEM((B_WIDTH, n), F32)],
        compiler_params=_params(),
        name="rwkv_sample",
    )(zb, shift_all, state_t, s_all, *_rwkv_param_args(p))


def _proj_residual_kernel(a_ref, b_ref, w_ref, h_ref, g_ref, o_ref):
    ka = a_ref.shape[1]
    acc = jnp.dot(a_ref[...], w_ref[:ka, :], preferred_element_type=F32)
    acc = acc + jnp.dot(b_ref[...], w_ref[ka:, :], preferred_element_type=F32)
    o_ref[...] = h_ref[...] + _rmsnorm(acc, g_ref[...])


def _proj_residual(a, b, w, h, g, tm):
    rows, d = h.shape
    tm = min(tm, rows)
    ka, kb = a.shape[1], b.shape[1]
    return pl.pallas_call(
        _proj_residual_kernel,
        grid=(rows // tm,),
        in_specs=[pl.BlockSpec((tm, ka), lambda i: (i, 0)), pl.BlockSpec((tm, kb), lambda i: (i, 0)),
                  _spec(w), pl.BlockSpec((tm, d), lambda i: (i, 0)), _spec(g)],
        out_specs=pl.BlockSpec((tm, d), lambda i: (i, 0)),
        out_shape=jax.ShapeDtypeStruct((rows, d), F32),
        compiler_params=_params(),
        name="proj_residual",
    )(a, b, _arr(w), h, _arr(g))


def _gelu_erf(x):
    return 0.5 * x * (1.0 + lax.erf(x * 0.7071067811865476))


def _gelu_tanh(x):
    c = 0.7978845608028654
    return x * (0.5 + 0.5 * jnp.tanh(x * (c + (0.044715 * c) * (x * x))))


def _layernorm(x, g, b):
    mu = jnp.mean(x, axis=-1, keepdims=True)
    d = x - mu
    var = jnp.mean(d * d, axis=-1, keepdims=True)
    return d * lax.rsqrt(var + LN_EPS) * g + b


def _gmlp_kernel(h_ref, gpre_ref, win_ref, lng_ref, lnb_ref, ws_ref, bs_ref, wout_ref, gpost_ref,
                 o_ref, *v_refs, chunk):
    x = h_ref[...]
    tm = x.shape[0]
    width = wout_ref.shape[0]
    hd = width // C_HEADS
    xn = _rmsnorm(x, gpre_ref[...])
    z = _gelu_erf(jnp.dot(xn.astype(BF16), win_ref[...], preferred_element_type=F32))
    u = z[:, :width]
    v = _layernorm(z[:, width:], lng_ref[...], lnb_ref[...])
    if chunk == 1:
        mixed = v * ws_ref[...] + bs_ref[...]
        v_refs[0][...] = v
    else:
        ri = lax.broadcasted_iota(jnp.int32, (chunk, chunk), 0)
        ci = lax.broadcasted_iota(jnp.int32, (chunk, chunk), 1)
        vb = v.astype(BF16)
        bias = bs_ref[...]
        cols = []
        for hh in range(C_HEADS):
            wtri = jnp.where(ri >= ci, ws_ref[hh], 0.0).astype(BF16)
            parts = [jnp.dot(wtri, vb[c0:c0 + chunk, hh * hd:(hh + 1) * hd], preferred_element_type=F32)
                     + bias[:, hh:hh + 1] for c0 in range(0, tm, chunk)]
            cols.append(jnp.concatenate(parts, axis=0) if len(parts) > 1 else parts[0])
        mixed = jnp.concatenate(cols, axis=1)
    y = u * mixed
    out = jnp.dot(y.astype(BF16), wout_ref[...], preferred_element_type=F32)
    o_ref[...] = x + _rmsnorm(out, gpost_ref[...])


def _gmlp(h, gpre, win, lng, lnb, ws, bs, wout, gpost, chunk, tm):
    rows, d = h.shape
    tm = min(tm, rows)
    width = wout.shape[0]
    out_specs = [pl.BlockSpec((tm, d), lambda i: (i, 0))]
    out_shape = [jax.ShapeDtypeStruct((rows, d), F32)]
    if chunk == 1:
        out_specs.append(pl.BlockSpec((tm, width), lambda i: (i, 0)))
        out_shape.append(jax.ShapeDtypeStruct((rows, width), F32))
    return pl.pallas_call(
        functools.partial(_gmlp_kernel, chunk=chunk),
        grid=(rows // tm,),
        in_specs=[pl.BlockSpec((tm, d), lambda i: (i, 0))]
        + [_spec(x) for x in (gpre, win, lng, lnb, ws, bs, wout, gpost)],
        out_specs=out_specs,
        out_shape=out_shape,
        compiler_params=_params(),
        name="gmlp",
    )(h, *[_arr(x) for x in (gpre, win, lng, lnb, ws, bs, wout, gpost)])


def _ffn_kernel(h_ref, gpre_ref, wg_ref, wu_ref, cw_ref, cb_ref, wd_ref, gpost_ref, *refs,
                carried, tiles_per_seq):
    x = h_ref[...]
    tm = x.shape[0]
    f = wg_ref.shape[1]
    xn = _rmsnorm(x, gpre_ref[...]).astype(BF16)
    if carried:
        o_ref, tail_ref, carry_ref = refs

        @pl.when(pl.program_id(0) % tiles_per_seq == 0)
        def _():
            carry_ref[...] = jnp.zeros_like(carry_ref)
    else:
        s2_ref, s1_ref, o_ref, gp_ref = refs
    cw = cw_ref[...]
    cb = cb_ref[...]

    def in_proj(c0, c1):
        return (jnp.dot(xn, wg_ref[:, c0:c1], preferred_element_type=F32),
                jnp.dot(xn, wu_ref[:, c0:c1], preferred_element_type=F32))

    bounds = [(c0, min(c0 + FFN_COL_BLOCK, f)) for c0 in range(0, f, FFN_COL_BLOCK)]
    nxt = in_proj(*bounds[0])
    out = None
    for idx, (c0, c1) in enumerate(bounds):
        gp, up = nxt
        if idx + 1 < len(bounds):
            nxt = in_proj(*bounds[idx + 1])
        if carried:
            rows = lax.broadcasted_iota(jnp.int32, gp.shape, 0)
            t2, t1 = carry_ref[0:1, c0:c1], carry_ref[1:2, c0:c1]
            x1 = jnp.where(rows == 0, t1, pltpu.roll(gp, 1, 0))
            x2 = jnp.where(rows == 0, t2, jnp.where(rows == 1, t1, pltpu.roll(gp, 2, 0)))
            carry_ref[0:2, c0:c1] = gp[tm - 2:tm, :]
            tail_ref[0, :, c0:c1] = gp[tm - 2:tm, :]
        else:
            x2, x1 = s2_ref[:, c0:c1], s1_ref[:, c0:c1]
            gp_ref[:, c0:c1] = gp
        conv = cb[:, c0:c1] + cw[2:3, c0:c1] * gp + cw[0:1, c0:c1] * x2 + cw[1:2, c0:c1] * x1
        hid = (_gelu_tanh(conv) * up).astype(BF16)
        part = jnp.dot(hid, wd_ref[c0:c1, :], preferred_element_type=F32)
        out = part if out is None else out + part
    o_ref[...] = x + _rmsnorm(out, gpost_ref[...])


def _ffn(h, gpre, wg, wu, cw, cb, wd, gpost, tm, seq_len=None, state=None):
    rows, d = h.shape
    f = wg.shape[1]
    tm = min(tm, rows)
    carried = state is None
    row_spec = pl.BlockSpec((tm, d), lambda i: (i, 0))
    params = (gpre, wg, wu, cw, cb, wd, gpost)
    in_specs = [row_spec] + [_spec(x) for x in params]
    args = [h] + [_arr(x) for x in params]
    if carried:
        tm = min(tm, seq_len)
        tiles_per_seq = seq_len // tm
        row_spec = pl.BlockSpec((tm, d), lambda i: (i, 0))
        in_specs[0] = row_spec
        out_specs = [row_spec, pl.BlockSpec((1, 2, f), lambda i: (i // tiles_per_seq, 0, 0))]
        out_shape = [jax.ShapeDtypeStruct((rows, d), F32),
                     jax.ShapeDtypeStruct((rows // seq_len, 2, f), F32)]
        scratch = [pltpu.VMEM((8, f), F32)]
    else:
        tiles_per_seq = 1
        in_specs += [pl.BlockSpec((None, None, tm, f), lambda i, tap=tap: (state.layer, tap, i, 0)) for tap in (0, 1)]
        args += [state.arr, state.arr]
        out_specs = [row_spec, pl.BlockSpec((tm, f), lambda i: (i, 0))]
        out_shape = [jax.ShapeDtypeStruct((rows, d), F32), jax.ShapeDtypeStruct((rows, f), F32)]
        scratch = []
    return pl.pallas_call(
        functools.partial(_ffn_kernel, carried=carried, tiles_per_seq=tiles_per_seq),
        grid=(rows // tm,),
        in_specs=in_specs,
        out_specs=out_specs,
        out_shape=out_shape,
        scratch_shapes=scratch,
        compiler_params=_params(),
        name="conv_ffn",
    )(*args)


def kernel(x_prompt, x_sample, cache_win_k, cache_win_v, state_wkv, state_shift, state_ffn_conv,
           norm_mix_pre, norm_mix_post, norm_ffn_pre, norm_ffn_post,
           w_in_even, attn_sinks, shift_mu, decay_w0, decay_w2, iclr_a0, iclr_a2, gate_g2,
           key_k, key_a, bonus_r_k, gn_gain, gn_bias, w_out_even,
           w_in_odd, sgu_ln_gain, sgu_ln_bias, sgu_w, sgu_b, w_out_odd,
           ffn_w_gate, ffn_w_up, ffn_conv_w, ffn_conv_b, ffn_w_down):
    n_seq, t, d = x_prompt.shape
    n_dec, dec_t, _ = x_sample.shape
    assert dec_t == 1, "the sample path is written for one new token per sequence"
    depth = norm_mix_pre.shape[0]
    n_even = w_in_even.shape[0]
    wp = min(WINDOW, t)
    wb = cache_win_k.shape[2]
    hd = B_HEAD_DIM
    chunk = min(t, CHUNK)
    assert chunk == sgu_w.shape[-1], "the prompt is mixed in whole CHUNK-position chunks"

    hp = x_prompt.reshape(n_seq * t, d)
    hs = x_sample.reshape(n_dec, d)
    cos_p, sin_p = _rope_tables(jnp.arange(t, dtype=jnp.int32))
    cos_s, sin_s = _rope_tables(PAST_LEN + jnp.arange(1, dtype=jnp.int32))
    seg = jnp.kron(jnp.eye(HEADS_PER_GROUP, dtype=F32), jnp.ones((hd, hd), F32)).astype(BF16)

    rows3 = lambda a: a.reshape(a.shape[0], 1, -1)
    bf = lambda a: a.astype(BF16)
    stacked = dict(
        g_pre=rows3(norm_mix_pre), g_post=rows3(norm_mix_post), f_pre=rows3(norm_ffn_pre),
        f_post=rows3(norm_ffn_post), w_in_even=bf(w_in_even), w_out_even=bf(w_out_even),
        sinks=rows3(attn_sinks), mu=rows3(shift_mu), w0=rows3(decay_w0), w2=bf(decay_w2), a0=rows3(iclr_a0),
        a2=bf(iclr_a2), g2=bf(gate_g2), k_k=rows3(key_k), k_a=rows3(key_a), r_k=rows3(bonus_r_k),
        gn_g=rows3(gn_gain), gn_b=rows3(gn_bias), w_in_odd=bf(w_in_odd), w_out_odd=bf(w_out_odd),
        ln_g=rows3(sgu_ln_gain), ln_b=rows3(sgu_ln_bias), sgu_w=sgu_w,
        sgu_bt=jnp.swapaxes(sgu_b, 1, 2),
        sgu_w0=rows3(jnp.repeat(sgu_w[:, :, 0, 0], sgu_ln_gain.shape[1] // C_HEADS, axis=1)),
        sgu_b0=rows3(jnp.repeat(sgu_b[:, :, 0], sgu_ln_gain.shape[1] // C_HEADS, axis=1)),
        wg=bf(ffn_w_gate), wu=bf(ffn_w_up), wd=bf(ffn_w_down), cw=ffn_conv_w, cb=rows3(ffn_conv_b))
    cache_kt = jnp.transpose(cache_win_k, (0, 1, 3, 4, 2)).reshape(n_even, n_dec, A_KV, wb)
    cache_vt = jnp.transpose(cache_win_v, (0, 1, 3, 4, 2)).reshape(n_even, n_dec, A_KV, wb)
    wkv_t = jnp.transpose(state_wkv, (0, 2, 3, 4, 1)).reshape(n_even, B_HEADS * hd * hd, n_dec)
    wkv_new_t = jnp.zeros_like(wkv_t)
    conv_taps = jnp.swapaxes(state_ffn_conv, 1, 2)

    kp_l, vp_l, ks_l, vs_l, sp_l, shp_l, shs_l, vsgu_l, cp_l, cs_l = ([] for _ in range(10))
    for layer in range(depth):
        j = layer // 2
        at = lambda name, idx: _Layer(stacked[name], idx)
        g_pre, g_post = at("g_pre", layer), at("g_post", layer)
        if layer % 2 == 0:
            w_in, w_out, sinks = at("w_in_even", j), at("w_out_even", j), at("sinks", j)
            p = {name: at(name, j) for name in RWKV_PARAM_NAMES if name != "seg"}
            p["seg"] = seg
            splits = (A_PROJ, w_in.shape[1] - A_PROJ)
            za, zb = _norm_proj(hp, g_pre, w_in, splits, tm=ROW_TILE)
            oa, kp, vp = _attn_prompt(za, cos_p, sin_p, sinks, n_seq, t)
            ob, sp = _rwkv_prompt(zb, p, n_seq, t)
            hp = _proj_residual(oa, ob, w_out, hp, g_post, tm=ROW_TILE)
            kp_l.append(kp.reshape(n_seq, wp, A_KV_HEADS, HEAD_DIM))
            vp_l.append(vp.reshape(n_seq, wp, A_KV_HEADS, HEAD_DIM))
            sp_l.append(sp)
            shp_l.append(zb.reshape(n_seq, t, -1)[:, -1])
            za, zb = _norm_proj(hs, g_pre, w_in, splits, tm=ROW_TILE)
            oa, kn, vn = _attn_sample(za, cache_kt, cache_vt, j, cos_s, sin_s, sinks)
            ob, wkv_new_t = _rwkv_sample(zb, state_shift, wkv_t, wkv_new_t, j, p)
            hs = _proj_residual(oa, ob, w_out, hs, g_post, tm=ROW_TILE)
            ks_l.append(kn.reshape(n_dec, 1, A_KV_HEADS, HEAD_DIM))
            vs_l.append(vn.reshape(n_dec, 1, A_KV_HEADS, HEAD_DIM))
            shs_l.append(zb)
        else:
            w_in, w_out = at("w_in_odd", j), at("w_out_odd", j)
            lng, lnb = at("ln_g", j), at("ln_b", j)
            hp, = _gmlp(hp, g_pre, w_in, lng, lnb, at("sgu_w", j), at("sgu_bt", j), w_out, g_post,
                        chunk=chunk, tm=GMLP_CHUNKS_PER_STEP * chunk)
            hs, vsg = _gmlp(hs, g_pre, w_in, lng, lnb, at("sgu_w0", j), at("sgu_b0", j), w_out, g_post,
                            chunk=1, tm=ROW_TILE)
            vsgu_l.append(vsg.reshape(n_dec, 1, -1))
        ffn_params = [at(name, layer) for name in ("f_pre", "wg", "wu", "cw", "cb", "wd", "f_post")]
        hp, cp = _ffn(hp, *ffn_params, tm=FFN_ROW_TILE, seq_len=t)
        hs, gp = _ffn(hs, *ffn_params, tm=FFN_ROW_TILE, state=_Layer(conv_taps, layer))
        cp_l.append(cp)
        cs_l.append(jnp.stack([state_ffn_conv[layer][:, 1], gp], axis=1))

    wkv_sample = jnp.transpose(wkv_new_t.reshape(n_even, B_HEADS, hd, hd, n_dec), (0, 4, 1, 2, 3))
    return (hp.reshape(n_seq, t, d), hs.reshape(n_dec, 1, d),
            jnp.stack(kp_l), jnp.stack(vp_l), jnp.stack(ks_l), jnp.stack(vs_l),
            jnp.stack(sp_l), wkv_sample, jnp.stack(shp_l), jnp.stack(shs_l),
            jnp.stack(vsgu_l), jnp.stack(cp_l), jnp.stack(cs_l))
```

```python
import functools

import jax
import jax.numpy as jnp
from jax import lax
from jax.experimental import pallas as pl
from jax.experimental.pallas import tpu as pltpu

F32 = jnp.float32
BF16 = jnp.bfloat16

PAST_LEN = 8192
HEAD_DIM = 64
A_HEADS = 8
A_KV_HEADS = 2
A_GROUP = A_HEADS // A_KV_HEADS
WINDOW = 128
ROPE_THETA = 10000.0
B_HEAD_DIM = 64
B_HEADS = 8
B_WIDTH = B_HEADS * B_HEAD_DIM
DECAY_RANK = 64
ICLR_RANK = 64
DECAY_SCALE = 0.606531
GN_EPS = 64e-5
CHUNK = 128
C_HEADS = 8
RMS_EPS = 1e-6
LN_EPS = 1e-5
NEG = -1e30
A_Q = A_HEADS * HEAD_DIM
A_KV = A_KV_HEADS * HEAD_DIM
A_PROJ = A_Q + 2 * A_KV

WKV_CHUNK = 64
WKV_CHUNKS_PER_STEP = 4
ATTN_BLOCKS_PER_STEP = 4
WKV_SAMPLE_HEADS_PER_STEP = 2
FFN_COL_BLOCK = 256
ROW_TILE = 512
FFN_ROW_TILE = 256
GMLP_CHUNKS_PER_STEP = 2
GMLP_COL_BLOCK = 256
MXU_LANES = 256
HEADS_PER_GROUP = MXU_LANES // B_HEAD_DIM
V7X_VMEM_LIMIT_BYTES = 56 * 1024 * 1024

NT_DIMS = (((1,), (1,)), ((), ()))
TN_DIMS = (((0,), (0,)), ((), ()))


def _params(n_axes=1):
    return pltpu.CompilerParams(dimension_semantics=("arbitrary",) * n_axes,
                                vmem_limit_bytes=V7X_VMEM_LIMIT_BYTES)


def _rmsnorm(x, g):
    return x * lax.rsqrt(jnp.mean(x * x, axis=-1, keepdims=True) + RMS_EPS) * g


def _bdot(a, b):
    return jnp.dot(a.astype(BF16), b.astype(BF16), preferred_element_type=F32)


def _const_spec(shape):
    return pl.BlockSpec(shape, lambda *_: (0,) * len(shape))


class _Layer:
    def __init__(self, arr, layer):
        self.arr, self.layer = arr, layer

    @property
    def shape(self):
        return self.arr.shape[1:]


def _spec(x):
    if isinstance(x, _Layer):
        idx = (x.layer,) + (0,) * len(x.shape)
        return pl.BlockSpec((None,) + x.shape, lambda *_: idx)
    return _const_spec(x.shape)


def _arr(x):
    return x.arr if isinstance(x, _Layer) else x


def _norm_proj_kernel(x_ref, g_ref, w_ref, *out_refs, splits):
    xn = _rmsnorm(x_ref[...], g_ref[...])
    z = jnp.dot(xn.astype(BF16), w_ref[...], preferred_element_type=F32)
    off = 0
    for o_ref, n in zip(out_refs, splits):
        o_ref[...] = z[:, off:off + n]
        off += n


def _norm_proj(x, g, w, splits, tm):
    rows, d = x.shape
    n = w.shape[1]
    tm = min(tm, rows)
    return pl.pallas_call(
        functools.partial(_norm_proj_kernel, splits=splits),
        grid=(rows // tm,),
        in_specs=[pl.BlockSpec((tm, d), lambda i: (i, 0)), _spec(g), _spec(w)],
        out_specs=[pl.BlockSpec((tm, s), lambda i: (i, 0)) for s in splits],
        out_shape=[jax.ShapeDtypeStruct((rows, s), F32) for s in splits],
        compiler_params=_params(),
        name="norm_proj",
    )(x, _arr(g), _arr(w))


def _rope_tables(pos):
    half = HEAD_DIM // 2
    inv = ROPE_THETA ** (-jnp.arange(half, dtype=F32) / half)
    ang = pos.astype(F32)[:, None] * inv[None, :]
    cos, sin = jnp.cos(ang), jnp.sin(ang)
    cos2 = jnp.concatenate([cos, cos], axis=-1)
    sin2 = jnp.concatenate([-sin, sin], axis=-1)
    return jnp.tile(cos2, (1, 2)), jnp.tile(sin2, (1, 2))


def _rope(x, cos, sin):
    width = x.shape[1]
    half = HEAD_DIM // 2
    lane = lax.broadcasted_iota(jnp.int32, x.shape, 1)
    upper = pltpu.roll(x, width - half, 1)
    lower = pltpu.roll(x, half, 1)
    rot = jnp.where((lane % HEAD_DIM) < half, upper, lower)
    return x * cos + rot * sin


def _tile_lanes(t, reps):
    return jnp.concatenate([t] * reps, axis=1)


def _attn_prompt_kernel(cur_ref, prev_ref, cos_ref, sin_ref, cosp_ref, sinp_ref, sink_ref,
                        o_ref, k_out_ref, v_out_ref):
    w = prev_ref.shape[0]
    n_blk = cur_ref.shape[0] // w
    cur = cur_ref[...]
    prev = prev_ref[...]
    cos, sin = cos_ref[...], sin_ref[...]
    q = (_rope(cur[:, :A_Q], _tile_lanes(cos, A_HEADS // 2), _tile_lanes(sin, A_HEADS // 2))
         * (HEAD_DIM ** -0.5)).astype(BF16)
    k = _rope(cur[:, A_Q:A_Q + A_KV], cos, sin)
    v = cur[:, A_Q + A_KV:A_PROJ]
    kp = _rope(prev[:, A_Q:A_Q + A_KV], cosp_ref[...], sinp_ref[...])
    vp = prev[:, A_Q + A_KV:A_PROJ]
    k_out_ref[0] = k[(n_blk - 1) * w:, :]
    v_out_ref[0] = v[(n_blk - 1) * w:, :]

    qi = lax.broadcasted_iota(jnp.int32, (w, 2 * w), 0)
    kj = lax.broadcasted_iota(jnp.int32, (w, 2 * w), 1)
    diff = w + qi - kj
    band = (diff >= 0) & (diff <= WINDOW)
    first_band = band & ((pl.program_id(1) > 0) | (kj >= w))
    kk_all = jnp.concatenate([kp, k], axis=0).astype(BF16)
    vv_all = jnp.concatenate([vp, v], axis=0).astype(BF16)

    items = [(j, h) for j in range(n_blk) for h in range(A_HEADS)]
    kv_lanes = lambda h: slice((h // A_GROUP) * HEAD_DIM, (h // A_GROUP + 1) * HEAD_DIM)
    s = [lax.dot_general(q[j * w:(j + 1) * w, h * HEAD_DIM:(h + 1) * HEAD_DIM],
                         kk_all[j * w:(j + 2) * w, kv_lanes(h)], NT_DIMS, preferred_element_type=F32)
         for j, h in items]
    s = [jnp.where(first_band if j == 0 else band, x, NEG) for (j, h), x in zip(items, s)]
    sink = [sink_ref[:, h:h + 1] for j, h in items]
    m = [jnp.maximum(jnp.max(x, axis=-1, keepdims=True), sk) for x, sk in zip(s, sink)]
    p = [jnp.exp(x - mx) for x, mx in zip(s, m)]
    den = [jnp.sum(x, axis=-1, keepdims=True) + jnp.exp(sk - mx) for x, sk, mx in zip(p, sink, m)]
    o = [jnp.dot(x.astype(BF16), vv_all[j * w:(j + 2) * w, kv_lanes(h)], preferred_element_type=F32)
         for (j, h), x in zip(items, p)]
    o = [x / dn for x, dn in zip(o, den)]
    o_ref[...] = jnp.concatenate(
        [jnp.concatenate(o[j * A_HEADS:(j + 1) * A_HEADS], axis=1) for j in range(n_blk)], axis=0).astype(o_ref.dtype)


def _attn_prompt(za, cos, sin, sinks, n_seq, t):
    w = min(WINDOW, t)
    n_blk = ATTN_BLOCKS_PER_STEP if t % (w * ATTN_BLOCKS_PER_STEP) == 0 else 1
    tm = n_blk * w
    nt = t // tm
    cur_map = lambda b, i: (b * nt + i, 0)
    prev_map = lambda b, i: ((b * nt + i) * n_blk - jnp.minimum(i, 1), 0)
    tab_cur = lambda b, i: (i, 0)
    tab_prev = lambda b, i: (i * n_blk - jnp.minimum(i, 1), 0)
    return pl.pallas_call(
        _attn_prompt_kernel,
        grid=(n_seq, nt),
        in_specs=[pl.BlockSpec((tm, A_PROJ), cur_map), pl.BlockSpec((w, A_PROJ), prev_map),
                  pl.BlockSpec((tm, A_KV), tab_cur), pl.BlockSpec((tm, A_KV), tab_cur),
                  pl.BlockSpec((w, A_KV), tab_prev), pl.BlockSpec((w, A_KV), tab_prev),
                  _spec(sinks)],
        out_specs=[pl.BlockSpec((tm, A_Q), cur_map),
                   pl.BlockSpec((1, w, A_KV), lambda b, i: (b, 0, 0)),
                   pl.BlockSpec((1, w, A_KV), lambda b, i: (b, 0, 0))],
        out_shape=[jax.ShapeDtypeStruct((n_seq * t, A_Q), BF16),
                   jax.ShapeDtypeStruct((n_seq, w, A_KV), F32),
                   jax.ShapeDtypeStruct((n_seq, w, A_KV), F32)],
        compiler_params=_params(2),
        name="attn_prompt",
    )(za, za, cos, sin, cos, sin, _arr(sinks))


def _attn_sample_kernel(za_ref, ck_ref, cv_ref, cos_ref, sin_ref, sink_ref, o_ref, kn_ref, vn_ref):
    sb, wb = ck_ref.shape[0], ck_ref.shape[2]
    za = za_ref[...]
    cos, sin = cos_ref[...], sin_ref[...]
    q = _rope(za[:, :A_Q], _tile_lanes(cos, A_HEADS // 2), _tile_lanes(sin, A_HEADS // 2))
    k = _rope(za[:, A_Q:A_Q + A_KV], cos, sin)
    v = za[:, A_Q + A_KV:A_PROJ]
    kn_ref[...] = k
    vn_ref[...] = v
    rows = A_GROUP * sb
    row_seq = lax.broadcasted_iota(jnp.int32, (rows, sb * wb), 0) % sb
    col = lax.broadcasted_iota(jnp.int32, (rows, sb * wb), 1)
    own = (col >= row_seq * wb) & (col < (row_seq + 1) * wb)
    kvs = range(A_KV_HEADS)
    sl = [slice(kv * HEAD_DIM, (kv + 1) * HEAD_DIM) for kv in kvs]
    heads = [[kv * A_GROUP + g for g in range(A_GROUP)] for kv in kvs]
    kt = [jnp.concatenate([ck_ref[b, sl[kv], :] for b in range(sb)], axis=1).astype(BF16) for kv in kvs]
    vt = [jnp.concatenate([cv_ref[b, sl[kv], :] for b in range(sb)], axis=1).astype(BF16) for kv in kvs]
    qs =[jnp.concatenate([q[:, h * HEAD_DIM:(h + 1) * HEAD_DIM] for h in heads[kv]], axis=0) * (HEAD_DIM ** -0.5)
          for kv in kvs]
    sink = [jnp.concatenate([jnp.broadcast_to(sink_ref[:, h:h + 1], (sb, 1)) for h in heads[kv]], axis=0)
            for kv in kvs]
    s = [jnp.dot(qs[kv].astype(BF16), kt[kv], preferred_element_type=F32) for kv in kvs]
    s = [jnp.where(own, x, NEG) for x in s]
    k_rep = [jnp.concatenate([k[:, sl[kv]]] * A_GROUP, axis=0) for kv in kvs]
    v_rep = [jnp.concatenate([v[:, sl[kv]]] * A_GROUP, axis=0) for kv in kvs]
    s_new = [jnp.sum(qs[kv] * k_rep[kv], axis=-1, keepdims=True) for kv in kvs]
    m = [jnp.maximum(jnp.maximum(jnp.max(s[kv], axis=-1, keepdims=True), s_new[kv]), sink[kv]) for kv in kvs]
    p = [jnp.exp(s[kv] - m[kv]) for kv in kvs]
    p_new = [jnp.exp(s_new[kv] - m[kv]) for kv in kvs]
    den = [jnp.sum(p[kv], axis=-1, keepdims=True) + p_new[kv] + jnp.exp(sink[kv] - m[kv]) for kv in kvs]
    o = [lax.dot_general(p[kv].astype(BF16), vt[kv], NT_DIMS, preferred_element_type=F32) for kv in kvs]
    o = [(o[kv] + p_new[kv] * v_rep[kv]) / den[kv] for kv in kvs]
    o_ref[...] = jnp.concatenate([o[kv][g * sb:(g + 1) * sb] for kv in kvs for g in range(A_GROUP)],
                                 axis=1).astype(o_ref.dtype)


def _attn_sample(za, cache_kt, cache_vt, layer, cos, sin, sinks):
    n, wb = cache_kt.shape[1], cache_kt.shape[3]
    sb = min(16, n)
    cache_spec = pl.BlockSpec((None, sb, A_KV, wb), lambda i: (layer, i, 0, 0))
    return pl.pallas_call(
        _attn_sample_kernel,
        grid=(n // sb,),
        in_specs=[pl.BlockSpec((sb, A_PROJ), lambda i: (i, 0)), cache_spec, cache_spec,
                  _const_spec((1, A_KV)), _const_spec((1, A_KV)), _spec(sinks)],
        out_specs=[pl.BlockSpec((sb, A_Q), lambda i: (i, 0)),
                   pl.BlockSpec((sb, A_KV), lambda i: (i, 0)),
                   pl.BlockSpec((sb, A_KV), lambda i: (i, 0))],
        out_shape=[jax.ShapeDtypeStruct((n, A_Q), BF16),
                   jax.ShapeDtypeStruct((n, A_KV), F32),
                   jax.ShapeDtypeStruct((n, A_KV), F32)],
        compiler_params=_params(),
        name="attn_sample",
    )(za, cache_kt, cache_vt, cos, sin, _arr(sinks))


RWKV_PARAM_NAMES = ("mu", "w0", "w2", "a0", "a2", "g2", "k_k", "k_a", "r_k", "gn_g", "gn_b", "seg")


def _rwkv_param_specs(p):
    return [_spec(p[name]) for name in RWKV_PARAM_NAMES]


def _rwkv_param_args(p):
    return [_arr(p[name]) for name in RWKV_PARAM_NAMES]


def _rwkv_prep(zb, prev, p):
    o1, o2, o3 = B_WIDTH, 2 * B_WIDTH, 3 * B_WIDTH
    o4, o5 = o3 + DECAY_RANK, o3 + DECAY_RANK + ICLR_RANK
    zs = zb + (prev - zb) * p["mu"]
    r, k, v = zs[:, :o1], zs[:, o1:o2], zs[:, o2:o3]
    wd, ad, gd = zs[:, o3:o4], zs[:, o4:o5], zs[:, o5:]
    lw = -DECAY_SCALE * jax.nn.sigmoid(p["w0"] + _bdot(jnp.tanh(wd), p["w2"]))
    a = jax.nn.sigmoid(p["a0"] + _bdot(ad, p["a2"]))
    g = _bdot(jax.nn.sigmoid(gd), p["g2"])
    kk = k * p["k_k"]
    kk = kk / jnp.maximum(jnp.sqrt(_segsum(kk * kk, p["seg"])), 1e-12)
    k = k * (1.0 + (a - 1.0) * p["k_a"])
    return r, lw, k, v, kk, kk * a, g


def _split2(x):
    hi = x.astype(BF16)
    return hi, (x - hi.astype(F32)).astype(BF16)


def _segsum(x, seg):
    outs = []
    for g0 in range(0, x.shape[1], MXU_LANES):
        hi, lo = _split2(x[:, g0:g0 + MXU_LANES])
        outs.append(jnp.dot(hi, seg, preferred_element_type=F32) + jnp.dot(lo, seg, preferred_element_type=F32))
    return jnp.concatenate(outs, axis=1)


def _rwkv_finish(o, r, k, v, g, p):
    seg = p["seg"]
    mean = _segsum(o, seg) * (1.0 / B_HEAD_DIM)
    d = o - mean
    var = _segsum(d * d, seg) * (1.0 / B_HEAD_DIM)
    o = d * lax.rsqrt(var + GN_EPS) * p["gn_g"] + p["gn_b"]
    bonus = _segsum(r * k * p["r_k"], seg) * v
    return (o + bonus) * g


def _split3(x):
    hi = x.astype(BF16)
    r1 = x - hi.astype(F32)
    mid = r1.astype(BF16)
    return hi, mid, (r1 - mid.astype(F32)).astype(BF16)


def _mm(a, b):
    return jnp.dot(a.astype(BF16), b.astype(BF16), preferred_element_type=F32)


def _head_block_diag(x, bd_mask):
    tiled = jnp.concatenate([x.astype(BF16)] * HEADS_PER_GROUP, axis=0)
    return jnp.where(bd_mask, tiled, jnp.zeros_like(tiled))


def _head_block_diag_t(x, bd_mask):
    xt = x.astype(F32).T
    tiled = jnp.concatenate([xt.astype(BF16)] * HEADS_PER_GROUP, axis=1)
    return jnp.where(bd_mask, tiled, jnp.zeros_like(tiled))


def _rwkv_chunk_kernel(zb_ref, *refs, n_sub):
    n_p = len(RWKV_PARAM_NAMES)
    p = {name: ref[...] for name, ref in zip(RWKV_PARAM_NAMES, refs[:n_p])}
    o_ref, s_out_ref, carry_ref, state_ref = refs[n_p:]
    n_rows = zb_ref.shape[0]
    c = n_rows // n_sub
    d = B_HEAD_DIM

    @pl.when(pl.program_id(1) == 0)
    def _():
        carry_ref[...] = jnp.zeros_like(carry_ref)
        state_ref[...] = jnp.zeros_like(state_ref)

    zb = zb_ref[...]
    rows = lax.broadcasted_iota(jnp.int32, zb.shape, 0)
    prev = jnp.where(rows == 0, carry_ref[0:1, :], pltpu.roll(zb, 1, 0))
    carry_ref[0:1, :] = zb[n_rows - 1:n_rows, :]
    r, lw, k, v, kk, b, g = _rwkv_prep(zb, prev, p)

    ti = lax.broadcasted_iota(jnp.int32, (c, MXU_LANES), 0)
    si = lax.broadcasted_iota(jnp.int32, (c, MXU_LANES), 1) % d
    lane_head = lax.broadcasted_iota(jnp.int32, (c, MXU_LANES), 1) // d
    incl, strict = ti >= si, ti > si
    eye = (ti == si).astype(F32)
    bd_mask = (lax.broadcasted_iota(jnp.int32, (MXU_LANES, MXU_LANES), 0) // c
               == lax.broadcasted_iota(jnp.int32, (MXU_LANES, MXU_LANES), 1) // d)
    tri = (lax.broadcasted_iota(jnp.int32, (c, c), 0) >= lax.broadcasted_iota(jnp.int32, (c, c), 1)).astype(BF16)
    bd = functools.partial(_head_block_diag, bd_mask=bd_mask)
    bdt = functools.partial(_head_block_diag_t, bd_mask=bd_mask)

    items = []
    for sub in range(n_sub):
        rs = slice(sub * c, (sub + 1) * c)
        lw_c = lw[rs]
        gcum = sum(jnp.dot(tri, part, preferred_element_type=F32) for part in _split3(lw_c))
        g_last = gcum[c - 1:c, :]
        e_neg = jnp.exp(-gcum)
        e_hat = jnp.exp(g_last - gcum)
        kap = (kk[rs] * jnp.exp(gcum - lw_c)).astype(BF16)
        rt = r[rs] * jnp.exp(gcum)
        kt, bt = k[rs] * e_neg, b[rs] * e_neg
        khat, bhat = (k[rs] * e_hat).astype(BF16), (b[rs] * e_hat).astype(BF16)
        dec_last = jnp.exp(g_last)
        v_c = v[rs].astype(BF16)
        for g0 in range(0, B_WIDTH, MXU_LANES):
            gl = slice(g0, g0 + MXU_LANES)
            items.append(dict(g0=g0, kap=kap[:, gl], rt=rt[:, gl], kt=kt[:, gl], bt=bt[:, gl], khat=khat[:, gl],
                              bhat=bhat[:, gl], v=v_c[:, gl], dec=dec_last[:, gl]))
    for it in items:
        lhs = jnp.concatenate([it["kap"], it["rt"].astype(BF16)], axis=0)
        it["m_k"] = _mm(lhs, bdt(it["kt"]))
        it["m_b"] = _mm(lhs, bdt(it["bt"]))
    for it in items:
        it["a_k"] = jnp.where(strict, it["m_k"][:c], 0.0)
        it["p_k"] = jnp.where(incl, it["m_k"][c:], 0.0)
        it["pw"] = -jnp.where(strict, it["m_b"][:c], 0.0)
        it["y"] = jnp.where(incl, it["m_b"][c:], 0.0)
        it["t_inv"] = None
    span = 1
    while span < c:
        last = 2 * span >= c
        for it in items:
            rows = ([] if last else [it["pw"]]) + ([] if it["t_inv"] is None else [it["t_inv"]]) + [it["y"]]
            out = _mm(jnp.concatenate(rows, axis=0), bd(it["pw"]))
            off = 0 if last else c
            if it["t_inv"] is None:
                it["t_inv"] = eye + it["pw"]
            else:
                it["t_inv"] = it["t_inv"] + out[off:off + c]
                off += c
            it["y"] = it["y"] + out[off:off + c]
            if not last:
                it["pw"] = out[:c]
        span *= 2
    for it in items:
        out = _mm(jnp.concatenate([it["a_k"], it["p_k"]], axis=0), bd(it["v"]))
        it["akv"], it["pkv"] = out[:c], out[c:]
    for it in items:
        ty = jnp.concatenate([it["t_inv"], it["y"]], axis=0).astype(BF16)
        out_k = _mm(ty, bd(it["kap"]))
        out_u = _mm(ty, bd(it["akv"]))
        it["k_p"], it["u0"] = out_k[:c], out_u[:c]
        it["r_p"] = (it["rt"] - out_k[c:]).astype(BF16)
        it["o0"] = it["pkv"] - out_u[c:]
    for it in items:
        it["w_bd"] = jnp.where(bd_mask, lax.dot_general(it["k_p"].astype(BF16), it["bhat"], TN_DIMS,
                                                        preferred_element_type=F32), 0.0).astype(BF16)
        s1_full = lax.dot_general(jnp.concatenate([it["v"], it["u0"].astype(BF16)], axis=0),
                                  jnp.concatenate([it["khat"], -it["bhat"]], axis=0),
                                  TN_DIMS, preferred_element_type=F32)
        it["s1"] = sum(jnp.where(lane_head == h, s1_full[h * d:(h + 1) * d], 0.0) for h in range(HEADS_PER_GROUP))
    state = {g0: state_ref[:, g0:g0 + MXU_LANES] for g0 in range(0, B_WIDTH, MXU_LANES)}
    o_parts = []
    for it in items:
        s0 = state[it["g0"]]
        o_parts.append(_mm(it["r_p"], bdt(s0)) + it["o0"])
        state[it["g0"]] = s0 * it["dec"] - _mm(s0, it["w_bd"]) + it["s1"]
    for g0, s_new in state.items():
        state_ref[:, g0:g0 + MXU_LANES] = s_new
    n_groups = B_WIDTH // MXU_LANES
    o = jnp.concatenate([jnp.concatenate(o_parts[i:i + n_groups], axis=1)
                         for i in range(0, len(o_parts), n_groups)], axis=0)
    o_ref[...] = _rwkv_finish(o, r, k, v, g, p).astype(o_ref.dtype)
    for h in range(B_HEADS):
        s_out_ref[0, h] = state_ref[:, h * d:(h + 1) * d]


def _rwkv_prompt(zb, p, n_seq, t):
    c = WKV_CHUNK
    assert c == B_HEAD_DIM and t % c == 0, "head packing puts WKV_CHUNK time steps where a head's lanes go"
    n_sub = WKV_CHUNKS_PER_STEP if t % (c * WKV_CHUNKS_PER_STEP) == 0 else 1
    tm = c * n_sub
    nc = t // tm
    width = zb.shape[1]
    return pl.pallas_call(
        functools.partial(_rwkv_chunk_kernel, n_sub=n_sub),
        grid=(n_seq, nc),
        in_specs=[pl.BlockSpec((tm, width), lambda b, i: (b * nc + i, 0))] + _rwkv_param_specs(p),
        out_specs=[pl.BlockSpec((tm, B_WIDTH), lambda b, i: (b * nc + i, 0)),
                   pl.BlockSpec((1, B_HEADS, B_HEAD_DIM, B_HEAD_DIM), lambda b, i: (b, 0, 0, 0))],
        out_shape=[jax.ShapeDtypeStruct((n_seq * t, B_WIDTH), BF16),
                   jax.ShapeDtypeStruct((n_seq, B_HEADS, B_HEAD_DIM, B_HEAD_DIM), F32)],
        scratch_shapes=[pltpu.VMEM((8, width), F32),
                        pltpu.VMEM((B_HEAD_DIM, B_WIDTH), F32)],
        compiler_params=_params(2),
        name="rwkv_prompt",
    )(zb, *_rwkv_param_args(p))


def _rwkv_sample_kernel(zb_ref, shift_ref, s_ref, s_all_ref, *refs, heads_per_step):
    del s_all_ref
    n_p = len(RWKV_PARAM_NAMES)
    p_refs = refs[:n_p]
    o_ref, s_out_ref, vec_ref, keep_ref, ot_ref = refs[n_p:]
    d = B_HEAD_DIM
    step = pl.program_id(0)

    @pl.when(step == 0)
    def _():
        p = {name: ref[...] for name, ref in zip(RWKV_PARAM_NAMES, p_refs)}
        r, lw, k, v, kk, b, g = _rwkv_prep(zb_ref[...], shift_ref[...], p)
        for idx, val in enumerate((r, jnp.exp(lw), k, v, kk, b)):
            vec_ref[idx] = val.T
        for idx, val in enumerate((r, k, v, g)):
            keep_ref[idx] = val

    for hh in range(heads_per_step):
        base = pl.multiple_of((step * heads_per_step + hh) * d, d)
        r_h, w_h, k_h, kk_h, b_h = (vec_ref[idx, pl.ds(base, d), :] for idx in (0, 1, 2, 4, 5))

        def value_row(i, carry, hh=hh, base=base, r_h=r_h, w_h=w_h, k_h=k_h, kk_h=kk_h, b_h=b_h):
            rows = pl.ds(pl.multiple_of(hh * d * d + i * d, d), d)
            s_i = s_ref[rows, :]
            sa = jnp.sum(s_i * kk_h, axis=0, keepdims=True)
            s_i = s_i * w_h - sa * b_h + vec_ref[3, pl.ds(base + i, 1), :] * k_h
            s_out_ref[rows, :] = s_i
            ot_ref[pl.ds(base + i, 1), :] = jnp.sum(s_i * r_h, axis=0, keepdims=True)
            return carry

        lax.fori_loop(0, d, value_row, 0, unroll=4)

    @pl.when(step == pl.num_programs(0) - 1)
    def _():
        p = {name: ref[...] for name, ref in zip(RWKV_PARAM_NAMES, p_refs)}
        o_ref[...] = _rwkv_finish(ot_ref[...].T, keep_ref[0], keep_ref[1], keep_ref[2], keep_ref[3],
                                  p).astype(o_ref.dtype)


def _rwkv_sample(zb, shift_all, state_t, s_all, layer, p):
    n, width = zb.shape
    d = B_HEAD_DIM
    hps = WKV_SAMPLE_HEADS_PER_STEP
    state_spec = pl.BlockSpec((None, hps * d * d, n), lambda i: (layer, i, 0))
    return pl.pallas_call(
        functools.partial(_rwkv_sample_kernel, heads_per_step=hps),
        grid=(B_HEADS // hps,),
        in_specs=[_const_spec((n, width)), pl.BlockSpec((None, n, width), lambda i: (layer, 0, 0)),
                  state_spec, pl.BlockSpec(memory_space=pl.ANY)] + _rwkv_param_specs(p),
        out_specs=[_const_spec((n, B_WIDTH)), state_spec],
        out_shape=[jax.ShapeDtypeStruct((n, B_WIDTH), BF16), jax.ShapeDtypeStruct(s_all.shape, F32)],
        input_output_aliases={3: 1},
        scratch_shapes=[pltpu.VMEM((6, B_WIDTH, n), F32), pltpu.VMEM((4, n, B_WIDTH), F32),
                        pltpu.VMEM((B_WIDTH, n), F32)],
        compiler_params=_params(),
        name="rwkv_sample",
    )(zb, shift_all, state_t, s_all, *_rwkv_param_args(p))


def _gelu_erf(x):
    return 0.5 * x * (1.0 + lax.erf(x * 0.7071067811865476))


def _gelu_tanh(x):
    c = 0.7978845608028654
    return x * (0.5 + 0.5 * jnp.tanh(x * (c + (0.044715 * c) * (x * x))))


def _gmlp_kernel(h_ref, gpre_ref, win_ref, lng_ref, lnb_ref, ws_ref, bs_ref, wout_ref, gpost_ref,
                 o_ref, *v_refs, chunk):
    x = h_ref[...]
    tm = x.shape[0]
    width = wout_ref.shape[0]
    hd = width // C_HEADS
    blk = GMLP_COL_BLOCK
    n_blk = width // blk
    xn = _rmsnorm(x, gpre_ref[...]).astype(BF16)

    def proj(c0):
        return jnp.dot(xn, win_ref[:, c0:c0 + blk], preferred_element_type=F32)

    nxt = proj(width)
    v_blocks = []
    for i in range(n_blk):
        cur = nxt
        nxt = proj(width + (i + 1) * blk) if i + 1 < n_blk else proj(0)
        v_blocks.append(_gelu_erf(cur))
    mu = sum(jnp.sum(vb, axis=-1, keepdims=True) for vb in v_blocks) * (1.0 / width)
    var = sum(jnp.sum((vb - mu) * (vb - mu), axis=-1, keepdims=True) for vb in v_blocks) * (1.0 / width)
    rstd = lax.rsqrt(var + LN_EPS)
    lng, lnb = lng_ref[...], lnb_ref[...]
    if chunk > 1:
        tri = (lax.broadcasted_iota(jnp.int32, (chunk, chunk), 0)
               >= lax.broadcasted_iota(jnp.int32, (chunk, chunk), 1))
        bias = bs_ref[...]
    out = None
    for i in range(n_blk):
        c0 = i * blk
        u_pre = nxt
        if i + 1 < n_blk:
            nxt = proj(c0 + blk)
        vn = (v_blocks[i] - mu) * rstd * lng[:, c0:c0 + blk] + lnb[:, c0:c0 + blk]
        if chunk == 1:
            mixed = vn * ws_ref[:, c0:c0 + blk] + bs_ref[:, c0:c0 + blk]
            v_refs[0][:, c0:c0 + blk] = vn
        else:
            vb = vn.astype(BF16)
            cols = []
            for hh in range(blk // hd):
                head = c0 // hd + hh
                wtri = jnp.where(tri, ws_ref[head], 0.0).astype(BF16)
                parts = [jnp.dot(wtri, vb[r0:r0 + chunk, hh * hd:(hh + 1) * hd], preferred_element_type=F32)
                         + bias[:, head:head + 1] for r0 in range(0, tm, chunk)]
                cols.append(jnp.concatenate(parts, axis=0) if len(parts) > 1 else parts[0])
            mixed = jnp.concatenate(cols, axis=1)
        y = (_gelu_erf(u_pre) * mixed).astype(BF16)
        part = jnp.dot(y, wout_ref[c0:c0 + blk, :], preferred_element_type=F32)
        out = part if out is None else out + part
    o_ref[...] = x + _rmsnorm(out, gpost_ref[...])


def _gmlp(h, gpre, win, lng, lnb, ws, bs, wout, gpost, chunk, tm):
    rows, d = h.shape
    tm = min(tm, rows)
    width = wout.shape[0]
    out_specs = [pl.BlockSpec((tm, d), lambda i: (i, 0))]
    out_shape = [jax.ShapeDtypeStruct((rows, d), F32)]
    if chunk == 1:
        out_specs.append(pl.BlockSpec((tm, width), lambda i: (i, 0)))
        out_shape.append(jax.ShapeDtypeStruct((rows, width), F32))
    return pl.pallas_call(
        functools.partial(_gmlp_kernel, chunk=chunk),
        grid=(rows // tm,),
        in_specs=[pl.BlockSpec((tm, d), lambda i: (i, 0))]
        + [_spec(x) for x in (gpre, win, lng, lnb, ws, bs, wout, gpost)],
        out_specs=out_specs,
        out_shape=out_shape,
        compiler_params=_params(),
        name="gmlp",
    )(h, *[_arr(x) for x in (gpre, win, lng, lnb, ws, bs, wout, gpost)])


def _ffn_kernel(h_ref, *refs, carried, tiles_per_seq, mixed):
    x = h_ref[...]
    if mixed:
        a_ref, b_ref, wmix_ref, gmix_ref = refs[:4]
        refs = refs[4:]
        ka = a_ref.shape[1]
        acc = jnp.dot(a_ref[...], wmix_ref[:ka, :], preferred_element_type=F32)
        acc = acc + jnp.dot(b_ref[...], wmix_ref[ka:, :], preferred_element_type=F32)
        x = x + _rmsnorm(acc, gmix_ref[...])
    gpre_ref, wg_ref, wu_ref, cw_ref, cb_ref, wd_ref, gpost_ref = refs[:7]
    refs = refs[7:]
    tm = x.shape[0]
    f = wg_ref.shape[1]
    xn = _rmsnorm(x, gpre_ref[...]).astype(BF16)
    if carried:
        o_ref, tail_ref, carry_ref = refs

        @pl.when(pl.program_id(0) % tiles_per_seq == 0)
        def _():
            carry_ref[...] = jnp.zeros_like(carry_ref)
    else:
        s2_ref, s1_ref, o_ref, gp_ref = refs
    cw = cw_ref[...]
    cb = cb_ref[...]

    def in_proj(c0, c1):
        return (jnp.dot(xn, wg_ref[:, c0:c1], preferred_element_type=F32),
                jnp.dot(xn, wu_ref[:, c0:c1], preferred_element_type=F32))

    bounds = [(c0, min(c0 + FFN_COL_BLOCK, f)) for c0 in range(0, f, FFN_COL_BLOCK)]
    nxt = in_proj(*bounds[0])
    out = None
    for idx, (c0, c1) in enumerate(bounds):
        gp, up = nxt
        if idx + 1 < len(bounds):
            nxt = in_proj(*bounds[idx + 1])
        if carried:
            rows = lax.broadcasted_iota(jnp.int32, gp.shape, 0)
            t2, t1 = carry_ref[0:1, c0:c1], carry_ref[1:2, c0:c1]
            x1 = jnp.where(rows == 0, t1, pltpu.roll(gp, 1, 0))
            x2 = jnp.where(rows == 0, t2, jnp.where(rows == 1, t1, pltpu.roll(gp, 2, 0)))
            carry_ref[0:2, c0:c1] = gp[tm - 2:tm, :]
            tail_ref[0, :, c0:c1] = gp[tm - 2:tm, :]
        else:
            x2, x1 = s2_ref[:, c0:c1], s1_ref[:, c0:c1]
            gp_ref[:, c0:c1] = gp
        conv = cb[:, c0:c1] + cw[2:3, c0:c1] * gp + cw[0:1, c0:c1] * x2 + cw[1:2, c0:c1] * x1
        hid = (_gelu_tanh(conv) * up).astype(BF16)
        part = jnp.dot(hid, wd_ref[c0:c1, :], preferred_element_type=F32)
        out = part if out is None else out + part
    o_ref[...] = x + _rmsnorm(out, gpost_ref[...])


def _ffn(h, gpre, wg, wu, cw, cb, wd, gpost, tm, seq_len=None, state=None, mix=None):
    rows, d = h.shape
    f = wg.shape[1]
    carried = state is None
    tm = min(tm, rows, seq_len) if carried else min(tm, rows)
    row_spec = pl.BlockSpec((tm, d), lambda i: (i, 0))
    params = (gpre, wg, wu, cw, cb, wd, gpost)
    in_specs, args = [row_spec], [h]
    if mix is not None:
        a, b, w_mix, g_mix = mix
        in_specs += [pl.BlockSpec((tm, a.shape[1]), lambda i: (i, 0)), pl.BlockSpec((tm, b.shape[1]), lambda i: (i, 0)),
                     _spec(w_mix), _spec(g_mix)]
        args += [a, b, _arr(w_mix), _arr(g_mix)]
    in_specs += [_spec(x) for x in params]
    args += [_arr(x) for x in params]
    if carried:
        tiles_per_seq = seq_len // tm
        out_specs = [row_spec, pl.BlockSpec((1, 2, f), lambda i: (i // tiles_per_seq, 0, 0))]
        out_shape = [jax.ShapeDtypeStruct((rows, d), F32),
                     jax.ShapeDtypeStruct((rows // seq_len, 2, f), F32)]
        scratch = [pltpu.VMEM((8, f), F32)]
    else:
        tiles_per_seq = 1
        in_specs += [pl.BlockSpec((None, None, tm, f), lambda i, tap=tap: (state.layer, tap, i, 0)) for tap in (0, 1)]
        args += [state.arr, state.arr]
        out_specs = [row_spec, pl.BlockSpec((tm, f), lambda i: (i, 0))]
        out_shape = [jax.ShapeDtypeStruct((rows, d), F32), jax.ShapeDtypeStruct((rows, f), F32)]
        scratch = []
    return pl.pallas_call(
        functools.partial(_ffn_kernel, carried=carried, tiles_per_seq=tiles_per_seq, mixed=mix is not None),
        grid=(rows // tm,),
        in_specs=in_specs,
        out_specs=out_specs,
        out_shape=out_shape,
        scratch_shapes=scratch,
        compiler_params=_params(),
        name="conv_ffn",
    )(*args)


def kernel(x_prompt, x_sample, cache_win_k, cache_win_v, state_wkv, state_shift, state_ffn_conv,
           norm_mix_pre, norm_mix_post, norm_ffn_pre, norm_ffn_post,
           w_in_even, attn_sinks, shift_mu, decay_w0, decay_w2, iclr_a0, iclr_a2, gate_g2,
           key_k, key_a, bonus_r_k, gn_gain, gn_bias, w_out_even,
           w_in_odd, sgu_ln_gain, sgu_ln_bias, sgu_w, sgu_b, w_out_odd,
           ffn_w_gate, ffn_w_up, ffn_conv_w, ffn_conv_b, ffn_w_down):
    n_seq, t, d = x_prompt.shape
    n_dec, dec_t, _ = x_sample.shape
    assert dec_t == 1, "the sample path is written for one new token per sequence"
    depth = norm_mix_pre.shape[0]
    n_even = w_in_even.shape[0]
    wp = min(WINDOW, t)
    wb = cache_win_k.shape[2]
    hd = B_HEAD_DIM
    chunk = min(t, CHUNK)
    assert chunk == sgu_w.shape[-1], "the prompt is mixed in whole CHUNK-position chunks"

    hp = x_prompt.reshape(n_seq * t, d)
    hs = x_sample.reshape(n_dec, d)
    cos_p, sin_p = _rope_tables(jnp.arange(t, dtype=jnp.int32))
    cos_s, sin_s = _rope_tables(PAST_LEN + jnp.arange(1, dtype=jnp.int32))
    seg = jnp.kron(jnp.eye(HEADS_PER_GROUP, dtype=F32), jnp.ones((hd, hd), F32)).astype(BF16)

    rows3 = lambda a: a.reshape(a.shape[0], 1, -1)
    bf = lambda a: a.astype(BF16)
    stacked = dict(
        g_pre=rows3(norm_mix_pre), g_post=rows3(norm_mix_post), f_pre=rows3(norm_ffn_pre),
        f_post=rows3(norm_ffn_post), w_in_even=bf(w_in_even), w_out_even=bf(w_out_even),
        sinks=rows3(attn_sinks), mu=rows3(shift_mu), w0=rows3(decay_w0), w2=bf(decay_w2), a0=rows3(iclr_a0),
        a2=bf(iclr_a2), g2=bf(gate_g2), k_k=rows3(key_k), k_a=rows3(key_a), r_k=rows3(bonus_r_k),
        gn_g=rows3(gn_gain), gn_b=rows3(gn_bias), w_in_odd=bf(w_in_odd), w_out_odd=bf(w_out_odd),
        ln_g=rows3(sgu_ln_gain), ln_b=rows3(sgu_ln_bias), sgu_w=sgu_w,
        sgu_bt=jnp.swapaxes(sgu_b, 1, 2),
        sgu_w0=rows3(jnp.repeat(sgu_w[:, :, 0, 0], sgu_ln_gain.shape[1] // C_HEADS, axis=1)),
        sgu_b0=rows3(jnp.repeat(sgu_b[:, :, 0], sgu_ln_gain.shape[1] // C_HEADS, axis=1)),
        wg=bf(ffn_w_gate), wu=bf(ffn_w_up), wd=bf(ffn_w_down), cw=ffn_conv_w, cb=rows3(ffn_conv_b))
    cache_kt = jnp.transpose(cache_win_k, (0, 1, 3, 4, 2)).reshape(n_even, n_dec, A_KV, wb)
    cache_vt = jnp.transpose(cache_win_v, (0, 1, 3, 4, 2)).reshape(n_even, n_dec, A_KV, wb)
    wkv_t = jnp.transpose(state_wkv, (0, 2, 3, 4, 1)).reshape(n_even, B_HEADS * hd * hd, n_dec)
    wkv_new_t = jnp.zeros_like(wkv_t)
    conv_taps = jnp.swapaxes(state_ffn_conv, 1, 2)

    kp_l, vp_l, ks_l, vs_l, sp_l, shp_l, shs_l, vsgu_l, cp_l, cs_l = ([] for _ in range(10))
    for layer in range(depth):
        j = layer // 2
        at = lambda name, idx: _Layer(stacked[name], idx)
        g_pre, g_post = at("g_pre", layer), at("g_post", layer)
        if layer % 2 == 0:
            w_in, w_out, sinks = at("w_in_even", j), at("w_out_even", j), at("sinks", j)
            p = {name: at(name, j) for name in RWKV_PARAM_NAMES if name != "seg"}
            p["seg"] = seg
            splits = (A_PROJ, w_in.shape[1] - A_PROJ)
            za, zb = _norm_proj(hp, g_pre, w_in, splits, tm=ROW_TILE)
            oa, kp, vp = _attn_prompt(za, cos_p, sin_p, sinks, n_seq, t)
            ob, sp = _rwkv_prompt(zb, p, n_seq, t)
            mix_p = (oa, ob, w_out, g_post)
            kp_l.append(kp.reshape(n_seq, wp, A_KV_HEADS, HEAD_DIM))
            vp_l.append(vp.reshape(n_seq, wp, A_KV_HEADS, HEAD_DIM))
            sp_l.append(sp)
            shp_l.append(zb.reshape(n_seq, t, -1)[:, -1])
            za, zb = _norm_proj(hs, g_pre, w_in, splits, tm=ROW_TILE)
            oa, kn, vn = _attn_sample(za, cache_kt, cache_vt, j, cos_s, sin_s, sinks)
            ob, wkv_new_t = _rwkv_sample(zb, state_shift, wkv_t, wkv_new_t, j, p)
            mix_s = (oa, ob, w_out, g_post)
            ks_l.append(kn.reshape(n_dec, 1, A_KV_HEADS, HEAD_DIM))
            vs_l.append(vn.reshape(n_dec, 1, A_KV_HEADS, HEAD_DIM))
            shs_l.append(zb)
        else:
            w_in, w_out = at("w_in_odd", j), at("w_out_odd", j)
            lng, lnb = at("ln_g", j), at("ln_b", j)
            hp, = _gmlp(hp, g_pre, w_in, lng, lnb, at("sgu_w", j), at("sgu_bt", j), w_out, g_post,
                        chunk=chunk, tm=GMLP_CHUNKS_PER_STEP * chunk)
            hs, vsg = _gmlp(hs, g_pre, w_in, lng, lnb, at("sgu_w0", j), at("sgu_b0", j), w_out, g_post,
                            chunk=1, tm=ROW_TILE)
            vsgu_l.append(vsg.reshape(n_dec, 1, -1))
            mix_p = mix_s = None
        ffn_params = [at(name, layer) for name in ("f_pre", "wg", "wu", "cw", "cb", "wd", "f_post")]
        hp, cp = _ffn(hp, *ffn_params, tm=FFN_ROW_TILE, seq_len=t, mix=mix_p)
        hs, gp = _ffn(hs, *ffn_params, tm=FFN_ROW_TILE, state=_Layer(conv_taps, layer), mix=mix_s)
        cp_l.append(cp)
        cs_l.append(jnp.stack([state_ffn_conv[layer][:, 1], gp], axis=1))

    wkv_sample = jnp.transpose(wkv_new_t.reshape(n_even, B_HEADS, hd, hd, n_dec), (0, 4, 1, 2, 3))
    return (hp.reshape(n_seq, t, d), hs.reshape(n_dec, 1, d),
            jnp.stack(kp_l), jnp.stack(vp_l), jnp.stack(ks_l), jnp.stack(vs_l),
            jnp.stack(sp_l), wkv_sample, jnp.stack(shp_l), jnp.stack(shs_l),
            jnp.stack(vsgu_l), jnp.stack(cp_l), jnp.stack(cs_l))
```

```python
import functools

import jax
import jax.numpy as jnp
from jax import lax
from jax.experimental import pallas as pl
from jax.experimental.pallas import tpu as pltpu

F32 = jnp.float32
BF16 = jnp.bfloat16

PAST_LEN = 8192
HEAD_DIM = 64
A_HEADS = 8
A_KV_HEADS = 2
A_GROUP = A_HEADS // A_KV_HEADS
WINDOW = 128
ROPE_THETA = 10000.0
B_HEAD_DIM = 64
B_HEADS = 8
B_WIDTH = B_HEADS * B_HEAD_DIM
DECAY_RANK = 64
ICLR_RANK = 64
DECAY_SCALE = 0.606531
GN_EPS = 64e-5
CHUNK = 128
C_HEADS = 8
RMS_EPS = 1e-6
LN_EPS = 1e-5
NEG = -1e30
A_Q = A_HEADS * HEAD_DIM
A_KV = A_KV_HEADS * HEAD_DIM
A_PROJ = A_Q + 2 * A_KV

WKV_CHUNK = 64
WKV_CHUNKS_PER_STEP = 4
ATTN_BLOCKS_PER_STEP = 4
WKV_SAMPLE_HEADS_PER_STEP = 2
FFN_COL_BLOCK = 256
ROW_TILE = 512
FFN_ROW_TILE = 256
GMLP_CHUNKS_PER_STEP = 2
MXU_LANES = 256
HEADS_PER_GROUP = MXU_LANES // B_HEAD_DIM
V7X_VMEM_LIMIT_BYTES = 56 * 1024 * 1024

NT_DIMS = (((1,), (1,)), ((), ()))
TN_DIMS = (((0,), (0,)), ((), ()))


def _params(n_axes=1):
    return pltpu.CompilerParams(dimension_semantics=("arbitrary",) * n_axes,
                                vmem_limit_bytes=V7X_VMEM_LIMIT_BYTES)


def _rmsnorm(x, g):
    return x * lax.rsqrt(jnp.mean(x * x, axis=-1, keepdims=True) + RMS_EPS) * g


def _bdot(a, b):
    return jnp.dot(a.astype(BF16), b.astype(BF16), preferred_element_type=F32)


def _const_spec(shape):
    return pl.BlockSpec(shape, lambda *_: (0,) * len(shape))


class _Layer:
    def __init__(self, arr, layer):
        self.arr, self.layer = arr, layer

    @property
    def shape(self):
        return self.arr.shape[1:]


def _spec(x):
    if isinstance(x, _Layer):
        idx = (x.layer,) + (0,) * len(x.shape)
        return pl.BlockSpec((None,) + x.shape, lambda *_: idx)
    return _const_spec(x.shape)


def _arr(x):
    return x.arr if isinstance(x, _Layer) else x


def _norm_proj_kernel(x_ref, g_ref, w_ref, *out_refs, splits):
    xn = _rmsnorm(x_ref[...], g_ref[...])
    z = jnp.dot(xn.astype(BF16), w_ref[...], preferred_element_type=F32)
    off = 0
    for o_ref, n in zip(out_refs, splits):
        o_ref[...] = z[:, off:off + n]
        off += n


def _norm_proj(x, g, w, splits, tm):
    rows, d = x.shape
    n = w.shape[1]
    tm = min(tm, rows)
    return pl.pallas_call(
        functools.partial(_norm_proj_kernel, splits=splits),
        grid=(rows // tm,),
        in_specs=[pl.BlockSpec((tm, d), lambda i: (i, 0)), _spec(g), _spec(w)],
        out_specs=[pl.BlockSpec((tm, s), lambda i: (i, 0)) for s in splits],
        out_shape=[jax.ShapeDtypeStruct((rows, s), F32) for s in splits],
        compiler_params=_params(),
        name="norm_proj",
    )(x, _arr(g), _arr(w))


def _rope_tables(pos):
    half = HEAD_DIM // 2
    inv = ROPE_THETA ** (-jnp.arange(half, dtype=F32) / half)
    ang = pos.astype(F32)[:, None] * inv[None, :]
    cos, sin = jnp.cos(ang), jnp.sin(ang)
    cos2 = jnp.concatenate([cos, cos], axis=-1)
    sin2 = jnp.concatenate([-sin, sin], axis=-1)
    return jnp.tile(cos2, (1, 2)), jnp.tile(sin2, (1, 2))


def _rope(x, cos, sin):
    width = x.shape[1]
    half = HEAD_DIM // 2
    lane = lax.broadcasted_iota(jnp.int32, x.shape, 1)
    upper = pltpu.roll(x, width - half, 1)
    lower = pltpu.roll(x, half, 1)
    rot = jnp.where((lane % HEAD_DIM) < half, upper, lower)
    return x * cos + rot * sin


def _tile_lanes(t, reps):
    return jnp.concatenate([t] * reps, axis=1)


def _attn_prompt_kernel(cur_ref, prev_ref, cos_ref, sin_ref, cosp_ref, sinp_ref, sink_ref,
                        o_ref, k_out_ref, v_out_ref):
    w = prev_ref.shape[0]
    n_blk = cur_ref.shape[0] // w
    cur = cur_ref[...]
    prev = prev_ref[...]
    cos, sin = cos_ref[...], sin_ref[...]
    q = (_rope(cur[:, :A_Q], _tile_lanes(cos, A_HEADS // 2), _tile_lanes(sin, A_HEADS // 2))
         * (HEAD_DIM ** -0.5)).astype(BF16)
    k = _rope(cur[:, A_Q:A_Q + A_KV], cos, sin)
    v = cur[:, A_Q + A_KV:A_PROJ]
    kp = _rope(prev[:, A_Q:A_Q + A_KV], cosp_ref[...], sinp_ref[...])
    vp = prev[:, A_Q + A_KV:A_PROJ]
    k_out_ref[0] = k[(n_blk - 1) * w:, :]
    v_out_ref[0] = v[(n_blk - 1) * w:, :]

    kk_all = jnp.concatenate([kp, k], axis=0).astype(BF16)
    vv_all = jnp.concatenate([vp, v], axis=0).astype(BF16)

    items = [(j, kv) for j in range(n_blk) for kv in range(A_KV_HEADS)]
    kv_lanes = lambda kv: slice(kv * HEAD_DIM, (kv + 1) * HEAD_DIM)
    heads = lambda kv: range(kv * A_GROUP, (kv + 1) * A_GROUP)
    kj = lax.broadcasted_iota(jnp.int32, (2 * w, A_GROUP * w), 0)
    qi = lax.broadcasted_iota(jnp.int32, (2 * w, A_GROUP * w), 1) % w
    diff = w + qi - kj
    band = (diff >= 0) & (diff <= WINDOW)
    first_band = band & ((pl.program_id(1) > 0) | (kj >= w))
    sink = [jnp.concatenate([jnp.broadcast_to(sink_ref[:, h:h + 1], (1, w)) for h in heads(kv)], axis=1)
            for j, kv in items]
    qs = [jnp.concatenate([q[j * w:(j + 1) * w, h * HEAD_DIM:(h + 1) * HEAD_DIM] for h in heads(kv)], axis=0)
          for j, kv in items]
    s = [lax.dot_general(kk_all[j * w:(j + 2) * w, kv_lanes(kv)], x, NT_DIMS, preferred_element_type=F32)
         for (j, kv), x in zip(items, qs)]
    s = [jnp.where(first_band if j == 0 else band, x, NEG) for (j, kv), x in zip(items, s)]
    m = [jnp.maximum(jnp.max(x, axis=0, keepdims=True), sk) for x, sk in zip(s, sink)]
    p = [jnp.exp(x - mx) for x, mx in zip(s, m)]
    den = [jnp.sum(x, axis=0, keepdims=True) + jnp.exp(sk - mx) for x, sk, mx in zip(p, sink, m)]
    o = [lax.dot_general(vv_all[j * w:(j + 2) * w, kv_lanes(kv)], x.astype(BF16), TN_DIMS,
                         preferred_element_type=F32) for (j, kv), x in zip(items, p)]
    o = [(x / dn).T for x, dn in zip(o, den)]
    o_ref[...] = jnp.concatenate(
        [jnp.concatenate([o[j * A_KV_HEADS + kv][g * w:(g + 1) * w] for kv in range(A_KV_HEADS)
                          for g in range(A_GROUP)], axis=1) for j in range(n_blk)], axis=0).astype(o_ref.dtype)


def _attn_prompt(za, cos, sin, sinks, n_seq, t):
    w = min(WINDOW, t)
    n_blk = ATTN_BLOCKS_PER_STEP if t % (w * ATTN_BLOCKS_PER_STEP) == 0 else 1
    tm = n_blk * w
    nt = t // tm
    cur_map = lambda b, i: (b * nt + i, 0)
    prev_map = lambda b, i: ((b * nt + i) * n_blk - jnp.minimum(i, 1), 0)
    tab_cur = lambda b, i: (i, 0)
    tab_prev = lambda b, i: (i * n_blk - jnp.minimum(i, 1), 0)
    return pl.pallas_call(
        _attn_prompt_kernel,
        grid=(n_seq, nt),
        in_specs=[pl.BlockSpec((tm, A_PROJ), cur_map), pl.BlockSpec((w, A_PROJ), prev_map),
                  pl.BlockSpec((tm, A_KV), tab_cur), pl.BlockSpec((tm, A_KV), tab_cur),
                  pl.BlockSpec((w, A_KV), tab_prev), pl.BlockSpec((w, A_KV), tab_prev),
                  _spec(sinks)],
        out_specs=[pl.BlockSpec((tm, A_Q), cur_map),
                   pl.BlockSpec((1, w, A_KV), lambda b, i: (b, 0, 0)),
                   pl.BlockSpec((1, w, A_KV), lambda b, i: (b, 0, 0))],
        out_shape=[jax.ShapeDtypeStruct((n_seq * t, A_Q), BF16),
                   jax.ShapeDtypeStruct((n_seq, w, A_KV), F32),
                   jax.ShapeDtypeStruct((n_seq, w, A_KV), F32)],
        compiler_params=_params(2),
        name="attn_prompt",
    )(za, za, cos, sin, cos, sin, _arr(sinks))


def _attn_sample_kernel(za_ref, ck_ref, cv_ref, cos_ref, sin_ref, sink_ref, o_ref, kn_ref, vn_ref):
    sb, wb = ck_ref.shape[0], ck_ref.shape[2]
    za = za_ref[...]
    cos, sin = cos_ref[...], sin_ref[...]
    q = _rope(za[:, :A_Q], _tile_lanes(cos, A_HEADS // 2), _tile_lanes(sin, A_HEADS // 2))
    k = _rope(za[:, A_Q:A_Q + A_KV], cos, sin)
    v = za[:, A_Q + A_KV:A_PROJ]
    kn_ref[...] = k
    vn_ref[...] = v
    rows = A_GROUP * sb
    row_seq = lax.broadcasted_iota(jnp.int32, (rows, sb * wb), 0) % sb
    col = lax.broadcasted_iota(jnp.int32, (rows, sb * wb), 1)
    own = (col >= row_seq * wb) & (col < (row_seq + 1) * wb)
    kvs = range(A_KV_HEADS)
    sl = [slice(kv * HEAD_DIM, (kv + 1) * HEAD_DIM) for kv in kvs]
    heads = [[kv * A_GROUP + g for g in range(A_GROUP)] for kv in kvs]
    kt = [jnp.concatenate([ck_ref[b, sl[kv], :] for b in range(sb)], axis=1).astype(BF16) for kv in kvs]
    vt = [jnp.concatenate([cv_ref[b, sl[kv], :] for b in range(sb)], axis=1).astype(BF16) for kv in kvs]
    qs =[jnp.concatenate([q[:, h * HEAD_DIM:(h + 1) * HEAD_DIM] for h in heads[kv]], axis=0) * (HEAD_DIM ** -0.5)
          for kv in kvs]
    sink = [jnp.concatenate([jnp.broadcast_to(sink_ref[:, h:h + 1], (sb, 1)) for h in heads[kv]], axis=0)
            for kv in kvs]
    s = [jnp.dot(qs[kv].astype(BF16), kt[kv], preferred_element_type=F32) for kv in kvs]
    s = [jnp.where(own, x, NEG) for x in s]
    k_rep = [jnp.concatenate([k[:, sl[kv]]] * A_GROUP, axis=0) for kv in kvs]
    v_rep = [jnp.concatenate([v[:, sl[kv]]] * A_GROUP, axis=0) for kv in kvs]
    s_new = [jnp.sum(qs[kv] * k_rep[kv], axis=-1, keepdims=True) for kv in kvs]
    m = [jnp.maximum(jnp.maximum(jnp.max(s[kv], axis=-1, keepdims=True), s_new[kv]), sink[kv]) for kv in kvs]
    p = [jnp.exp(s[kv] - m[kv]) for kv in kvs]
    p_new = [jnp.exp(s_new[kv] - m[kv]) for kv in kvs]
    den = [jnp.sum(p[kv], axis=-1, keepdims=True) + p_new[kv] + jnp.exp(sink[kv] - m[kv]) for kv in kvs]
    o = [lax.dot_general(p[kv].astype(BF16), vt[kv], NT_DIMS, preferred_element_type=F32) for kv in kvs]
    o = [(o[kv] + p_new[kv] * v_rep[kv]) / den[kv] for kv in kvs]
    o_ref[...] = jnp.concatenate([o[kv][g * sb:(g + 1) * sb] for kv in kvs for g in range(A_GROUP)],
                                 axis=1).astype(o_ref.dtype)


def _attn_sample(za, cache_kt, cache_vt, layer, cos, sin, sinks):
    n, wb = cache_kt.shape[1], cache_kt.shape[3]
    sb = min(16, n)
    cache_spec = pl.BlockSpec((None, sb, A_KV, wb), lambda i: (layer, i, 0, 0))
    return pl.pallas_call(
        _attn_sample_kernel,
        grid=(n // sb,),
        in_specs=[pl.BlockSpec((sb, A_PROJ), lambda i: (i, 0)), cache_spec, cache_spec,
                  _const_spec((1, A_KV)), _const_spec((1, A_KV)), _spec(sinks)],
        out_specs=[pl.BlockSpec((sb, A_Q), lambda i: (i, 0)),
                   pl.BlockSpec((sb, A_KV), lambda i: (i, 0)),
                   pl.BlockSpec((sb, A_KV), lambda i: (i, 0))],
        out_shape=[jax.ShapeDtypeStruct((n, A_Q), BF16),
                   jax.ShapeDtypeStruct((n, A_KV), F32),
                   jax.ShapeDtypeStruct((n, A_KV), F32)],
        compiler_params=_params(),
        name="attn_sample",
    )(za, cache_kt, cache_vt, cos, sin, _arr(sinks))


RWKV_PARAM_NAMES = ("mu", "w0", "w2", "a0", "a2", "g2", "k_k", "k_a", "r_k", "gn_g", "gn_b", "seg")


def _rwkv_param_specs(p):
    return [_spec(p[name]) for name in RWKV_PARAM_NAMES]


def _rwkv_param_args(p):
    return [_arr(p[name]) for name in RWKV_PARAM_NAMES]


def _rwkv_prep(zb, prev, p):
    o1, o2, o3 = B_WIDTH, 2 * B_WIDTH, 3 * B_WIDTH
    o4, o5 = o3 + DECAY_RANK, o3 + DECAY_RANK + ICLR_RANK
    zs = zb + (prev - zb) * p["mu"]
    r, k, v = zs[:, :o1], zs[:, o1:o2], zs[:, o2:o3]
    wd, ad, gd = zs[:, o3:o4], zs[:, o4:o5], zs[:, o5:]
    lw = -DECAY_SCALE * jax.nn.sigmoid(p["w0"] + _bdot(jnp.tanh(wd), p["w2"]))
    a = jax.nn.sigmoid(p["a0"] + _bdot(ad, p["a2"]))
    g = _bdot(jax.nn.sigmoid(gd), p["g2"])
    kk = k * p["k_k"]
    kk = kk / jnp.maximum(jnp.sqrt(_segsum(kk * kk, p["seg"])), 1e-12)
    k = k * (1.0 + (a - 1.0) * p["k_a"])
    return r, lw, k, v, kk, kk * a, g


def _split2(x):
    hi = x.astype(BF16)
    return hi, (x - hi.astype(F32)).astype(BF16)


def _segsum(x, seg):
    outs = []
    for g0 in range(0, x.shape[1], MXU_LANES):
        hi, lo = _split2(x[:, g0:g0 + MXU_LANES])
        outs.append(jnp.dot(hi, seg, preferred_element_type=F32) + jnp.dot(lo, seg, preferred_element_type=F32))
    return jnp.concatenate(outs, axis=1)


def _rwkv_finish(o, r, k, v, g, p):
    seg = p["seg"]
    mean = _segsum(o, seg) * (1.0 / B_HEAD_DIM)
    d = o - mean
    var = _segsum(d * d, seg) * (1.0 / B_HEAD_DIM)
    o = d * lax.rsqrt(var + GN_EPS) * p["gn_g"] + p["gn_b"]
    bonus = _segsum(r * k * p["r_k"], seg) * v
    return (o + bonus) * g


def _split3(x):
    hi = x.astype(BF16)
    r1 = x - hi.astype(F32)
    mid = r1.astype(BF16)
    return hi, mid, (r1 - mid.astype(F32)).astype(BF16)


def _mm(a, b):
    return jnp.dot(a.astype(BF16), b.astype(BF16), preferred_element_type=F32)


def _head_block_diag(x, bd_mask):
    tiled = jnp.concatenate([x.astype(BF16)] * HEADS_PER_GROUP, axis=0)
    return jnp.where(bd_mask, tiled, jnp.zeros_like(tiled))


def _head_block_diag_t(x, bd_mask):
    xt = x.astype(F32).T
    tiled = jnp.concatenate([xt.astype(BF16)] * HEADS_PER_GROUP, axis=1)
    return jnp.where(bd_mask, tiled, jnp.zeros_like(tiled))


def _rwkv_chunk_kernel(zb_ref, *refs, n_sub):
    n_p = len(RWKV_PARAM_NAMES)
    p = {name: ref[...] for name, ref in zip(RWKV_PARAM_NAMES, refs[:n_p])}
    o_ref, s_out_ref, carry_ref, state_ref = refs[n_p:]
    n_rows = zb_ref.shape[0]
    c = n_rows // n_sub
    d = B_HEAD_DIM

    @pl.when(pl.program_id(1) == 0)
    def _():
        carry_ref[...] = jnp.zeros_like(carry_ref)
        state_ref[...] = jnp.zeros_like(state_ref)

    zb = zb_ref[...]
    rows = lax.broadcasted_iota(jnp.int32, zb.shape, 0)
    prev = jnp.where(rows == 0, carry_ref[0:1, :], pltpu.roll(zb, 1, 0))
    carry_ref[0:1, :] = zb[n_rows - 1:n_rows, :]
    r, lw, k, v, kk, b, g = _rwkv_prep(zb, prev, p)

    ti = lax.broadcasted_iota(jnp.int32, (c, MXU_LANES), 0)
    si = lax.broadcasted_iota(jnp.int32, (c, MXU_LANES), 1) % d
    lane_head = lax.broadcasted_iota(jnp.int32, (c, MXU_LANES), 1) // d
    incl, strict = ti >= si, ti > si
    eye = (ti == si).astype(F32)
    bd_mask = (lax.broadcasted_iota(jnp.int32, (MXU_LANES, MXU_LANES), 0) // c
               == lax.broadcasted_iota(jnp.int32, (MXU_LANES, MXU_LANES), 1) // d)
    tri = (lax.broadcasted_iota(jnp.int32, (c, c), 0) >= lax.broadcasted_iota(jnp.int32, (c, c), 1)).astype(BF16)
    bd = functools.partial(_head_block_diag, bd_mask=bd_mask)
    bdt = functools.partial(_head_block_diag_t, bd_mask=bd_mask)

    items = []
    for sub in range(n_sub):
        rs = slice(sub * c, (sub + 1) * c)
        lw_c = lw[rs]
        gcum = sum(jnp.dot(tri, part, preferred_element_type=F32) for part in _split3(lw_c))
        g_last = gcum[c - 1:c, :]
        e_neg = jnp.exp(-gcum)
        e_hat = jnp.exp(g_last - gcum)
        kap = (kk[rs] * jnp.exp(gcum - lw_c)).astype(BF16)
        rt = r[rs] * jnp.exp(gcum)
        kt, bt = k[rs] * e_neg, b[rs] * e_neg
        khat, bhat = (k[rs] * e_hat).astype(BF16), (b[rs] * e_hat).astype(BF16)
        dec_last = jnp.exp(g_last)
        v_c = v[rs].astype(BF16)
        for g0 in range(0, B_WIDTH, MXU_LANES):
            gl = slice(g0, g0 + MXU_LANES)
            items.append(dict(g0=g0, kap=kap[:, gl], rt=rt[:, gl], kt=kt[:, gl], bt=bt[:, gl], khat=khat[:, gl],
                              bhat=bhat[:, gl], v=v_c[:, gl], dec=dec_last[:, gl]))
    for it in items:
        lhs = jnp.concatenate([it["kap"], it["rt"].astype(BF16)], axis=0)
        it["m_k"] = _mm(lhs, bdt(it["kt"]))
        it["m_b"] = _mm(lhs, bdt(it["bt"]))
    for it in items:
        it["a_k"] = jnp.where(strict, it["m_k"][:c], 0.0)
        it["p_k"] = jnp.where(incl, it["m_k"][c:], 0.0)
        it["pw"] = -jnp.where(strict, it["m_b"][:c], 0.0)
        it["y"] = jnp.where(incl, it["m_b"][c:], 0.0)
        it["t_inv"] = None
    span = 1
    while span < c:
        last = 2 * span >= c
        for it in items:
            rows = ([] if last else [it["pw"]]) + ([] if it["t_inv"] is None else [it["t_inv"]]) + [it["y"]]
            out = _mm(jnp.concatenate(rows, axis=0), bd(it["pw"]))
            off = 0 if last else c
            if it["t_inv"] is None:
                it["t_inv"] = eye + it["pw"]
            else:
                it["t_inv"] = it["t_inv"] + out[off:off + c]
                off += c
            it["y"] = it["y"] + out[off:off + c]
            if not last:
                it["pw"] = out[:c]
        span *= 2
    for it in items:
        out = _mm(jnp.concatenate([it["a_k"], it["p_k"]], axis=0), bd(it["v"]))
        it["akv"], it["pkv"] = out[:c], out[c:]
    for it in items:
        ty = jnp.concatenate([it["t_inv"], it["y"]], axis=0).astype(BF16)
        out_k = _mm(ty, bd(it["kap"]))
        out_u = _mm(ty, bd(it["akv"]))
        it["k_p"], it["u0"] = out_k[:c], out_u[:c]
        it["r_p"] = (it["rt"] - out_k[c:]).astype(BF16)
        it["o0"] = it["pkv"] - out_u[c:]
    for it in items:
        it["w_bd"] = jnp.where(bd_mask, lax.dot_general(it["k_p"].astype(BF16), it["bhat"], TN_DIMS,
                                                        preferred_element_type=F32), 0.0).astype(BF16)
        s1_full = lax.dot_general(jnp.concatenate([it["v"], it["u0"].astype(BF16)], axis=0),
                                  jnp.concatenate([it["khat"], -it["bhat"]], axis=0),
                                  TN_DIMS, preferred_element_type=F32)
        it["s1"] = sum(jnp.where(lane_head == h, s1_full[h * d:(h + 1) * d], 0.0) for h in range(HEADS_PER_GROUP))
    state = {g0: state_ref[:, g0:g0 + MXU_LANES] for g0 in range(0, B_WIDTH, MXU_LANES)}
    o_parts = []
    for it in items:
        s0 = state[it["g0"]]
        o_parts.append(_mm(it["r_p"], bdt(s0)) + it["o0"])
        state[it["g0"]] = s0 * it["dec"] - _mm(s0, it["w_bd"]) + it["s1"]
    for g0, s_new in state.items():
        state_ref[:, g0:g0 + MXU_LANES] = s_new
    n_groups = B_WIDTH // MXU_LANES
    o = jnp.concatenate([jnp.concatenate(o_parts[i:i + n_groups], axis=1)
                         for i in range(0, len(o_parts), n_groups)], axis=0)
    o_ref[...] = _rwkv_finish(o, r, k, v, g, p).astype(o_ref.dtype)
    for h in range(B_HEADS):
        s_out_ref[0, h] = state_ref[:, h * d:(h + 1) * d]


def _rwkv_prompt(zb, p, n_seq, t):
    c = WKV_CHUNK
    assert c == B_HEAD_DIM and t % c == 0, "head packing puts WKV_CHUNK time steps where a head's lanes go"
    n_sub = WKV_CHUNKS_PER_STEP if t % (c * WKV_CHUNKS_PER_STEP) == 0 else 1
    tm = c * n_sub
    nc = t // tm
    width = zb.shape[1]
    return pl.pallas_call(
        functools.partial(_rwkv_chunk_kernel, n_sub=n_sub),
        grid=(n_seq, nc),
        in_specs=[pl.BlockSpec((tm, width), lambda b, i: (b * nc + i, 0))] + _rwkv_param_specs(p),
        out_specs=[pl.BlockSpec((tm, B_WIDTH), lambda b, i: (b * nc + i, 0)),
                   pl.BlockSpec((1, B_HEADS, B_HEAD_DIM, B_HEAD_DIM), lambda b, i: (b, 0, 0, 0))],
        out_shape=[jax.ShapeDtypeStruct((n_seq * t, B_WIDTH), BF16),
                   jax.ShapeDtypeStruct((n_seq, B_HEADS, B_HEAD_DIM, B_HEAD_DIM), F32)],
        scratch_shapes=[pltpu.VMEM((8, width), F32),
                        pltpu.VMEM((B_HEAD_DIM, B_WIDTH), F32)],
        compiler_params=_params(2),
        name="rwkv_prompt",
    )(zb, *_rwkv_param_args(p))


def _rwkv_sample_kernel(zb_ref, shift_ref, s_ref, s_all_ref, *refs, heads_per_step):
    del s_all_ref
    n_p = len(RWKV_PARAM_NAMES)
    p_refs = refs[:n_p]
    o_ref, s_out_ref, vec_ref, keep_ref, ot_ref = refs[n_p:]
    d = B_HEAD_DIM
    step = pl.program_id(0)

    @pl.when(step == 0)
    def _():
        p = {name: ref[...] for name, ref in zip(RWKV_PARAM_NAMES, p_refs)}
        r, lw, k, v, kk, b, g = _rwkv_prep(zb_ref[...], shift_ref[...], p)
        for idx, val in enumerate((r, jnp.exp(lw), k, v, kk, b)):
            vec_ref[idx] = val.T
        for idx, val in enumerate((r, k, v, g)):
            keep_ref[idx] = val

    for hh in range(heads_per_step):
        base = pl.multiple_of((step * heads_per_step + hh) * d, d)
        r_h, w_h, k_h, kk_h, b_h = (vec_ref[idx, pl.ds(base, d), :] for idx in (0, 1, 2, 4, 5))

        def value_row(i, carry, hh=hh, base=base, r_h=r_h, w_h=w_h, k_h=k_h, kk_h=kk_h, b_h=b_h):
            rows = pl.ds(pl.multiple_of(hh * d * d + i * d, d), d)
            s_i = s_ref[rows, :]
            sa = jnp.sum(s_i * kk_h, axis=0, keepdims=True)
            s_i = s_i * w_h - sa * b_h + vec_ref[3, pl.ds(base + i, 1), :] * k_h
            s_out_ref[rows, :] = s_i
            ot_ref[pl.ds(base + i, 1), :] = jnp.sum(s_i * r_h, axis=0, keepdims=True)
            return carry

        lax.fori_loop(0, d, value_row, 0, unroll=4)

    @pl.when(step == pl.num_programs(0) - 1)
    def _():
        p = {name: ref[...] for name, ref in zip(RWKV_PARAM_NAMES, p_refs)}
        o_ref[...] = _rwkv_finish(ot_ref[...].T, keep_ref[0], keep_ref[1], keep_ref[2], keep_ref[3],
                                  p).astype(o_ref.dtype)


def _rwkv_sample(zb, shift_all, state_t, s_all, layer, p):
    n, width = zb.shape
    d = B_HEAD_DIM
    hps = WKV_SAMPLE_HEADS_PER_STEP
    state_spec = pl.BlockSpec((None, hps * d * d, n), lambda i: (layer, i, 0))
    return pl.pallas_call(
        functools.partial(_rwkv_sample_kernel, heads_per_step=hps),
        grid=(B_HEADS // hps,),
        in_specs=[_const_spec((n, width)), pl.BlockSpec((None, n, width), lambda i: (layer, 0, 0)),
                  state_spec, pl.BlockSpec(memory_space=pl.ANY)] + _rwkv_param_specs(p),
        out_specs=[_const_spec((n, B_WIDTH)), state_spec],
        out_shape=[jax.ShapeDtypeStruct((n, B_WIDTH), BF16), jax.ShapeDtypeStruct(s_all.shape, F32)],
        input_output_aliases={3: 1},
        scratch_shapes=[pltpu.VMEM((6, B_WIDTH, n), F32), pltpu.VMEM((4, n, B_WIDTH), F32),
                        pltpu.VMEM((B_WIDTH, n), F32)],
        compiler_params=_params(),
        name="rwkv_sample",
    )(zb, shift_all, state_t, s_all, *_rwkv_param_args(p))


def _gelu_erf(x):
    return 0.5 * x * (1.0 + lax.erf(x * 0.7071067811865476))


def _gelu_tanh(x):
    c = 0.7978845608028654
    return x * (0.5 + 0.5 * jnp.tanh(x * (c + (0.044715 * c) * (x * x))))


def _layernorm(x, g, b):
    mu = jnp.mean(x, axis=-1, keepdims=True)
    d = x - mu
    var = jnp.mean(d * d, axis=-1, keepdims=True)
    return d * lax.rsqrt(var + LN_EPS) * g + b


def _gmlp_kernel(h_ref, gpre_ref, win_ref, lng_ref, lnb_ref, ws_ref, bs_ref, wout_ref, gpost_ref,
                 o_ref, *v_refs, chunk):
    x = h_ref[...]
    tm = x.shape[0]
    width = wout_ref.shape[0]
    hd = width // C_HEADS
    xn = _rmsnorm(x, gpre_ref[...])
    z = _gelu_erf(jnp.dot(xn.astype(BF16), win_ref[...], preferred_element_type=F32))
    u = z[:, :width]
    v = _layernorm(z[:, width:], lng_ref[...], lnb_ref[...])
    if chunk == 1:
        mixed = v * ws_ref[...] + bs_ref[...]
        v_refs[0][...] = v
    else:
        ri = lax.broadcasted_iota(jnp.int32, (chunk, chunk), 0)
        ci = lax.broadcasted_iota(jnp.int32, (chunk, chunk), 1)
        vb = v.astype(BF16)
        bias = bs_ref[...]
        cols = []
        for hh in range(C_HEADS):
            wtri = jnp.where(ri >= ci, ws_ref[hh], 0.0).astype(BF16)
            parts = [jnp.dot(wtri, vb[c0:c0 + chunk, hh * hd:(hh + 1) * hd], preferred_element_type=F32)
                     + bias[:, hh:hh + 1] for c0 in range(0, tm, chunk)]
            cols.append(jnp.concatenate(parts, axis=0) if len(parts) > 1 else parts[0])
        mixed = jnp.concatenate(cols, axis=1)
    y = u * mixed
    out = jnp.dot(y.astype(BF16), wout_ref[...], preferred_element_type=F32)
    o_ref[...] = x + _rmsnorm(out, gpost_ref[...])


def _gmlp(h, gpre, win, lng, lnb, ws, bs, wout, gpost, chunk, tm):
    rows, d = h.shape
    tm = min(tm, rows)
    width = wout.shape[0]
    out_specs = [pl.BlockSpec((tm, d), lambda i: (i, 0))]
    out_shape = [jax.ShapeDtypeStruct((rows, d), F32)]
    if chunk == 1:
        out_specs.append(pl.BlockSpec((tm, width), lambda i: (i, 0)))
        out_shape.append(jax.ShapeDtypeStruct((rows, width), F32))
    return pl.pallas_call(
        functools.partial(_gmlp_kernel, chunk=chunk),
        grid=(rows // tm,),
        in_specs=[pl.BlockSpec((tm, d), lambda i: (i, 0))]
        + [_spec(x) for x in (gpre, win, lng, lnb, ws, bs, wout, gpost)],
        out_specs=out_specs,
        out_shape=out_shape,
        compiler_params=_params(),
        name="gmlp",
    )(h, *[_arr(x) for x in (gpre, win, lng, lnb, ws, bs, wout, gpost)])


def _ffn_kernel(h_ref, *refs, carried, tiles_per_seq, mixed):
    x = h_ref[...]
    if mixed:
        a_ref, b_ref, wmix_ref, gmix_ref = refs[:4]
        refs = refs[4:]
        ka = a_ref.shape[1]
        acc = jnp.dot(a_ref[...], wmix_ref[:ka, :], preferred_element_type=F32)
        acc = acc + jnp.dot(b_ref[...], wmix_ref[ka:, :], preferred_element_type=F32)
        x = x + _rmsnorm(acc, gmix_ref[...])
    gpre_ref, wg_ref, wu_ref, cw_ref, cb_ref, wd_ref, gpost_ref = refs[:7]
    refs = refs[7:]
    tm = x.shape[0]
    f = wg_ref.shape[1]
    xn = _rmsnorm(x, gpre_ref[...]).astype(BF16)
    if carried:
        o_ref, tail_ref, carry_ref = refs

        @pl.when(pl.program_id(0) % tiles_per_seq == 0)
        def _():
            carry_ref[...] = jnp.zeros_like(carry_ref)
    else:
        s2_ref, s1_ref, o_ref, gp_ref = refs
    cw = cw_ref[...]
    cb = cb_ref[...]

    def in_proj(c0, c1):
        return (jnp.dot(xn, wg_ref[:, c0:c1], preferred_element_type=F32),
                jnp.dot(xn, wu_ref[:, c0:c1], preferred_element_type=F32))

    bounds = [(c0, min(c0 + FFN_COL_BLOCK, f)) for c0 in range(0, f, FFN_COL_BLOCK)]
    nxt = in_proj(*bounds[0])
    out = None
    for idx, (c0, c1) in enumerate(bounds):
        gp, up = nxt
        if idx + 1 < len(bounds):
            nxt = in_proj(*bounds[idx + 1])
        if carried:
            rows = lax.broadcasted_iota(jnp.int32, gp.shape, 0)
            t2, t1 = carry_ref[0:1, c0:c1], carry_ref[1:2, c0:c1]
            x1 = jnp.where(rows == 0, t1, pltpu.roll(gp, 1, 0))
            x2 = jnp.where(rows == 0, t2, jnp.where(rows == 1, t1, pltpu.roll(gp, 2, 0)))
            carry_ref[0:2, c0:c1] = gp[tm - 2:tm, :]
            tail_ref[0, :, c0:c1] = gp[tm - 2:tm, :]
        else:
            x2, x1 = s2_ref[:, c0:c1], s1_ref[:, c0:c1]
            gp_ref[:, c0:c1] = gp
        conv = cb[:, c0:c1] + cw[2:3, c0:c1] * gp + cw[0:1, c0:c1] * x2 + cw[1:2, c0:c1] * x1
        hid = (_gelu_tanh(conv) * up).astype(BF16)
        part = jnp.dot(hid, wd_ref[c0:c1, :], preferred_element_type=F32)
        out = part if out is None else out + part
    o_ref[...] = x + _rmsnorm(out, gpost_ref[...])


def _ffn(h, gpre, wg, wu, cw, cb, wd, gpost, tm, seq_len=None, state=None, mix=None):
    rows, d = h.shape
    f = wg.shape[1]
    carried = state is None
    tm = min(tm, rows, seq_len) if carried else min(tm, rows)
    row_spec = pl.BlockSpec((tm, d), lambda i: (i, 0))
    params = (gpre, wg, wu, cw, cb, wd, gpost)
    in_specs, args = [row_spec], [h]
    if mix is not None:
        a, b, w_mix, g_mix = mix
        in_specs += [pl.BlockSpec((tm, a.shape[1]), lambda i: (i, 0)), pl.BlockSpec((tm, b.shape[1]), lambda i: (i, 0)),
                     _spec(w_mix), _spec(g_mix)]
        args += [a, b, _arr(w_mix), _arr(g_mix)]
    in_specs += [_spec(x) for x in params]
    args += [_arr(x) for x in params]
    if carried:
        tiles_per_seq = seq_len // tm
        out_specs = [row_spec, pl.BlockSpec((1, 2, f), lambda i: (i // tiles_per_seq, 0, 0))]
        out_shape = [jax.ShapeDtypeStruct((rows, d), F32),
                     jax.ShapeDtypeStruct((rows // seq_len, 2, f), F32)]
        scratch = [pltpu.VMEM((8, f), F32)]
    else:
        tiles_per_seq = 1
        in_specs += [pl.BlockSpec((None, None, tm, f), lambda i, tap=tap: (state.layer, tap, i, 0)) for tap in (0, 1)]
        args += [state.arr, state.arr]
        out_specs = [row_spec, pl.BlockSpec((tm, f), lambda i: (i, 0))]
        out_shape = [jax.ShapeDtypeStruct((rows, d), F32), jax.ShapeDtypeStruct((rows, f), F32)]
        scratch = []
    return pl.pallas_call(
        functools.partial(_ffn_kernel, carried=carried, tiles_per_seq=tiles_per_seq, mixed=mix is not None),
        grid=(rows // tm,),
        in_specs=in_specs,
        out_specs=out_specs,
        out_shape=out_shape,
        scratch_shapes=scratch,
        compiler_params=_params(),
        name="conv_ffn",
    )(*args)


def kernel(x_prompt, x_sample, cache_win_k, cache_win_v, state_wkv, state_shift, state_ffn_conv,
           norm_mix_pre, norm_mix_post, norm_ffn_pre, norm_ffn_post,
           w_in_even, attn_sinks, shift_mu, decay_w0, decay_w2, iclr_a0, iclr_a2, gate_g2,
           key_k, key_a, bonus_r_k, gn_gain, gn_bias, w_out_even,
           w_in_odd, sgu_ln_gain, sgu_ln_bias, sgu_w, sgu_b, w_out_odd,
           ffn_w_gate, ffn_w_up, ffn_conv_w, ffn_conv_b, ffn_w_down):
    n_seq, t, d = x_prompt.shape
    n_dec, dec_t, _ = x_sample.shape
    assert dec_t == 1, "the sample path is written for one new token per sequence"
    depth = norm_mix_pre.shape[0]
    n_even = w_in_even.shape[0]
    wp = min(WINDOW, t)
    wb = cache_win_k.shape[2]
    hd = B_HEAD_DIM
    chunk = min(t, CHUNK)
    assert chunk == sgu_w.shape[-1], "the prompt is mixed in whole CHUNK-position chunks"

    hp = x_prompt.reshape(n_seq * t, d)
    hs = x_sample.reshape(n_dec, d)
    cos_p, sin_p = _rope_tables(jnp.arange(t, dtype=jnp.int32))
    cos_s, sin_s = _rope_tables(PAST_LEN + jnp.arange(1, dtype=jnp.int32))
    seg = jnp.kron(jnp.eye(HEADS_PER_GROUP, dtype=F32), jnp.ones((hd, hd), F32)).astype(BF16)

    rows3 = lambda a: a.reshape(a.shape[0], 1, -1)
    bf = lambda a: a.astype(BF16)
    stacked = dict(
        g_pre=rows3(norm_mix_pre), g_post=rows3(norm_mix_post), f_pre=rows3(norm_ffn_pre),
        f_post=rows3(norm_ffn_post), w_in_even=bf(w_in_even), w_out_even=bf(w_out_even),
        sinks=rows3(attn_sinks), mu=rows3(shift_mu), w0=rows3(decay_w0), w2=bf(decay_w2), a0=rows3(iclr_a0),
        a2=bf(iclr_a2), g2=bf(gate_g2), k_k=rows3(key_k), k_a=rows3(key_a), r_k=rows3(bonus_r_k),
        gn_g=rows3(gn_gain), gn_b=rows3(gn_bias), w_in_odd=bf(w_in_odd), w_out_odd=bf(w_out_odd),
        ln_g=rows3(sgu_ln_gain), ln_b=rows3(sgu_ln_bias), sgu_w=sgu_w,
        sgu_bt=jnp.swapaxes(sgu_b, 1, 2),
        sgu_w0=rows3(jnp.repeat(sgu_w[:, :, 0, 0], sgu_ln_gain.shape[1] // C_HEADS, axis=1)),
        sgu_b0=rows3(jnp.repeat(sgu_b[:, :, 0], sgu_ln_gain.shape[1] // C_HEADS, axis=1)),
        wg=bf(ffn_w_gate), wu=bf(ffn_w_up), wd=bf(ffn_w_down), cw=ffn_conv_w, cb=rows3(ffn_conv_b))
    cache_kt = jnp.transpose(cache_win_k, (0, 1, 3, 4, 2)).reshape(n_even, n_dec, A_KV, wb)
    cache_vt = jnp.transpose(cache_win_v, (0, 1, 3, 4, 2)).reshape(n_even, n_dec, A_KV, wb)
    wkv_t = jnp.transpose(state_wkv, (0, 2, 3, 4, 1)).reshape(n_even, B_HEADS * hd * hd, n_dec)
    wkv_new_t = jnp.zeros_like(wkv_t)
    conv_taps = jnp.swapaxes(state_ffn_conv, 1, 2)

    kp_l, vp_l, ks_l, vs_l, sp_l, shp_l, shs_l, vsgu_l, cp_l, cs_l = ([] for _ in range(10))
    for layer in range(depth):
        j = layer // 2
        at = lambda name, idx: _Layer(stacked[name], idx)
        g_pre, g_post = at("g_pre", layer), at("g_post", layer)
        if layer % 2 == 0:
            w_in, w_out, sinks = at("w_in_even", j), at("w_out_even", j), at("sinks", j)
            p = {name: at(name, j) for name in RWKV_PARAM_NAMES if name != "seg"}
            p["seg"] = seg
            splits = (A_PROJ, w_in.shape[1] - A_PROJ)
            za, zb = _norm_proj(hp, g_pre, w_in, splits, tm=ROW_TILE)
            oa, kp, vp = _attn_prompt(za, cos_p, sin_p, sinks, n_seq, t)
            ob, sp = _rwkv_prompt(zb, p, n_seq, t)
            mix_p = (oa, ob, w_out, g_post)
            kp_l.append(kp.reshape(n_seq, wp, A_KV_HEADS, HEAD_DIM))
            vp_l.append(vp.reshape(n_seq, wp, A_KV_HEADS, HEAD_DIM))
            sp_l.append(sp)
            shp_l.append(zb.reshape(n_seq, t, -1)[:, -1])
            za, zb = _norm_proj(hs, g_pre, w_in, splits, tm=ROW_TILE)
            oa, kn, vn = _attn_sample(za, cache_kt, cache_vt, j, cos_s, sin_s, sinks)
            ob, wkv_new_t = _rwkv_sample(zb, state_shift, wkv_t, wkv_new_t, j, p)
            mix_s = (oa, ob, w_out, g_post)
            ks_l.append(kn.reshape(n_dec, 1, A_KV_HEADS, HEAD_DIM))
            vs_l.append(vn.reshape(n_dec, 1, A_KV_HEADS, HEAD_DIM))
            shs_l.append(zb)
        else:
            w_in, w_out = at("w_in_odd", j), at("w_out_odd", j)
            lng, lnb = at("ln_g", j), at("ln_b", j)
            hp, = _gmlp(hp, g_pre, w_in, lng, lnb, at("sgu_w", j), at("sgu_bt", j), w_out, g_post,
                        chunk=chunk, tm=GMLP_CHUNKS_PER_STEP * chunk)
            hs, vsg = _gmlp(hs, g_pre, w_in, lng, lnb, at("sgu_w0", j), at("sgu_b0", j), w_out, g_post,
                            chunk=1, tm=ROW_TILE)
            vsgu_l.append(vsg.reshape(n_dec, 1, -1))
            mix_p = mix_s = None
        ffn_params = [at(name, layer) for name in ("f_pre", "wg", "wu", "cw", "cb", "wd", "f_post")]
        hp, cp = _ffn(hp, *ffn_params, tm=FFN_ROW_TILE, seq_len=t, mix=mix_p)
        hs, gp = _ffn(hs, *ffn_params, tm=FFN_ROW_TILE, state=_Layer(conv_taps, layer), mix=mix_s)
        cp_l.append(cp)
        cs_l.append(jnp.stack([state_ffn_conv[layer][:, 1], gp], axis=1))

    wkv_sample = jnp.transpose(wkv_new_t.reshape(n_even, B_HEADS, hd, hd, n_dec), (0, 4, 1, 2, 3))
    return (hp.reshape(n_seq, t, d), hs.reshape(n_dec, 1, d),
            jnp.stack(kp_l), jnp.stack(vp_l), jnp.stack(ks_l), jnp.stack(vs_l),
            jnp.stack(sp_l), wkv_sample, jnp.stack(shp_l), jnp.stack(shs_l),
            jnp.stack(vsgu_l), jnp.stack(cp_l), jnp.stack(cs_l))
```

```python
import functools

import jax
import jax.numpy as jnp
from jax import lax
from jax.experimental import pallas as pl
from jax.experimental.pallas import tpu as pltpu

F32 = jnp.float32
BF16 = jnp.bfloat16

PAST_LEN = 8192
HEAD_DIM = 64
A_HEADS = 8
A_KV_HEADS = 2
A_GROUP = A_HEADS // A_KV_HEADS
WINDOW = 128
ROPE_THETA = 10000.0
B_HEAD_DIM = 64
B_HEADS = 8
B_WIDTH = B_HEADS * B_HEAD_DIM
DECAY_RANK = 64
ICLR_RANK = 64
DECAY_SCALE = 0.606531
GN_EPS = 64e-5
CHUNK = 128
C_HEADS = 8
RMS_EPS = 1e-6
LN_EPS = 1e-5
NEG = -1e30
A_Q = A_HEADS * HEAD_DIM
A_KV = A_KV_HEADS * HEAD_DIM
A_PROJ = A_Q + 2 * A_KV

WKV_CHUNK = 64
WKV_CHUNKS_PER_STEP = 8
ATTN_BLOCKS_PER_STEP = 8
WKV_SAMPLE_HEADS_PER_STEP = 2
FFN_COL_BLOCK = 256
ROW_TILE = 1024
FFN_ROW_TILE = 256
GMLP_CHUNKS_PER_STEP = 4
MXU_LANES = 256
HEADS_PER_GROUP = MXU_LANES // B_HEAD_DIM
V7X_VMEM_LIMIT_BYTES = 56 * 1024 * 1024

NT_DIMS = (((1,), (1,)), ((), ()))
TN_DIMS = (((0,), (0,)), ((), ()))


def _params(n_axes=1):
    return pltpu.CompilerParams(dimension_semantics=("arbitrary",) * n_axes,
                                vmem_limit_bytes=V7X_VMEM_LIMIT_BYTES)


def _rmsnorm(x, g):
    return x * lax.rsqrt(jnp.mean(x * x, axis=-1, keepdims=True) + RMS_EPS) * g


def _bdot(a, b):
    return jnp.dot(a.astype(BF16), b.astype(BF16), preferred_element_type=F32)


def _const_spec(shape):
    return pl.BlockSpec(shape, lambda *_: (0,) * len(shape))


class _Layer:
    def __init__(self, arr, layer):
        self.arr, self.layer = arr, layer

    @property
    def shape(self):
        return self.arr.shape[1:]


def _spec(x):
    if isinstance(x, _Layer):
        idx = (x.layer,) + (0,) * len(x.shape)
        return pl.BlockSpec((None,) + x.shape, lambda *_: idx)
    return _const_spec(x.shape)


def _arr(x):
    return x.arr if isinstance(x, _Layer) else x


def _norm_proj_kernel(x_ref, g_ref, w_ref, *out_refs, splits):
    xn = _rmsnorm(x_ref[...], g_ref[...])
    z = jnp.dot(xn.astype(BF16), w_ref[...], preferred_element_type=F32)
    off = 0
    for o_ref, n in zip(out_refs, splits):
        o_ref[...] = z[:, off:off + n]
        off += n


def _norm_proj(x, g, w, splits, tm):
    rows, d = x.shape
    n = w.shape[1]
    tm = min(tm, rows)
    return pl.pallas_call(
        functools.partial(_norm_proj_kernel, splits=splits),
        grid=(rows // tm,),
        in_specs=[pl.BlockSpec((tm, d), lambda i: (i, 0)), _spec(g), _spec(w)],
        out_specs=[pl.BlockSpec((tm, s), lambda i: (i, 0)) for s in splits],
        out_shape=[jax.ShapeDtypeStruct((rows, s), F32) for s in splits],
        compiler_params=_params(),
        name="norm_proj",
    )(x, _arr(g), _arr(w))


def _rope_tables(pos):
    half = HEAD_DIM // 2
    inv = ROPE_THETA ** (-jnp.arange(half, dtype=F32) / half)
    ang = pos.astype(F32)[:, None] * inv[None, :]
    cos, sin = jnp.cos(ang), jnp.sin(ang)
    cos2 = jnp.concatenate([cos, cos], axis=-1)
    sin2 = jnp.concatenate([-sin, sin], axis=-1)
    return jnp.tile(cos2, (1, 2)), jnp.tile(sin2, (1, 2))


def _rope(x, cos, sin):
    width = x.shape[1]
    half = HEAD_DIM // 2
    lane = lax.broadcasted_iota(jnp.int32, x.shape, 1)
    upper = pltpu.roll(x, width - half, 1)
    lower = pltpu.roll(x, half, 1)
    rot = jnp.where((lane % HEAD_DIM) < half, upper, lower)
    return x * cos + rot * sin


def _tile_lanes(t, reps):
    return jnp.concatenate([t] * reps, axis=1)


def _attn_prompt_kernel(cur_ref, prev_ref, cos_ref, sin_ref, cosp_ref, sinp_ref, sink_ref,
                        o_ref, k_out_ref, v_out_ref):
    w = prev_ref.shape[0]
    n_blk = cur_ref.shape[0] // w
    cur = cur_ref[...]
    prev = prev_ref[...]
    cos, sin = cos_ref[...], sin_ref[...]
    q = (_rope(cur[:, :A_Q], _tile_lanes(cos, A_HEADS // 2), _tile_lanes(sin, A_HEADS // 2))
         * (HEAD_DIM ** -0.5)).astype(BF16)
    k = _rope(cur[:, A_Q:A_Q + A_KV], cos, sin)
    v = cur[:, A_Q + A_KV:A_PROJ]
    kp = _rope(prev[:, A_Q:A_Q + A_KV], cosp_ref[...], sinp_ref[...])
    vp = prev[:, A_Q + A_KV:A_PROJ]
    k_out_ref[0] = k[(n_blk - 1) * w:, :]
    v_out_ref[0] = v[(n_blk - 1) * w:, :]

    kk_all = jnp.concatenate([kp, k], axis=0).astype(BF16)
    vv_all = jnp.concatenate([vp, v], axis=0).astype(BF16)

    items = [(j, kv) for j in range(n_blk) for kv in range(A_KV_HEADS)]
    kv_lanes = lambda kv: slice(kv * HEAD_DIM, (kv + 1) * HEAD_DIM)
    heads = lambda kv: range(kv * A_GROUP, (kv + 1) * A_GROUP)
    kj = lax.broadcasted_iota(jnp.int32, (2 * w, A_GROUP * w), 0)
    qi = lax.broadcasted_iota(jnp.int32, (2 * w, A_GROUP * w), 1) % w
    diff = w + qi - kj
    band = (diff >= 0) & (diff <= WINDOW)
    first_band = band & ((pl.program_id(1) > 0) | (kj >= w))
    sink = [jnp.concatenate([jnp.broadcast_to(sink_ref[:, h:h + 1], (1, w)) for h in heads(kv)], axis=1)
            for j, kv in items]
    qs = [jnp.concatenate([q[j * w:(j + 1) * w, h * HEAD_DIM:(h + 1) * HEAD_DIM] for h in heads(kv)], axis=0)
          for j, kv in items]
    s = [lax.dot_general(kk_all[j * w:(j + 2) * w, kv_lanes(kv)], x, NT_DIMS, preferred_element_type=F32)
         for (j, kv), x in zip(items, qs)]
    s = [jnp.where(first_band if j == 0 else band, x, NEG) for (j, kv), x in zip(items, s)]
    m = [jnp.maximum(jnp.max(x, axis=0, keepdims=True), sk) for x, sk in zip(s, sink)]
    p = [jnp.exp(x - mx) for x, mx in zip(s, m)]
    den = [jnp.sum(x, axis=0, keepdims=True) + jnp.exp(sk - mx) for x, sk, mx in zip(p, sink, m)]
    o = [lax.dot_general(vv_all[j * w:(j + 2) * w, kv_lanes(kv)], x.astype(BF16), TN_DIMS,
                         preferred_element_type=F32) for (j, kv), x in zip(items, p)]
    o = [(x / dn).T for x, dn in zip(o, den)]
    o_ref[...] = jnp.concatenate(
        [jnp.concatenate([o[j * A_KV_HEADS + kv][g * w:(g + 1) * w] for kv in range(A_KV_HEADS)
                          for g in range(A_GROUP)], axis=1) for j in range(n_blk)], axis=0).astype(o_ref.dtype)


def _attn_prompt(za, cos, sin, sinks, n_seq, t):
    w = min(WINDOW, t)
    n_blk = ATTN_BLOCKS_PER_STEP if t % (w * ATTN_BLOCKS_PER_STEP) == 0 else 1
    tm = n_blk * w
    nt = t // tm
    cur_map = lambda b, i: (b * nt + i, 0)
    prev_map = lambda b, i: ((b * nt + i) * n_blk - jnp.minimum(i, 1), 0)
    tab_cur = lambda b, i: (i, 0)
    tab_prev = lambda b, i: (i * n_blk - jnp.minimum(i, 1), 0)
    return pl.pallas_call(
        _attn_prompt_kernel,
        grid=(n_seq, nt),
        in_specs=[pl.BlockSpec((tm, A_PROJ), cur_map), pl.BlockSpec((w, A_PROJ), prev_map),
                  pl.BlockSpec((tm, A_KV), tab_cur), pl.BlockSpec((tm, A_KV), tab_cur),
                  pl.BlockSpec((w, A_KV), tab_prev), pl.BlockSpec((w, A_KV), tab_prev),
                  _spec(sinks)],
        out_specs=[pl.BlockSpec((tm, A_Q), cur_map),
                   pl.BlockSpec((1, w, A_KV), lambda b, i: (b, 0, 0)),
                   pl.BlockSpec((1, w, A_KV), lambda b, i: (b, 0, 0))],
        out_shape=[jax.ShapeDtypeStruct((n_seq * t, A_Q), BF16),
                   jax.ShapeDtypeStruct((n_seq, w, A_KV), F32),
                   jax.ShapeDtypeStruct((n_seq, w, A_KV), F32)],
        compiler_params=_params(2),
        name="attn_prompt",
    )(za, za, cos, sin, cos, sin, _arr(sinks))


def _attn_sample_kernel(za_ref, ck_ref, cv_ref, cos_ref, sin_ref, sink_ref, o_ref, kn_ref, vn_ref):
    sb, wb = ck_ref.shape[0], ck_ref.shape[2]
    za = za_ref[...]
    cos, sin = cos_ref[...], sin_ref[...]
    q = _rope(za[:, :A_Q], _tile_lanes(cos, A_HEADS // 2), _tile_lanes(sin, A_HEADS // 2))
    k = _rope(za[:, A_Q:A_Q + A_KV], cos, sin)
    v = za[:, A_Q + A_KV:A_PROJ]
    kn_ref[...] = k
    vn_ref[...] = v
    rows = A_GROUP * sb
    row_seq = lax.broadcasted_iota(jnp.int32, (rows, sb * wb), 0) % sb
    col = lax.broadcasted_iota(jnp.int32, (rows, sb * wb), 1)
    own = (col >= row_seq * wb) & (col < (row_seq + 1) * wb)
    kvs = range(A_KV_HEADS)
    sl = [slice(kv * HEAD_DIM, (kv + 1) * HEAD_DIM) for kv in kvs]
    heads = [[kv * A_GROUP + g for g in range(A_GROUP)] for kv in kvs]
    kt = [jnp.concatenate([ck_ref[b, sl[kv], :] for b in range(sb)], axis=1).astype(BF16) for kv in kvs]
    vt = [jnp.concatenate([cv_ref[b, sl[kv], :] for b in range(sb)], axis=1).astype(BF16) for kv in kvs]
    qs =[jnp.concatenate([q[:, h * HEAD_DIM:(h + 1) * HEAD_DIM] for h in heads[kv]], axis=0) * (HEAD_DIM ** -0.5)
          for kv in kvs]
    sink = [jnp.concatenate([jnp.broadcast_to(sink_ref[:, h:h + 1], (sb, 1)) for h in heads[kv]], axis=0)
            for kv in kvs]
    s = [jnp.dot(qs[kv].astype(BF16), kt[kv], preferred_element_type=F32) for kv in kvs]
    s = [jnp.where(own, x, NEG) for x in s]
    k_rep = [jnp.concatenate([k[:, sl[kv]]] * A_GROUP, axis=0) for kv in kvs]
    v_rep = [jnp.concatenate([v[:, sl[kv]]] * A_GROUP, axis=0) for kv in kvs]
    s_new = [jnp.sum(qs[kv] * k_rep[kv], axis=-1, keepdims=True) for kv in kvs]
    m = [jnp.maximum(jnp.maximum(jnp.max(s[kv], axis=-1, keepdims=True), s_new[kv]), sink[kv]) for kv in kvs]
    p = [jnp.exp(s[kv] - m[kv]) for kv in kvs]
    p_new = [jnp.exp(s_new[kv] - m[kv]) for kv in kvs]
    den = [jnp.sum(p[kv], axis=-1, keepdims=True) + p_new[kv] + jnp.exp(sink[kv] - m[kv]) for kv in kvs]
    o = [lax.dot_general(p[kv].astype(BF16), vt[kv], NT_DIMS, preferred_element_type=F32) for kv in kvs]
    o = [(o[kv] + p_new[kv] * v_rep[kv]) / den[kv] for kv in kvs]
    o_ref[...] = jnp.concatenate([o[kv][g * sb:(g + 1) * sb] for kv in kvs for g in range(A_GROUP)],
                                 axis=1).astype(o_ref.dtype)


def _attn_sample(za, cache_kt, cache_vt, layer, cos, sin, sinks):
    n, wb = cache_kt.shape[1], cache_kt.shape[3]
    sb = min(16, n)
    cache_spec = pl.BlockSpec((None, sb, A_KV, wb), lambda i: (layer, i, 0, 0))
    return pl.pallas_call(
        _attn_sample_kernel,
        grid=(n // sb,),
        in_specs=[pl.BlockSpec((sb, A_PROJ), lambda i: (i, 0)), cache_spec, cache_spec,
                  _const_spec((1, A_KV)), _const_spec((1, A_KV)), _spec(sinks)],
        out_specs=[pl.BlockSpec((sb, A_Q), lambda i: (i, 0)),
                   pl.BlockSpec((sb, A_KV), lambda i: (i, 0)),
                   pl.BlockSpec((sb, A_KV), lambda i: (i, 0))],
        out_shape=[jax.ShapeDtypeStruct((n, A_Q), BF16),
                   jax.ShapeDtypeStruct((n, A_KV), F32),
                   jax.ShapeDtypeStruct((n, A_KV), F32)],
        compiler_params=_params(),
        name="attn_sample",
    )(za, cache_kt, cache_vt, cos, sin, _arr(sinks))


RWKV_PARAM_NAMES = ("mu", "w0", "w2", "a0", "a2", "g2", "k_k", "k_a", "r_k", "gn_g", "gn_b", "seg")


def _rwkv_param_specs(p):
    return [_spec(p[name]) for name in RWKV_PARAM_NAMES]


def _rwkv_param_args(p):
    return [_arr(p[name]) for name in RWKV_PARAM_NAMES]


def _rwkv_prep(zb, prev, p):
    o1, o2, o3 = B_WIDTH, 2 * B_WIDTH, 3 * B_WIDTH
    o4, o5 = o3 + DECAY_RANK, o3 + DECAY_RANK + ICLR_RANK
    zs = zb + (prev - zb) * p["mu"]
    r, k, v = zs[:, :o1], zs[:, o1:o2], zs[:, o2:o3]
    wd, ad, gd = zs[:, o3:o4], zs[:, o4:o5], zs[:, o5:]
    lw = -DECAY_SCALE * jax.nn.sigmoid(p["w0"] + _bdot(jnp.tanh(wd), p["w2"]))
    a = jax.nn.sigmoid(p["a0"] + _bdot(ad, p["a2"]))
    g = _bdot(jax.nn.sigmoid(gd), p["g2"])
    kk = k * p["k_k"]
    kk = kk / jnp.maximum(jnp.sqrt(_segsum(kk * kk, p["seg"])), 1e-12)
    k = k * (1.0 + (a - 1.0) * p["k_a"])
    return r, lw, k, v, kk, kk * a, g


def _split2(x):
    hi = x.astype(BF16)
    return hi, (x - hi.astype(F32)).astype(BF16)


def _segsum(x, seg):
    outs = []
    for g0 in range(0, x.shape[1], MXU_LANES):
        hi, lo = _split2(x[:, g0:g0 + MXU_LANES])
        outs.append(jnp.dot(hi, seg, preferred_element_type=F32) + jnp.dot(lo, seg, preferred_element_type=F32))
    return jnp.concatenate(outs, axis=1)


def _rwkv_finish(o, r, k, v, g, p):
    seg = p["seg"]
    mean = _segsum(o, seg) * (1.0 / B_HEAD_DIM)
    d = o - mean
    var = _segsum(d * d, seg) * (1.0 / B_HEAD_DIM)
    o = d * lax.rsqrt(var + GN_EPS) * p["gn_g"] + p["gn_b"]
    bonus = _segsum(r * k * p["r_k"], seg) * v
    return (o + bonus) * g


def _split3(x):
    hi = x.astype(BF16)
    r1 = x - hi.astype(F32)
    mid = r1.astype(BF16)
    return hi, mid, (r1 - mid.astype(F32)).astype(BF16)


def _mm(a, b):
    return jnp.dot(a.astype(BF16), b.astype(BF16), preferred_element_type=F32)


def _head_block_diag(x, bd_mask):
    tiled = jnp.concatenate([x.astype(BF16)] * HEADS_PER_GROUP, axis=0)
    return jnp.where(bd_mask, tiled, jnp.zeros_like(tiled))


def _head_block_diag_t(x, bd_mask):
    xt = x.astype(F32).T
    tiled = jnp.concatenate([xt.astype(BF16)] * HEADS_PER_GROUP, axis=1)
    return jnp.where(bd_mask, tiled, jnp.zeros_like(tiled))


def _rwkv_chunk_kernel(zb_ref, *refs, n_sub):
    n_p = len(RWKV_PARAM_NAMES)
    p = {name: ref[...] for name, ref in zip(RWKV_PARAM_NAMES, refs[:n_p])}
    o_ref, s_out_ref, carry_ref, state_ref = refs[n_p:]
    n_rows = zb_ref.shape[0]
    c = n_rows // n_sub
    d = B_HEAD_DIM

    @pl.when(pl.program_id(1) == 0)
    def _():
        carry_ref[...] = jnp.zeros_like(carry_ref)
        state_ref[...] = jnp.zeros_like(state_ref)

    zb = zb_ref[...]
    rows = lax.broadcasted_iota(jnp.int32, zb.shape, 0)
    prev = jnp.where(rows == 0, carry_ref[0:1, :], pltpu.roll(zb, 1, 0))
    carry_ref[0:1, :] = zb[n_rows - 1:n_rows, :]
    r, lw, k, v, kk, b, g = _rwkv_prep(zb, prev, p)

    ti = lax.broadcasted_iota(jnp.int32, (c, MXU_LANES), 0)
    si = lax.broadcasted_iota(jnp.int32, (c, MXU_LANES), 1) % d
    lane_head = lax.broadcasted_iota(jnp.int32, (c, MXU_LANES), 1) // d
    incl, strict = ti >= si, ti > si
    eye = (ti == si).astype(F32)
    bd_mask = (lax.broadcasted_iota(jnp.int32, (MXU_LANES, MXU_LANES), 0) // c
               == lax.broadcasted_iota(jnp.int32, (MXU_LANES, MXU_LANES), 1) // d)
    tri = (lax.broadcasted_iota(jnp.int32, (c, c), 0) >= lax.broadcasted_iota(jnp.int32, (c, c), 1)).astype(BF16)
    bd = functools.partial(_head_block_diag, bd_mask=bd_mask)
    bdt = functools.partial(_head_block_diag_t, bd_mask=bd_mask)

    items = []
    for sub in range(n_sub):
        rs = slice(sub * c, (sub + 1) * c)
        lw_c = lw[rs]
        gcum = sum(jnp.dot(tri, part, preferred_element_type=F32) for part in _split3(lw_c))
        g_last = gcum[c - 1:c, :]
        e_neg = jnp.exp(-gcum)
        e_hat = jnp.exp(g_last - gcum)
        kap = (kk[rs] * jnp.exp(gcum - lw_c)).astype(BF16)
        rt = r[rs] * jnp.exp(gcum)
        kt, bt = k[rs] * e_neg, b[rs] * e_neg
        khat, bhat = (k[rs] * e_hat).astype(BF16), (b[rs] * e_hat).astype(BF16)
        dec_last = jnp.exp(g_last)
        v_c = v[rs].astype(BF16)
        for g0 in range(0, B_WIDTH, MXU_LANES):
            gl = slice(g0, g0 + MXU_LANES)
            items.append(dict(g0=g0, kap=kap[:, gl], rt=rt[:, gl], kt=kt[:, gl], bt=bt[:, gl], khat=khat[:, gl],
                              bhat=bhat[:, gl], v=v_c[:, gl], dec=dec_last[:, gl]))
    for it in items:
        lhs = jnp.concatenate([it["kap"], it["rt"].astype(BF16)], axis=0)
        it["m_k"] = _mm(lhs, bdt(it["kt"]))
        it["m_b"] = _mm(lhs, bdt(it["bt"]))
    for it in items:
        it["a_k"] = jnp.where(strict, it["m_k"][:c], 0.0)
        it["p_k"] = jnp.where(incl, it["m_k"][c:], 0.0)
        it["pw"] = -jnp.where(strict, it["m_b"][:c], 0.0)
        it["y"] = jnp.where(incl, it["m_b"][c:], 0.0)
        it["t_inv"] = None
    span = 1
    while span < c:
        last = 2 * span >= c
        for it in items:
            rows = ([] if last else [it["pw"]]) + ([] if it["t_inv"] is None else [it["t_inv"]]) + [it["y"]]
            out = _mm(jnp.concatenate(rows, axis=0), bd(it["pw"]))
            off = 0 if last else c
            if it["t_inv"] is None:
                it["t_inv"] = eye + it["pw"]
            else:
                it["t_inv"] = it["t_inv"] + out[off:off + c]
                off += c
            it["y"] = it["y"] + out[off:off + c]
            if not last:
                it["pw"] = out[:c]
        span *= 2
    for it in items:
        out = _mm(jnp.concatenate([it["a_k"], it["p_k"]], axis=0), bd(it["v"]))
        it["akv"], it["pkv"] = out[:c], out[c:]
    for it in items:
        ty = jnp.concatenate([it["t_inv"], it["y"]], axis=0).astype(BF16)
        out_k = _mm(ty, bd(it["kap"]))
        out_u = _mm(ty, bd(it["akv"]))
        it["k_p"], it["u0"] = out_k[:c], out_u[:c]
        it["r_p"] = (it["rt"] - out_k[c:]).astype(BF16)
        it["o0"] = it["pkv"] - out_u[c:]
    for it in items:
        it["w_bd"] = jnp.where(bd_mask, lax.dot_general(it["k_p"].astype(BF16), it["bhat"], TN_DIMS,
                                                        preferred_element_type=F32), 0.0).astype(BF16)
        s1_full = lax.dot_general(jnp.concatenate([it["v"], it["u0"].astype(BF16)], axis=0),
                                  jnp.concatenate([it["khat"], -it["bhat"]], axis=0),
                                  TN_DIMS, preferred_element_type=F32)
        it["s1"] = sum(jnp.where(lane_head == h, s1_full[h * d:(h + 1) * d], 0.0) for h in range(HEADS_PER_GROUP))
    state = {g0: state_ref[:, g0:g0 + MXU_LANES] for g0 in range(0, B_WIDTH, MXU_LANES)}
    o_parts = []
    for it in items:
        s0 = state[it["g0"]]
        o_parts.append(_mm(it["r_p"], bdt(s0)) + it["o0"])
        state[it["g0"]] = s0 * it["dec"] - _mm(s0, it["w_bd"]) + it["s1"]
    for g0, s_new in state.items():
        state_ref[:, g0:g0 + MXU_LANES] = s_new
    n_groups = B_WIDTH // MXU_LANES
    o = jnp.concatenate([jnp.concatenate(o_parts[i:i + n_groups], axis=1)
                         for i in range(0, len(o_parts), n_groups)], axis=0)
    o_ref[...] = _rwkv_finish(o, r, k, v, g, p).astype(o_ref.dtype)
    for h in range(B_HEADS):
        s_out_ref[0, h] = state_ref[:, h * d:(h + 1) * d]


def _rwkv_prompt(zb, p, n_seq, t):
    c = WKV_CHUNK
    assert c == B_HEAD_DIM and t % c == 0, "head packing puts WKV_CHUNK time steps where a head's lanes go"
    n_sub = WKV_CHUNKS_PER_STEP if t % (c * WKV_CHUNKS_PER_STEP) == 0 else 1
    tm = c * n_sub
    nc = t // tm
    width = zb.shape[1]
    return pl.pallas_call(
        functools.partial(_rwkv_chunk_kernel, n_sub=n_sub),
        grid=(n_seq, nc),
        in_specs=[pl.BlockSpec((tm, width), lambda b, i: (b * nc + i, 0))] + _rwkv_param_specs(p),
        out_specs=[pl.BlockSpec((tm, B_WIDTH), lambda b, i: (b * nc + i, 0)),
                   pl.BlockSpec((1, B_HEADS, B_HEAD_DIM, B_HEAD_DIM), lambda b, i: (b, 0, 0, 0))],
        out_shape=[jax.ShapeDtypeStruct((n_seq * t, B_WIDTH), BF16),
                   jax.ShapeDtypeStruct((n_seq, B_HEADS, B_HEAD_DIM, B_HEAD_DIM), F32)],
        scratch_shapes=[pltpu.VMEM((8, width), F32),
                        pltpu.VMEM((B_HEAD_DIM, B_WIDTH), F32)],
        compiler_params=_params(2),
        name="rwkv_prompt",
    )(zb, *_rwkv_param_args(p))


def _rwkv_sample_kernel(zb_ref, shift_ref, s_ref, s_all_ref, *refs, heads_per_step):
    del s_all_ref
    n_p = len(RWKV_PARAM_NAMES)
    p_refs = refs[:n_p]
    o_ref, s_out_ref, vec_ref, keep_ref, ot_ref = refs[n_p:]
    d = B_HEAD_DIM
    step = pl.program_id(0)

    @pl.when(step == 0)
    def _():
        p = {name: ref[...] for name, ref in zip(RWKV_PARAM_NAMES, p_refs)}
        r, lw, k, v, kk, b, g = _rwkv_prep(zb_ref[...], shift_ref[...], p)
        for idx, val in enumerate((r, jnp.exp(lw), k, v, kk, b)):
            vec_ref[idx] = val.T
        for idx, val in enumerate((r, k, v, g)):
            keep_ref[idx] = val

    for hh in range(heads_per_step):
        base = pl.multiple_of((step * heads_per_step + hh) * d, d)
        r_h, w_h, k_h, kk_h, b_h = (vec_ref[idx, pl.ds(base, d), :] for idx in (0, 1, 2, 4, 5))

        def value_row(i, carry, hh=hh, base=base, r_h=r_h, w_h=w_h, k_h=k_h, kk_h=kk_h, b_h=b_h):
            rows = pl.ds(pl.multiple_of(hh * d * d + i * d, d), d)
            s_i = s_ref[rows, :]
            sa = jnp.sum(s_i * kk_h, axis=0, keepdims=True)
            s_i = s_i * w_h - sa * b_h + vec_ref[3, pl.ds(base + i, 1), :] * k_h
            s_out_ref[rows, :] = s_i
            ot_ref[pl.ds(base + i, 1), :] = jnp.sum(s_i * r_h, axis=0, keepdims=True)
            return carry

        lax.fori_loop(0, d, value_row, 0, unroll=4)

    @pl.when(step == pl.num_programs(0) - 1)
    def _():
        p = {name: ref[...] for name, ref in zip(RWKV_PARAM_NAMES, p_refs)}
        o_ref[...] = _rwkv_finish(ot_ref[...].T, keep_ref[0], keep_ref[1], keep_ref[2], keep_ref[3],
                                  p).astype(o_ref.dtype)


def _rwkv_sample(zb, shift_all, state_t, s_all, layer, p):
    n, width = zb.shape
    d = B_HEAD_DIM
    hps = WKV_SAMPLE_HEADS_PER_STEP
    state_spec = pl.BlockSpec((None, hps * d * d, n), lambda i: (layer, i, 0))
    return pl.pallas_call(
        functools.partial(_rwkv_sample_kernel, heads_per_step=hps),
        grid=(B_HEADS // hps,),
        in_specs=[_const_spec((n, width)), pl.BlockSpec((None, n, width), lambda i: (layer, 0, 0)),
                  state_spec, pl.BlockSpec(memory_space=pl.ANY)] + _rwkv_param_specs(p),
        out_specs=[_const_spec((n, B_WIDTH)), state_spec],
        out_shape=[jax.ShapeDtypeStruct((n, B_WIDTH), BF16), jax.ShapeDtypeStruct(s_all.shape, F32)],
        input_output_aliases={3: 1},
        scratch_shapes=[pltpu.VMEM((6, B_WIDTH, n), F32), pltpu.VMEM((4, n, B_WIDTH), F32),
                        pltpu.VMEM((B_WIDTH, n), F32)],
        compiler_params=_params(),
        name="rwkv_sample",
    )(zb, shift_all, state_t, s_all, *_rwkv_param_args(p))


def _gelu_erf(x):
    return 0.5 * x * (1.0 + lax.erf(x * 0.7071067811865476))


def _gelu_tanh(x):
    c = 0.7978845608028654
    return x * (0.5 + 0.5 * jnp.tanh(x * (c + (0.044715 * c) * (x * x))))


def _layernorm(x, g, b):
    mu = jnp.mean(x, axis=-1, keepdims=True)
    d = x - mu
    var = jnp.mean(d * d, axis=-1, keepdims=True)
    return d * lax.rsqrt(var + LN_EPS) * g + b


def _gmlp_kernel(h_ref, gpre_ref, win_ref, lng_ref, lnb_ref, ws_ref, bs_ref, wout_ref, gpost_ref,
                 o_ref, *v_refs, chunk):
    x = h_ref[...]
    tm = x.shape[0]
    width = wout_ref.shape[0]
    hd = width // C_HEADS
    xn = _rmsnorm(x, gpre_ref[...])
    z = _gelu_erf(jnp.dot(xn.astype(BF16), win_ref[...], preferred_element_type=F32))
    u = z[:, :width]
    v = _layernorm(z[:, width:], lng_ref[...], lnb_ref[...])
    if chunk == 1:
        mixed = v * ws_ref[...] + bs_ref[...]
        v_refs[0][...] = v
    else:
        ri = lax.broadcasted_iota(jnp.int32, (chunk, chunk), 0)
        ci = lax.broadcasted_iota(jnp.int32, (chunk, chunk), 1)
        vb = v.astype(BF16)
        bias = bs_ref[...]
        cols = []
        for hh in range(C_HEADS):
            wtri = jnp.where(ri >= ci, ws_ref[hh], 0.0).astype(BF16)
            parts = [jnp.dot(wtri, vb[c0:c0 + chunk, hh * hd:(hh + 1) * hd], preferred_element_type=F32)
                     + bias[:, hh:hh + 1] for c0 in range(0, tm, chunk)]
            cols.append(jnp.concatenate(parts, axis=0) if len(parts) > 1 else parts[0])
        mixed = jnp.concatenate(cols, axis=1)
    y = u * mixed
    out = jnp.dot(y.astype(BF16), wout_ref[...], preferred_element_type=F32)
    o_ref[...] = x + _rmsnorm(out, gpost_ref[...])


def _gmlp(h, gpre, win, lng, lnb, ws, bs, wout, gpost, chunk, tm):
    rows, d = h.shape
    tm = min(tm, rows)
    width = wout.shape[0]
    out_specs = [pl.BlockSpec((tm, d), lambda i: (i, 0))]
    out_shape = [jax.ShapeDtypeStruct((rows, d), F32)]
    if chunk == 1:
        out_specs.append(pl.BlockSpec((tm, width), lambda i: (i, 0)))
        out_shape.append(jax.ShapeDtypeStruct((rows, width), F32))
    return pl.pallas_call(
        functools.partial(_gmlp_kernel, chunk=chunk),
        grid=(rows // tm,),
        in_specs=[pl.BlockSpec((tm, d), lambda i: (i, 0))]
        + [_spec(x) for x in (gpre, win, lng, lnb, ws, bs, wout, gpost)],
        out_specs=out_specs,
        out_shape=out_shape,
        compiler_params=_params(),
        name="gmlp",
    )(h, *[_arr(x) for x in (gpre, win, lng, lnb, ws, bs, wout, gpost)])


def _ffn_kernel(h_ref, *refs, carried, tiles_per_seq, mixed):
    x = h_ref[...]
    if mixed:
        a_ref, b_ref, wmix_ref, gmix_ref = refs[:4]
        refs = refs[4:]
        ka = a_ref.shape[1]
        acc = jnp.dot(a_ref[...], wmix_ref[:ka, :], preferred_element_type=F32)
        acc = acc + jnp.dot(b_ref[...], wmix_ref[ka:, :], preferred_element_type=F32)
        x = x + _rmsnorm(acc, gmix_ref[...])
    gpre_ref, wg_ref, wu_ref, cw_ref, cb_ref, wd_ref, gpost_ref = refs[:7]
    refs = refs[7:]
    tm = x.shape[0]
    f = wg_ref.shape[1]
    xn = _rmsnorm(x, gpre_ref[...]).astype(BF16)
    if carried:
        o_ref, tail_ref, carry_ref = refs

        @pl.when(pl.program_id(0) % tiles_per_seq == 0)
        def _():
            carry_ref[...] = jnp.zeros_like(carry_ref)
    else:
        s2_ref, s1_ref, o_ref, gp_ref = refs
    cw = cw_ref[...]
    cb = cb_ref[...]

    def in_proj(c0, c1):
        return (jnp.dot(xn, wg_ref[:, c0:c1], preferred_element_type=F32),
                jnp.dot(xn, wu_ref[:, c0:c1], preferred_element_type=F32))

    bounds = [(c0, min(c0 + FFN_COL_BLOCK, f)) for c0 in range(0, f, FFN_COL_BLOCK)]
    nxt = in_proj(*bounds[0])
    out = None
    for idx, (c0, c1) in enumerate(bounds):
        gp, up = nxt
        if idx + 1 < len(bounds):
            nxt = in_proj(*bounds[idx + 1])
        if carried:
            rows = lax.broadcasted_iota(jnp.int32, gp.shape, 0)
            t2, t1 = carry_ref[0:1, c0:c1], carry_ref[1:2, c0:c1]
            x1 = jnp.where(rows == 0, t1, pltpu.roll(gp, 1, 0))
            x2 = jnp.where(rows == 0, t2, jnp.where(rows == 1, t1, pltpu.roll(gp, 2, 0)))
            carry_ref[0:2, c0:c1] = gp[tm - 2:tm, :]
            tail_ref[0, :, c0:c1] = gp[tm - 2:tm, :]
        else:
            x2, x1 = s2_ref[:, c0:c1], s1_ref[:, c0:c1]
            gp_ref[:, c0:c1] = gp
        conv = cb[:, c0:c1] + cw[2:3, c0:c1] * gp + cw[0:1, c0:c1] * x2 + cw[1:2, c0:c1] * x1
        hid = (_gelu_tanh(conv) * up).astype(BF16)
        part = jnp.dot(hid, wd_ref[c0:c1, :], preferred_element_type=F32)
        out = part if out is None else out + part
    o_ref[...] = x + _rmsnorm(out, gpost_ref[...])


def _ffn(h, gpre, wg, wu, cw, cb, wd, gpost, tm, seq_len=None, state=None, mix=None):
    rows, d = h.shape
    f = wg.shape[1]
    carried = state is None
    tm = min(tm, rows, seq_len) if carried else min(tm, rows)
    row_spec = pl.BlockSpec((tm, d), lambda i: (i, 0))
    params = (gpre, wg, wu, cw, cb, wd, gpost)
    in_specs, args = [row_spec], [h]
    if mix is not None:
        a, b, w_mix, g_mix = mix
        in_specs += [pl.BlockSpec((tm, a.shape[1]), lambda i: (i, 0)), pl.BlockSpec((tm, b.shape[1]), lambda i: (i, 0)),
                     _spec(w_mix), _spec(g_mix)]
        args += [a, b, _arr(w_mix), _arr(g_mix)]
    in_specs += [_spec(x) for x in params]
    args += [_arr(x) for x in params]
    if carried:
        tiles_per_seq = seq_len // tm
        out_specs = [row_spec, pl.BlockSpec((1, 2, f), lambda i: (i // tiles_per_seq, 0, 0))]
        out_shape = [jax.ShapeDtypeStruct((rows, d), F32),
                     jax.ShapeDtypeStruct((rows // seq_len, 2, f), F32)]
        scratch = [pltpu.VMEM((8, f), F32)]
    else:
        tiles_per_seq = 1
        in_specs += [pl.BlockSpec((None, None, tm, f), lambda i, tap=tap: (state.layer, tap, i, 0)) for tap in (0, 1)]
        args += [state.arr, state.arr]
        out_specs = [row_spec, pl.BlockSpec((tm, f), lambda i: (i, 0))]
        out_shape = [jax.ShapeDtypeStruct((rows, d), F32), jax.ShapeDtypeStruct((rows, f), F32)]
        scratch = []
    return pl.pallas_call(
        functools.partial(_ffn_kernel, carried=carried, tiles_per_seq=tiles_per_seq, mixed=mix is not None),
        grid=(rows // tm,),
        in_specs=in_specs,
        out_specs=out_specs,
        out_shape=out_shape,
        scratch_shapes=scratch,
        compiler_params=_params(),
        name="conv_ffn",
    )(*args)


def kernel(x_prompt, x_sample, cache_win_k, cache_win_v, state_wkv, state_shift, state_ffn_conv,
           norm_mix_pre, norm_mix_post, norm_ffn_pre, norm_ffn_post,
           w_in_even, attn_sinks, shift_mu, decay_w0, decay_w2, iclr_a0, iclr_a2, gate_g2,
           key_k, key_a, bonus_r_k, gn_gain, gn_bias, w_out_even,
           w_in_odd, sgu_ln_gain, sgu_ln_bias, sgu_w, sgu_b, w_out_odd,
           ffn_w_gate, ffn_w_up, ffn_conv_w, ffn_conv_b, ffn_w_down):
    n_seq, t, d = x_prompt.shape
    n_dec, dec_t, _ = x_sample.shape
    assert dec_t == 1, "the sample path is written for one new token per sequence"
    depth = norm_mix_pre.shape[0]
    n_even = w_in_even.shape[0]
    wp = min(WINDOW, t)
    wb = cache_win_k.shape[2]
    hd = B_HEAD_DIM
    chunk = min(t, CHUNK)
    assert chunk == sgu_w.shape[-1], "the prompt is mixed in whole CHUNK-position chunks"

    hp = x_prompt.reshape(n_seq * t, d)
    hs = x_sample.reshape(n_dec, d)
    cos_p, sin_p = _rope_tables(jnp.arange(t, dtype=jnp.int32))
    cos_s, sin_s = _rope_tables(PAST_LEN + jnp.arange(1, dtype=jnp.int32))
    seg = jnp.kron(jnp.eye(HEADS_PER_GROUP, dtype=F32), jnp.ones((hd, hd), F32)).astype(BF16)

    rows3 = lambda a: a.reshape(a.shape[0], 1, -1)
    bf = lambda a: a.astype(BF16)
    stacked = dict(
        g_pre=rows3(norm_mix_pre), g_post=rows3(norm_mix_post), f_pre=rows3(norm_ffn_pre),
        f_post=rows3(norm_ffn_post), w_in_even=bf(w_in_even), w_out_even=bf(w_out_even),
        sinks=rows3(attn_sinks), mu=rows3(shift_mu), w0=rows3(decay_w0), w2=bf(decay_w2), a0=rows3(iclr_a0),
        a2=bf(iclr_a2), g2=bf(gate_g2), k_k=rows3(key_k), k_a=rows3(key_a), r_k=rows3(bonus_r_k),
        gn_g=rows3(gn_gain), gn_b=rows3(gn_bias), w_in_odd=bf(w_in_odd), w_out_odd=bf(w_out_odd),
        ln_g=rows3(sgu_ln_gain), ln_b=rows3(sgu_ln_bias), sgu_w=sgu_w,
        sgu_bt=jnp.swapaxes(sgu_b, 1, 2),
        sgu_w0=rows3(jnp.repeat(sgu_w[:, :, 0, 0], sgu_ln_gain.shape[1] // C_HEADS, axis=1)),
        sgu_b0=rows3(jnp.repeat(sgu_b[:, :, 0], sgu_ln_gain.shape[1] // C_HEADS, axis=1)),
        wg=bf(ffn_w_gate), wu=bf(ffn_w_up), wd=bf(ffn_w_down), cw=ffn_conv_w, cb=rows3(ffn_conv_b))
    cache_kt = jnp.transpose(cache_win_k, (0, 1, 3, 4, 2)).reshape(n_even, n_dec, A_KV, wb)
    cache_vt = jnp.transpose(cache_win_v, (0, 1, 3, 4, 2)).reshape(n_even, n_dec, A_KV, wb)
    wkv_t = jnp.transpose(state_wkv, (0, 2, 3, 4, 1)).reshape(n_even, B_HEADS * hd * hd, n_dec)
    wkv_new_t = jnp.zeros_like(wkv_t)
    conv_taps = jnp.swapaxes(state_ffn_conv, 1, 2)

    kp_l, vp_l, ks_l, vs_l, sp_l, shp_l, shs_l, vsgu_l, cp_l, cs_l = ([] for _ in range(10))
    for layer in range(depth):
        j = layer // 2
        at = lambda name, idx: _Layer(stacked[name], idx)
        g_pre, g_post = at("g_pre", layer), at("g_post", layer)
        if layer % 2 == 0:
            w_in, w_out, sinks = at("w_in_even", j), at("w_out_even", j), at("sinks", j)
            p = {name: at(name, j) for name in RWKV_PARAM_NAMES if name != "seg"}
            p["seg"] = seg
            splits = (A_PROJ, w_in.shape[1] - A_PROJ)
            za, zb = _norm_proj(hp, g_pre, w_in, splits, tm=ROW_TILE)
            oa, kp, vp = _attn_prompt(za, cos_p, sin_p, sinks, n_seq, t)
            ob, sp = _rwkv_prompt(zb, p, n_seq, t)
            mix_p = (oa, ob, w_out, g_post)
            kp_l.append(kp.reshape(n_seq, wp, A_KV_HEADS, HEAD_DIM))
            vp_l.append(vp.reshape(n_seq, wp, A_KV_HEADS, HEAD_DIM))
            sp_l.append(sp)
            shp_l.append(zb.reshape(n_seq, t, -1)[:, -1])
            za, zb = _norm_proj(hs, g_pre, w_in, splits, tm=ROW_TILE)
            oa, kn, vn = _attn_sample(za, cache_kt, cache_vt, j, cos_s, sin_s, sinks)
            ob, wkv_new_t = _rwkv_sample(zb, state_shift, wkv_t, wkv_new_t, j, p)
            mix_s = (oa, ob, w_out, g_post)
            ks_l.append(kn.reshape(n_dec, 1, A_KV_HEADS, HEAD_DIM))
            vs_l.append(vn.reshape(n_dec, 1, A_KV_HEADS, HEAD_DIM))
            shs_l.append(zb)
        else:
            w_in, w_out = at("w_in_odd", j), at("w_out_odd", j)
            lng, lnb = at("ln_g", j), at("ln_b", j)
            hp, = _gmlp(hp, g_pre, w_in, lng, lnb, at("sgu_w", j), at("sgu_bt", j), w_out, g_post,
                        chunk=chunk, tm=GMLP_CHUNKS_PER_STEP * chunk)
            hs, vsg = _gmlp(hs, g_pre, w_in, lng, lnb, at("sgu_w0", j), at("sgu_b0", j), w_out, g_post,
                            chunk=1, tm=ROW_TILE)
            vsgu_l.append(vsg.reshape(n_dec, 1, -1))
            mix_p = mix_s = None
        ffn_params = [at(name, layer) for name in ("f_pre", "wg", "wu", "cw", "cb", "wd", "f_post")]
        hp, cp = _ffn(hp, *ffn_params, tm=FFN_ROW_TILE, seq_len=t, mix=mix_p)
        hs, gp = _ffn(hs, *ffn_params, tm=FFN_ROW_TILE, state=_Layer(conv_taps, layer), mix=mix_s)
        cp_l.append(cp)
        cs_l.append(jnp.stack([state_ffn_conv[layer][:, 1], gp], axis=1))

    wkv_sample = jnp.transpose(wkv_new_t.reshape(n_even, B_HEADS, hd, hd, n_dec), (0, 4, 1, 2, 3))
    return (hp.reshape(n_seq, t, d), hs.reshape(n_dec, 1, d),
            jnp.stack(kp_l), jnp.stack(vp_l), jnp.stack(ks_l), jnp.stack(vs_l),
            jnp.stack(sp_l), wkv_sample, jnp.stack(shp_l), jnp.stack(shs_l),
            jnp.stack(vsgu_l), jnp.stack(cp_l), jnp.stack(cs_l))
```

```python
import functools

import jax
import jax.numpy as jnp
from jax import lax
from jax.experimental import pallas as pl
from jax.experimental.pallas import tpu as pltpu

F32 = jnp.float32
BF16 = jnp.bfloat16

PAST_LEN = 8192
HEAD_DIM = 64
A_HEADS = 8
A_KV_HEADS = 2
A_GROUP = A_HEADS // A_KV_HEADS
WINDOW = 128
ROPE_THETA = 10000.0
B_HEAD_DIM = 64
B_HEADS = 8
B_WIDTH = B_HEADS * B_HEAD_DIM
DECAY_RANK = 64
ICLR_RANK = 64
DECAY_SCALE = 0.606531
GN_EPS = 64e-5
CHUNK = 128
C_HEADS = 8
RMS_EPS = 1e-6
LN_EPS = 1e-5
NEG = -1e30
A_Q = A_HEADS * HEAD_DIM
A_KV = A_KV_HEADS * HEAD_DIM
A_PROJ = A_Q + 2 * A_KV

WKV_CHUNK = 64
WKV_CHUNKS_PER_STEP = 16
ATTN_BLOCKS_PER_STEP = 16
WKV_SAMPLE_HEADS_PER_STEP = 2
FFN_COL_BLOCK = 256
ROW_TILE = 1024
FFN_ROW_TILE = 256
GMLP_CHUNKS_PER_STEP = 8
MXU_LANES = 256
HEADS_PER_GROUP = MXU_LANES // B_HEAD_DIM
V7X_VMEM_LIMIT_BYTES = 56 * 1024 * 1024

NT_DIMS = (((1,), (1,)), ((), ()))
TN_DIMS = (((0,), (0,)), ((), ()))


def _params(n_axes=1):
    return pltpu.CompilerParams(dimension_semantics=("arbitrary",) * n_axes,
                                vmem_limit_bytes=V7X_VMEM_LIMIT_BYTES)


def _rmsnorm(x, g):
    return x * lax.rsqrt(jnp.mean(x * x, axis=-1, keepdims=True) + RMS_EPS) * g


def _bdot(a, b):
    return jnp.dot(a.astype(BF16), b.astype(BF16), preferred_element_type=F32)


def _const_spec(shape):
    return pl.BlockSpec(shape, lambda *_: (0,) * len(shape))


class _Layer:
    def __init__(self, arr, layer):
        self.arr, self.layer = arr, layer

    @property
    def shape(self):
        return self.arr.shape[1:]


def _spec(x):
    if isinstance(x, _Layer):
        idx = (x.layer,) + (0,) * len(x.shape)
        return pl.BlockSpec((None,) + x.shape, lambda *_: idx)
    return _const_spec(x.shape)


def _arr(x):
    return x.arr if isinstance(x, _Layer) else x


def _norm_proj_kernel(x_ref, g_ref, w_ref, *out_refs, splits):
    xn = _rmsnorm(x_ref[...], g_ref[...])
    z = jnp.dot(xn.astype(BF16), w_ref[...], preferred_element_type=F32)
    off = 0
    for o_ref, n in zip(out_refs, splits):
        o_ref[...] = z[:, off:off + n]
        off += n


def _norm_proj(x, g, w, splits, tm):
    rows, d = x.shape
    n = w.shape[1]
    tm = min(tm, rows)
    return pl.pallas_call(
        functools.partial(_norm_proj_kernel, splits=splits),
        grid=(rows // tm,),
        in_specs=[pl.BlockSpec((tm, d), lambda i: (i, 0)), _spec(g), _spec(w)],
        out_specs=[pl.BlockSpec((tm, s), lambda i: (i, 0)) for s in splits],
        out_shape=[jax.ShapeDtypeStruct((rows, s), F32) for s in splits],
        compiler_params=_params(),
        name="norm_proj",
    )(x, _arr(g), _arr(w))


def _rope_tables(pos):
    half = HEAD_DIM // 2
    inv = ROPE_THETA ** (-jnp.arange(half, dtype=F32) / half)
    ang = pos.astype(F32)[:, None] * inv[None, :]
    cos, sin = jnp.cos(ang), jnp.sin(ang)
    cos2 = jnp.concatenate([cos, cos], axis=-1)
    sin2 = jnp.concatenate([-sin, sin], axis=-1)
    return jnp.tile(cos2, (1, 2)), jnp.tile(sin2, (1, 2))


def _rope(x, cos, sin):
    width = x.shape[1]
    half = HEAD_DIM // 2
    lane = lax.broadcasted_iota(jnp.int32, x.shape, 1)
    upper = pltpu.roll(x, width - half, 1)
    lower = pltpu.roll(x, half, 1)
    rot = jnp.where((lane % HEAD_DIM) < half, upper, lower)
    return x * cos + rot * sin


def _tile_lanes(t, reps):
    return jnp.concatenate([t] * reps, axis=1)


def _attn_prompt_kernel(cur_ref, prev_ref, cos_ref, sin_ref, cosp_ref, sinp_ref, sink_ref,
                        o_ref, k_out_ref, v_out_ref):
    w = prev_ref.shape[0]
    n_blk = cur_ref.shape[0] // w
    cur = cur_ref[...]
    prev = prev_ref[...]
    cos, sin = cos_ref[...], sin_ref[...]
    q = (_rope(cur[:, :A_Q], _tile_lanes(cos, A_HEADS // 2), _tile_lanes(sin, A_HEADS // 2))
         * (HEAD_DIM ** -0.5)).astype(BF16)
    k = _rope(cur[:, A_Q:A_Q + A_KV], cos, sin)
    v = cur[:, A_Q + A_KV:A_PROJ]
    kp = _rope(prev[:, A_Q:A_Q + A_KV], cosp_ref[...], sinp_ref[...])
    vp = prev[:, A_Q + A_KV:A_PROJ]
    k_out_ref[0] = k[(n_blk - 1) * w:, :]
    v_out_ref[0] = v[(n_blk - 1) * w:, :]

    kk_all = jnp.concatenate([kp, k], axis=0).astype(BF16)
    vv_all = jnp.concatenate([vp, v], axis=0).astype(BF16)

    items = [(j, kv) for j in range(n_blk) for kv in range(A_KV_HEADS)]
    kv_lanes = lambda kv: slice(kv * HEAD_DIM, (kv + 1) * HEAD_DIM)
    heads = lambda kv: range(kv * A_GROUP, (kv + 1) * A_GROUP)
    kj = lax.broadcasted_iota(jnp.int32, (2 * w, A_GROUP * w), 0)
    qi = lax.broadcasted_iota(jnp.int32, (2 * w, A_GROUP * w), 1) % w
    diff = w + qi - kj
    band = (diff >= 0) & (diff <= WINDOW)
    first_band = band & ((pl.program_id(1) > 0) | (kj >= w))
    sink = [jnp.concatenate([jnp.broadcast_to(sink_ref[:, h:h + 1], (1, w)) for h in heads(kv)], axis=1)
            for j, kv in items]
    qs = [jnp.concatenate([q[j * w:(j + 1) * w, h * HEAD_DIM:(h + 1) * HEAD_DIM] for h in heads(kv)], axis=0)
          for j, kv in items]
    s = [lax.dot_general(kk_all[j * w:(j + 2) * w, kv_lanes(kv)], x, NT_DIMS, preferred_element_type=F32)
         for (j, kv), x in zip(items, qs)]
    s = [jnp.where(first_band if j == 0 else band, x, NEG) for (j, kv), x in zip(items, s)]
    m = [jnp.maximum(jnp.max(x, axis=0, keepdims=True), sk) for x, sk in zip(s, sink)]
    p = [jnp.exp(x - mx) for x, mx in zip(s, m)]
    den = [jnp.sum(x, axis=0, keepdims=True) + jnp.exp(sk - mx) for x, sk, mx in zip(p, sink, m)]
    o = [lax.dot_general(vv_all[j * w:(j + 2) * w, kv_lanes(kv)], x.astype(BF16), TN_DIMS,
                         preferred_element_type=F32) for (j, kv), x in zip(items, p)]
    o = [(x / dn).T for x, dn in zip(o, den)]
    o_ref[...] = jnp.concatenate(
        [jnp.concatenate([o[j * A_KV_HEADS + kv][g * w:(g + 1) * w] for kv in range(A_KV_HEADS)
                          for g in range(A_GROUP)], axis=1) for j in range(n_blk)], axis=0).astype(o_ref.dtype)


def _attn_prompt(za, cos, sin, sinks, n_seq, t):
    w = min(WINDOW, t)
    n_blk = ATTN_BLOCKS_PER_STEP if t % (w * ATTN_BLOCKS_PER_STEP) == 0 else 1
    tm = n_blk * w
    nt = t // tm
    cur_map = lambda b, i: (b * nt + i, 0)
    prev_map = lambda b, i: ((b * nt + i) * n_blk - jnp.minimum(i, 1), 0)
    tab_cur = lambda b, i: (i, 0)
    tab_prev = lambda b, i: (i * n_blk - jnp.minimum(i, 1), 0)
    return pl.pallas_call(
        _attn_prompt_kernel,
        grid=(n_seq, nt),
        in_specs=[pl.BlockSpec((tm, A_PROJ), cur_map), pl.BlockSpec((w, A_PROJ), prev_map),
                  pl.BlockSpec((tm, A_KV), tab_cur), pl.BlockSpec((tm, A_KV), tab_cur),
                  pl.BlockSpec((w, A_KV), tab_prev), pl.BlockSpec((w, A_KV), tab_prev),
                  _spec(sinks)],
        out_specs=[pl.BlockSpec((tm, A_Q), cur_map),
                   pl.BlockSpec((1, w, A_KV), lambda b, i: (b, 0, 0)),
                   pl.BlockSpec((1, w, A_KV), lambda b, i: (b, 0, 0))],
        out_shape=[jax.ShapeDtypeStruct((n_seq * t, A_Q), BF16),
                   jax.ShapeDtypeStruct((n_seq, w, A_KV), F32),
                   jax.ShapeDtypeStruct((n_seq, w, A_KV), F32)],
        compiler_params=_params(2),
        name="attn_prompt",
    )(za, za, cos, sin, cos, sin, _arr(sinks))


def _attn_sample_kernel(za_ref, ck_ref, cv_ref, cos_ref, sin_ref, sink_ref, o_ref, kn_ref, vn_ref):
    sb, wb = ck_ref.shape[0], ck_ref.shape[2]
    za = za_ref[...]
    cos, sin = cos_ref[...], sin_ref[...]
    q = _rope(za[:, :A_Q], _tile_lanes(cos, A_HEADS // 2), _tile_lanes(sin, A_HEADS // 2))
    k = _rope(za[:, A_Q:A_Q + A_KV], cos, sin)
    v = za[:, A_Q + A_KV:A_PROJ]
    kn_ref[...] = k
    vn_ref[...] = v
    rows = A_GROUP * sb
    row_seq = lax.broadcasted_iota(jnp.int32, (rows, sb * wb), 0) % sb
    col = lax.broadcasted_iota(jnp.int32, (rows, sb * wb), 1)
    own = (col >= row_seq * wb) & (col < (row_seq + 1) * wb)
    kvs = range(A_KV_HEADS)
    sl = [slice(kv * HEAD_DIM, (kv + 1) * HEAD_DIM) for kv in kvs]
    heads = [[kv * A_GROUP + g for g in range(A_GROUP)] for kv in kvs]
    kt = [jnp.concatenate([ck_ref[b, sl[kv], :] for b in range(sb)], axis=1).astype(BF16) for kv in kvs]
    vt = [jnp.concatenate([cv_ref[b, sl[kv], :] for b in range(sb)], axis=1).astype(BF16) for kv in kvs]
    qs =[jnp.concatenate([q[:, h * HEAD_DIM:(h + 1) * HEAD_DIM] for h in heads[kv]], axis=0) * (HEAD_DIM ** -0.5)
          for kv in kvs]
    sink = [jnp.concatenate([jnp.broadcast_to(sink_ref[:, h:h + 1], (sb, 1)) for h in heads[kv]], axis=0)
            for kv in kvs]
    s = [jnp.dot(qs[kv].astype(BF16), kt[kv], preferred_element_type=F32) for kv in kvs]
    s = [jnp.where(own, x, NEG) for x in s]
    k_rep = [jnp.concatenate([k[:, sl[kv]]] * A_GROUP, axis=0) for kv in kvs]
    v_rep = [jnp.concatenate([v[:, sl[kv]]] * A_GROUP, axis=0) for kv in kvs]
    s_new = [jnp.sum(qs[kv] * k_rep[kv], axis=-1, keepdims=True) for kv in kvs]
    m = [jnp.maximum(jnp.maximum(jnp.max(s[kv], axis=-1, keepdims=True), s_new[kv]), sink[kv]) for kv in kvs]
    p = [jnp.exp(s[kv] - m[kv]) for kv in kvs]
    p_new = [jnp.exp(s_new[kv] - m[kv]) for kv in kvs]
    den = [jnp.sum(p[kv], axis=-1, keepdims=True) + p_new[kv] + jnp.exp(sink[kv] - m[kv]) for kv in kvs]
    o = [lax.dot_general(p[kv].astype(BF16), vt[kv], NT_DIMS, preferred_element_type=F32) for kv in kvs]
    o = [(o[kv] + p_new[kv] * v_rep[kv]) / den[kv] for kv in kvs]
    o_ref[...] = jnp.concatenate([o[kv][g * sb:(g + 1) * sb] for kv in kvs for g in range(A_GROUP)],
                                 axis=1).astype(o_ref.dtype)


def _attn_sample(za, cache_kt, cache_vt, layer, cos, sin, sinks):
    n, wb = cache_kt.shape[1], cache_kt.shape[3]
    sb = min(16, n)
    cache_spec = pl.BlockSpec((None, sb, A_KV, wb), lambda i: (layer, i, 0, 0))
    return pl.pallas_call(
        _attn_sample_kernel,
        grid=(n // sb,),
        in_specs=[pl.BlockSpec((sb, A_PROJ), lambda i: (i, 0)), cache_spec, cache_spec,
                  _const_spec((1, A_KV)), _const_spec((1, A_KV)), _spec(sinks)],
        out_specs=[pl.BlockSpec((sb, A_Q), lambda i: (i, 0)),
                   pl.BlockSpec((sb, A_KV), lambda i: (i, 0)),
                   pl.BlockSpec((sb, A_KV), lambda i: (i, 0))],
        out_shape=[jax.ShapeDtypeStruct((n, A_Q), BF16),
                   jax.ShapeDtypeStruct((n, A_KV), F32),
                   jax.ShapeDtypeStruct((n, A_KV), F32)],
        compiler_params=_params(),
        name="attn_sample",
    )(za, cache_kt, cache_vt, cos, sin, _arr(sinks))


RWKV_PARAM_NAMES = ("mu", "w0", "w2", "a0", "a2", "g2", "k_k", "k_a", "r_k", "gn_g", "gn_b", "seg")


def _rwkv_param_specs(p):
    return [_spec(p[name]) for name in RWKV_PARAM_NAMES]


def _rwkv_param_args(p):
    return [_arr(p[name]) for name in RWKV_PARAM_NAMES]


def _rwkv_prep(zb, prev, p):
    o1, o2, o3 = B_WIDTH, 2 * B_WIDTH, 3 * B_WIDTH
    o4, o5 = o3 + DECAY_RANK, o3 + DECAY_RANK + ICLR_RANK
    zs = zb + (prev - zb) * p["mu"]
    r, k, v = zs[:, :o1], zs[:, o1:o2], zs[:, o2:o3]
    wd, ad, gd = zs[:, o3:o4], zs[:, o4:o5], zs[:, o5:]
    lw = -DECAY_SCALE * jax.nn.sigmoid(p["w0"] + _bdot(jnp.tanh(wd), p["w2"]))
    a = jax.nn.sigmoid(p["a0"] + _bdot(ad, p["a2"]))
    g = _bdot(jax.nn.sigmoid(gd), p["g2"])
    kk = k * p["k_k"]
    kk = kk / jnp.maximum(jnp.sqrt(_segsum(kk * kk, p["seg"])), 1e-12)
    k = k * (1.0 + (a - 1.0) * p["k_a"])
    return r, lw, k, v, kk, kk * a, g


def _split2(x):
    hi = x.astype(BF16)
    return hi, (x - hi.astype(F32)).astype(BF16)


def _segsum(x, seg):
    outs = []
    for g0 in range(0, x.shape[1], MXU_LANES):
        hi, lo = _split2(x[:, g0:g0 + MXU_LANES])
        outs.append(jnp.dot(hi, seg, preferred_element_type=F32) + jnp.dot(lo, seg, preferred_element_type=F32))
    return jnp.concatenate(outs, axis=1)


def _rwkv_finish(o, r, k, v, g, p):
    seg = p["seg"]
    mean = _segsum(o, seg) * (1.0 / B_HEAD_DIM)
    d = o - mean
    var = _segsum(d * d, seg) * (1.0 / B_HEAD_DIM)
    o = d * lax.rsqrt(var + GN_EPS) * p["gn_g"] + p["gn_b"]
    bonus = _segsum(r * k * p["r_k"], seg) * v
    return (o + bonus) * g


def _split3(x):
    hi = x.astype(BF16)
    r1 = x - hi.astype(F32)
    mid = r1.astype(BF16)
    return hi, mid, (r1 - mid.astype(F32)).astype(BF16)


def _mm(a, b):
    return jnp.dot(a.astype(BF16), b.astype(BF16), preferred_element_type=F32)


def _head_block_diag(x, bd_mask):
    tiled = jnp.concatenate([x.astype(BF16)] * HEADS_PER_GROUP, axis=0)
    return jnp.where(bd_mask, tiled, jnp.zeros_like(tiled))


def _head_block_diag_t(x, bd_mask):
    xt = x.astype(F32).T
    tiled = jnp.concatenate([xt.astype(BF16)] * HEADS_PER_GROUP, axis=1)
    return jnp.where(bd_mask, tiled, jnp.zeros_like(tiled))


def _rwkv_chunk_kernel(zb_ref, *refs, n_sub):
    n_p = len(RWKV_PARAM_NAMES)
    p = {name: ref[...] for name, ref in zip(RWKV_PARAM_NAMES, refs[:n_p])}
    o_ref, s_out_ref, carry_ref, state_ref = refs[n_p:]
    n_rows = zb_ref.shape[0]
    c = n_rows // n_sub
    d = B_HEAD_DIM

    @pl.when(pl.program_id(1) == 0)
    def _():
        carry_ref[...] = jnp.zeros_like(carry_ref)
        state_ref[...] = jnp.zeros_like(state_ref)

    zb = zb_ref[...]
    rows = lax.broadcasted_iota(jnp.int32, zb.shape, 0)
    prev = jnp.where(rows == 0, carry_ref[0:1, :], pltpu.roll(zb, 1, 0))
    carry_ref[0:1, :] = zb[n_rows - 1:n_rows, :]
    r, lw, k, v, kk, b, g = _rwkv_prep(zb, prev, p)

    ti = lax.broadcasted_iota(jnp.int32, (c, MXU_LANES), 0)
    si = lax.broadcasted_iota(jnp.int32, (c, MXU_LANES), 1) % d
    lane_head = lax.broadcasted_iota(jnp.int32, (c, MXU_LANES), 1) // d
    incl, strict = ti >= si, ti > si
    eye = (ti == si).astype(F32)
    bd_mask = (lax.broadcasted_iota(jnp.int32, (MXU_LANES, MXU_LANES), 0) // c
               == lax.broadcasted_iota(jnp.int32, (MXU_LANES, MXU_LANES), 1) // d)
    tri = (lax.broadcasted_iota(jnp.int32, (c, c), 0) >= lax.broadcasted_iota(jnp.int32, (c, c), 1)).astype(BF16)
    bd = functools.partial(_head_block_diag, bd_mask=bd_mask)
    bdt = functools.partial(_head_block_diag_t, bd_mask=bd_mask)

    items = []
    for sub in range(n_sub):
        rs = slice(sub * c, (sub + 1) * c)
        lw_c = lw[rs]
        gcum = sum(jnp.dot(tri, part, preferred_element_type=F32) for part in _split3(lw_c))
        g_last = gcum[c - 1:c, :]
        e_neg = jnp.exp(-gcum)
        e_hat = jnp.exp(g_last - gcum)
        kap = (kk[rs] * jnp.exp(gcum - lw_c)).astype(BF16)
        rt = r[rs] * jnp.exp(gcum)
        kt, bt = k[rs] * e_neg, b[rs] * e_neg
        khat, bhat = (k[rs] * e_hat).astype(BF16), (b[rs] * e_hat).astype(BF16)
        dec_last = jnp.exp(g_last)
        v_c = v[rs].astype(BF16)
        for g0 in range(0, B_WIDTH, MXU_LANES):
            gl = slice(g0, g0 + MXU_LANES)
            items.append(dict(g0=g0, kap=kap[:, gl], rt=rt[:, gl], kt=kt[:, gl], bt=bt[:, gl], khat=khat[:, gl],
                              bhat=bhat[:, gl], v=v_c[:, gl], dec=dec_last[:, gl]))
    for it in items:
        lhs = jnp.concatenate([it["kap"], it["rt"].astype(BF16)], axis=0)
        it["m_k"] = _mm(lhs, bdt(it["kt"]))
        it["m_b"] = _mm(lhs, bdt(it["bt"]))
    for it in items:
        it["a_k"] = jnp.where(strict, it["m_k"][:c], 0.0)
        it["p_k"] = jnp.where(incl, it["m_k"][c:], 0.0)
        it["pw"] = -jnp.where(strict, it["m_b"][:c], 0.0)
        it["y"] = jnp.where(incl, it["m_b"][c:], 0.0)
        it["t_inv"] = None
    span = 1
    while span < c:
        last = 2 * span >= c
        for it in items:
            rows = ([] if last else [it["pw"]]) + ([] if it["t_inv"] is None else [it["t_inv"]]) + [it["y"]]
            out = _mm(jnp.concatenate(rows, axis=0), bd(it["pw"]))
            off = 0 if last else c
            if it["t_inv"] is None:
                it["t_inv"] = eye + it["pw"]
            else:
                it["t_inv"] = it["t_inv"] + out[off:off + c]
                off += c
            it["y"] = it["y"] + out[off:off + c]
            if not last:
                it["pw"] = out[:c]
        span *= 2
    for it in items:
        out = _mm(jnp.concatenate([it["a_k"], it["p_k"]], axis=0), bd(it["v"]))
        it["akv"], it["pkv"] = out[:c], out[c:]
    for it in items:
        ty = jnp.concatenate([it["t_inv"], it["y"]], axis=0).astype(BF16)
        out_k = _mm(ty, bd(it["kap"]))
        out_u = _mm(ty, bd(it["akv"]))
        it["k_p"], it["u0"] = out_k[:c], out_u[:c]
        it["r_p"] = (it["rt"] - out_k[c:]).astype(BF16)
        it["o0"] = it["pkv"] - out_u[c:]
    for it in items:
        it["w_bd"] = jnp.where(bd_mask, lax.dot_general(it["k_p"].astype(BF16), it["bhat"], TN_DIMS,
                                                        preferred_element_type=F32), 0.0).astype(BF16)
        s1_full = lax.dot_general(jnp.concatenate([it["v"], it["u0"].astype(BF16)], axis=0),
                                  jnp.concatenate([it["khat"], -it["bhat"]], axis=0),
                                  TN_DIMS, preferred_element_type=F32)
        it["s1"] = sum(jnp.where(lane_head == h, s1_full[h * d:(h + 1) * d], 0.0) for h in range(HEADS_PER_GROUP))
    state = {g0: state_ref[:, g0:g0 + MXU_LANES] for g0 in range(0, B_WIDTH, MXU_LANES)}
    o_parts = []
    for it in items:
        s0 = state[it["g0"]]
        o_parts.append(_mm(it["r_p"], bdt(s0)) + it["o0"])
        state[it["g0"]] = s0 * it["dec"] - _mm(s0, it["w_bd"]) + it["s1"]
    for g0, s_new in state.items():
        state_ref[:, g0:g0 + MXU_LANES] = s_new
    n_groups = B_WIDTH // MXU_LANES
    o = jnp.concatenate([jnp.concatenate(o_parts[i:i + n_groups], axis=1)
                         for i in range(0, len(o_parts), n_groups)], axis=0)
    o_ref[...] = _rwkv_finish(o, r, k, v, g, p).astype(o_ref.dtype)
    for h in range(B_HEADS):
        s_out_ref[0, h] = state_ref[:, h * d:(h + 1) * d]


def _rwkv_prompt(zb, p, n_seq, t):
    c = WKV_CHUNK
    assert c == B_HEAD_DIM and t % c == 0, "head packing puts WKV_CHUNK time steps where a head's lanes go"
    n_sub = WKV_CHUNKS_PER_STEP if t % (c * WKV_CHUNKS_PER_STEP) == 0 else 1
    tm = c * n_sub
    nc = t // tm
    width = zb.shape[1]
    return pl.pallas_call(
        functools.partial(_rwkv_chunk_kernel, n_sub=n_sub),
        grid=(n_seq, nc),
        in_specs=[pl.BlockSpec((tm, width), lambda b, i: (b * nc + i, 0))] + _rwkv_param_specs(p),
        out_specs=[pl.BlockSpec((tm, B_WIDTH), lambda b, i: (b * nc + i, 0)),
                   pl.BlockSpec((1, B_HEADS, B_HEAD_DIM, B_HEAD_DIM), lambda b, i: (b, 0, 0, 0))],
        out_shape=[jax.ShapeDtypeStruct((n_seq * t, B_WIDTH), BF16),
                   jax.ShapeDtypeStruct((n_seq, B_HEADS, B_HEAD_DIM, B_HEAD_DIM), F32)],
        scratch_shapes=[pltpu.VMEM((8, width), F32),
                        pltpu.VMEM((B_HEAD_DIM, B_WIDTH), F32)],
        compiler_params=_params(2),
        name="rwkv_prompt",
    )(zb, *_rwkv_param_args(p))


def _rwkv_sample_kernel(zb_ref, shift_ref, s_ref, s_all_ref, *refs, heads_per_step):
    del s_all_ref
    n_p = len(RWKV_PARAM_NAMES)
    p_refs = refs[:n_p]
    o_ref, s_out_ref, vec_ref, keep_ref, ot_ref = refs[n_p:]
    d = B_HEAD_DIM
    step = pl.program_id(0)

    @pl.when(step == 0)
    def _():
        p = {name: ref[...] for name, ref in zip(RWKV_PARAM_NAMES, p_refs)}
        r, lw, k, v, kk, b, g = _rwkv_prep(zb_ref[...], shift_ref[...], p)
        for idx, val in enumerate((r, jnp.exp(lw), k, v, kk, b)):
            vec_ref[idx] = val.T
        for idx, val in enumerate((r, k, v, g)):
            keep_ref[idx] = val

    for hh in range(heads_per_step):
        base = pl.multiple_of((step * heads_per_step + hh) * d, d)
        r_h, w_h, k_h, kk_h, b_h = (vec_ref[idx, pl.ds(base, d), :] for idx in (0, 1, 2, 4, 5))

        def value_row(i, carry, hh=hh, base=base, r_h=r_h, w_h=w_h, k_h=k_h, kk_h=kk_h, b_h=b_h):
            rows = pl.ds(pl.multiple_of(hh * d * d + i * d, d), d)
            s_i = s_ref[rows, :]
            sa = jnp.sum(s_i * kk_h, axis=0, keepdims=True)
            s_i = s_i * w_h - sa * b_h + vec_ref[3, pl.ds(base + i, 1), :] * k_h
            s_out_ref[rows, :] = s_i
            ot_ref[pl.ds(base + i, 1), :] = jnp.sum(s_i * r_h, axis=0, keepdims=True)
            return carry

        lax.fori_loop(0, d, value_row, 0, unroll=4)

    @pl.when(step == pl.num_programs(0) - 1)
    def _():
        p = {name: ref[...] for name, ref in zip(RWKV_PARAM_NAMES, p_refs)}
        o_ref[...] = _rwkv_finish(ot_ref[...].T, keep_ref[0], keep_ref[1], keep_ref[2], keep_ref[3],
                                  p).astype(o_ref.dtype)


def _rwkv_sample(zb, shift_all, state_t, s_all, layer, p):
    n, width = zb.shape
    d = B_HEAD_DIM
    hps = WKV_SAMPLE_HEADS_PER_STEP
    state_spec = pl.BlockSpec((None, hps * d * d, n), lambda i: (layer, i, 0))
    return pl.pallas_call(
        functools.partial(_rwkv_sample_kernel, heads_per_step=hps),
        grid=(B_HEADS // hps,),
        in_specs=[_const_spec((n, width)), pl.BlockSpec((None, n, width), lambda i: (layer, 0, 0)),
                  state_spec, pl.BlockSpec(memory_space=pl.ANY)] + _rwkv_param_specs(p),
        out_specs=[_const_spec((n, B_WIDTH)), state_spec],
        out_shape=[jax.ShapeDtypeStruct((n, B_WIDTH), BF16), jax.ShapeDtypeStruct(s_all.shape, F32)],
        input_output_aliases={3: 1},
        scratch_shapes=[pltpu.VMEM((6, B_WIDTH, n), F32), pltpu.VMEM((4, n, B_WIDTH), F32),
                        pltpu.VMEM((B_WIDTH, n), F32)],
        compiler_params=_params(),
        name="rwkv_sample",
    )(zb, shift_all, state_t, s_all, *_rwkv_param_args(p))


def _gelu_erf(x):
    return 0.5 * x * (1.0 + lax.erf(x * 0.7071067811865476))


def _gelu_tanh(x):
    c = 0.7978845608028654
    return x * (0.5 + 0.5 * jnp.tanh(x * (c + (0.044715 * c) * (x * x))))


def _layernorm(x, g, b):
    mu = jnp.mean(x, axis=-1, keepdims=True)
    d = x - mu
    var = jnp.mean(d * d, axis=-1, keepdims=True)
    return d * lax.rsqrt(var + LN_EPS) * g + b


def _gmlp_kernel(h_ref, gpre_ref, win_ref, lng_ref, lnb_ref, ws_ref, bs_ref, wout_ref, gpost_ref,
                 o_ref, *v_refs, chunk):
    x = h_ref[...]
    tm = x.shape[0]
    width = wout_ref.shape[0]
    hd = width // C_HEADS
    xn = _rmsnorm(x, gpre_ref[...])
    z = _gelu_erf(jnp.dot(xn.astype(BF16), win_ref[...], preferred_element_type=F32))
    u = z[:, :width]
    v = _layernorm(z[:, width:], lng_ref[...], lnb_ref[...])
    if chunk == 1:
        mixed = v * ws_ref[...] + bs_ref[...]
        v_refs[0][...] = v
    else:
        ri = lax.broadcasted_iota(jnp.int32, (chunk, chunk), 0)
        ci = lax.broadcasted_iota(jnp.int32, (chunk, chunk), 1)
        vb = v.astype(BF16)
        bias = bs_ref[...]
        cols = []
        for hh in range(C_HEADS):
            wtri = jnp.where(ri >= ci, ws_ref[hh], 0.0).astype(BF16)
            parts = [jnp.dot(wtri, vb[c0:c0 + chunk, hh * hd:(hh + 1) * hd], preferred_element_type=F32)
                     + bias[:, hh:hh + 1] for c0 in range(0, tm, chunk)]
            cols.append(jnp.concatenate(parts, axis=0) if len(parts) > 1 else parts[0])
        mixed = jnp.concatenate(cols, axis=1)
    y = u * mixed
    out = jnp.dot(y.astype(BF16), wout_ref[...], preferred_element_type=F32)
    o_ref[...] = x + _rmsnorm(out, gpost_ref[...])


def _gmlp(h, gpre, win, lng, lnb, ws, bs, wout, gpost, chunk, tm):
    rows, d = h.shape
    tm = min(tm, rows)
    width = wout.shape[0]
    out_specs = [pl.BlockSpec((tm, d), lambda i: (i, 0))]
    out_shape = [jax.ShapeDtypeStruct((rows, d), F32)]
    if chunk == 1:
        out_specs.append(pl.BlockSpec((tm, width), lambda i: (i, 0)))
        out_shape.append(jax.ShapeDtypeStruct((rows, width), F32))
    return pl.pallas_call(
        functools.partial(_gmlp_kernel, chunk=chunk),
        grid=(rows // tm,),
        in_specs=[pl.BlockSpec((tm, d), lambda i: (i, 0))]
        + [_spec(x) for x in (gpre, win, lng, lnb, ws, bs, wout, gpost)],
        out_specs=out_specs,
        out_shape=out_shape,
        compiler_params=_params(),
        name="gmlp",
    )(h, *[_arr(x) for x in (gpre, win, lng, lnb, ws, bs, wout, gpost)])


def _ffn_kernel(h_ref, *refs, carried, tiles_per_seq, mixed):
    x = h_ref[...]
    if mixed:
        a_ref, b_ref, wmix_ref, gmix_ref = refs[:4]
        refs = refs[4:]
        ka = a_ref.shape[1]
        acc = jnp.dot(a_ref[...], wmix_ref[:ka, :], preferred_element_type=F32)
        acc = acc + jnp.dot(b_ref[...], wmix_ref[ka:, :], preferred_element_type=F32)
        x = x + _rmsnorm(acc, gmix_ref[...])
    gpre_ref, wg_ref, wu_ref, cw_ref, cb_ref, wd_ref, gpost_ref = refs[:7]
    refs = refs[7:]
    tm = x.shape[0]
    f = wg_ref.shape[1]
    xn = _rmsnorm(x, gpre_ref[...]).astype(BF16)
    if carried:
        o_ref, tail_ref, carry_ref = refs

        @pl.when(pl.program_id(0) % tiles_per_seq == 0)
        def _():
            carry_ref[...] = jnp.zeros_like(carry_ref)
    else:
        s2_ref, s1_ref, o_ref, gp_ref = refs
    cw = cw_ref[...]
    cb = cb_ref[...]

    def in_proj(c0, c1):
        return (jnp.dot(xn, wg_ref[:, c0:c1], preferred_element_type=F32),
                jnp.dot(xn, wu_ref[:, c0:c1], preferred_element_type=F32))

    bounds = [(c0, min(c0 + FFN_COL_BLOCK, f)) for c0 in range(0, f, FFN_COL_BLOCK)]
    nxt = in_proj(*bounds[0])
    out = None
    for idx, (c0, c1) in enumerate(bounds):
        gp, up = nxt
        if idx + 1 < len(bounds):
            nxt = in_proj(*bounds[idx + 1])
        if carried:
            rows = lax.broadcasted_iota(jnp.int32, gp.shape, 0)
            t2, t1 = carry_ref[0:1, c0:c1], carry_ref[1:2, c0:c1]
            x1 = jnp.where(rows == 0, t1, pltpu.roll(gp, 1, 0))
            x2 = jnp.where(rows == 0, t2, jnp.where(rows == 1, t1, pltpu.roll(gp, 2, 0)))
            carry_ref[0:2, c0:c1] = gp[tm - 2:tm, :]
            tail_ref[0, :, c0:c1] = gp[tm - 2:tm, :]
        else:
            x2, x1 = s2_ref[:, c0:c1], s1_ref[:, c0:c1]
            gp_ref[:, c0:c1] = gp
        conv = cb[:, c0:c1] + cw[2:3, c0:c1] * gp + cw[0:1, c0:c1] * x2 + cw[1:2, c0:c1] * x1
        hid = (_gelu_tanh(conv) * up).astype(BF16)
        part = jnp.dot(hid, wd_ref[c0:c1, :], preferred_element_type=F32)
        out = part if out is None else out + part
    o_ref[...] = x + _rmsnorm(out, gpost_ref[...])


def _ffn(h, gpre, wg, wu, cw, cb, wd, gpost, tm, seq_len=None, state=None, mix=None):
    rows, d = h.shape
    f = wg.shape[1]
    carried = state is None
    tm = min(tm, rows, seq_len) if carried else min(tm, rows)
    row_spec = pl.BlockSpec((tm, d), lambda i: (i, 0))
    params = (gpre, wg, wu, cw, cb, wd, gpost)
    in_specs, args = [row_spec], [h]
    if mix is not None:
        a, b, w_mix, g_mix = mix
        in_specs += [pl.BlockSpec((tm, a.shape[1]), lambda i: (i, 0)), pl.BlockSpec((tm, b.shape[1]), lambda i: (i, 0)),
                     _spec(w_mix), _spec(g_mix)]
        args += [a, b, _arr(w_mix), _arr(g_mix)]
    in_specs += [_spec(x) for x in params]
    args += [_arr(x) for x in params]
    if carried:
        tiles_per_seq = seq_len // tm
        out_specs = [row_spec, pl.BlockSpec((1, 2, f), lambda i: (i // tiles_per_seq, 0, 0))]
        out_shape = [jax.ShapeDtypeStruct((rows, d), F32),
                     jax.ShapeDtypeStruct((rows // seq_len, 2, f), F32)]
        scratch = [pltpu.VMEM((8, f), F32)]
    else:
        tiles_per_seq = 1
        in_specs += [pl.BlockSpec((None, None, tm, f), lambda i, tap=tap: (state.layer, tap, i, 0)) for tap in (0, 1)]
        args += [state.arr, state.arr]
        out_specs = [row_spec, pl.BlockSpec((tm, f), lambda i: (i, 0))]
        out_shape = [jax.ShapeDtypeStruct((rows, d), F32), jax.ShapeDtypeStruct((rows, f), F32)]
        scratch = []
    return pl.pallas_call(
        functools.partial(_ffn_kernel, carried=carried, tiles_per_seq=tiles_per_seq, mixed=mix is not None),
        grid=(rows // tm,),
        in_specs=in_specs,
        out_specs=out_specs,
        out_shape=out_shape,
        scratch_shapes=scratch,
        compiler_params=_params(),
        name="conv_ffn",
    )(*args)


def kernel(x_prompt, x_sample, cache_win_k, cache_win_v, state_wkv, state_shift, state_ffn_conv,
           norm_mix_pre, norm_mix_post, norm_ffn_pre, norm_ffn_post,
           w_in_even, attn_sinks, shift_mu, decay_w0, decay_w2, iclr_a0, iclr_a2, gate_g2,
           key_k, key_a, bonus_r_k, gn_gain, gn_bias, w_out_even,
           w_in_odd, sgu_ln_gain, sgu_ln_bias, sgu_w, sgu_b, w_out_odd,
           ffn_w_gate, ffn_w_up, ffn_conv_w, ffn_conv_b, ffn_w_down):
    n_seq, t, d = x_prompt.shape
    n_dec, dec_t, _ = x_sample.shape
    assert dec_t == 1, "the sample path is written for one new token per sequence"
    depth = norm_mix_pre.shape[0]
    n_even = w_in_even.shape[0]
    wp = min(WINDOW, t)
    wb = cache_win_k.shape[2]
    hd = B_HEAD_DIM
    chunk = min(t, CHUNK)
    assert chunk == sgu_w.shape[-1], "the prompt is mixed in whole CHUNK-position chunks"

    hp = x_prompt.reshape(n_seq * t, d)
    hs = x_sample.reshape(n_dec, d)
    cos_p, sin_p = _rope_tables(jnp.arange(t, dtype=jnp.int32))
    cos_s, sin_s = _rope_tables(PAST_LEN + jnp.arange(1, dtype=jnp.int32))
    seg = jnp.kron(jnp.eye(HEADS_PER_GROUP, dtype=F32), jnp.ones((hd, hd), F32)).astype(BF16)

    rows3 = lambda a: a.reshape(a.shape[0], 1, -1)
    bf = lambda a: a.astype(BF16)
    stacked = dict(
        g_pre=rows3(norm_mix_pre), g_post=rows3(norm_mix_post), f_pre=rows3(norm_ffn_pre),
        f_post=rows3(norm_ffn_post), w_in_even=bf(w_in_even), w_out_even=bf(w_out_even),
        sinks=rows3(attn_sinks), mu=rows3(shift_mu), w0=rows3(decay_w0), w2=bf(decay_w2), a0=rows3(iclr_a0),
        a2=bf(iclr_a2), g2=bf(gate_g2), k_k=rows3(key_k), k_a=rows3(key_a), r_k=rows3(bonus_r_k),
        gn_g=rows3(gn_gain), gn_b=rows3(gn_bias), w_in_odd=bf(w_in_odd), w_out_odd=bf(w_out_odd),
        ln_g=rows3(sgu_ln_gain), ln_b=rows3(sgu_ln_bias), sgu_w=sgu_w,
        sgu_bt=jnp.swapaxes(sgu_b, 1, 2),
        sgu_w0=rows3(jnp.repeat(sgu_w[:, :, 0, 0], sgu_ln_gain.shape[1] // C_HEADS, axis=1)),
        sgu_b0=rows3(jnp.repeat(sgu_b[:, :, 0], sgu_ln_gain.shape[1] // C_HEADS, axis=1)),
        wg=bf(ffn_w_gate), wu=bf(ffn_w_up), wd=bf(ffn_w_down), cw=ffn_conv_w, cb=rows3(ffn_conv_b))
    cache_kt = jnp.transpose(cache_win_k, (0, 1, 3, 4, 2)).reshape(n_even, n_dec, A_KV, wb)
    cache_vt = jnp.transpose(cache_win_v, (0, 1, 3, 4, 2)).reshape(n_even, n_dec, A_KV, wb)
    wkv_t = jnp.transpose(state_wkv, (0, 2, 3, 4, 1)).reshape(n_even, B_HEADS * hd * hd, n_dec)
    wkv_new_t = jnp.zeros_like(wkv_t)
    conv_taps = jnp.swapaxes(state_ffn_conv, 1, 2)

    kp_l, vp_l, ks_l, vs_l, sp_l, shp_l, shs_l, vsgu_l, cp_l, cs_l = ([] for _ in range(10))
    for layer in range(depth):
        j = layer // 2
        at = lambda name, idx: _Layer(stacked[name], idx)
        g_pre, g_post = at("g_pre", layer), at("g_post", layer)
        if layer % 2 == 0:
            w_in, w_out, sinks = at("w_in_even", j), at("w_out_even", j), at("sinks", j)
            p = {name: at(name, j) for name in RWKV_PARAM_NAMES if name != "seg"}
            p["seg"] = seg
            splits = (A_PROJ, w_in.shape[1] - A_PROJ)
            za, zb = _norm_proj(hp, g_pre, w_in, splits, tm=ROW_TILE)
            oa, kp, vp = _attn_prompt(za, cos_p, sin_p, sinks, n_seq, t)
            ob, sp = _rwkv_prompt(zb, p, n_seq, t)
            mix_p = (oa, ob, w_out, g_post)
            kp_l.append(kp.reshape(n_seq, wp, A_KV_HEADS, HEAD_DIM))
            vp_l.append(vp.reshape(n_seq, wp, A_KV_HEADS, HEAD_DIM))
            sp_l.append(sp)
            shp_l.append(zb.reshape(n_seq, t, -1)[:, -1])
            za, zb = _norm_proj(hs, g_pre, w_in, splits, tm=ROW_TILE)
            oa, kn, vn = _attn_sample(za, cache_kt, cache_vt, j, cos_s, sin_s, sinks)
            ob, wkv_new_t = _rwkv_sample(zb, state_shift, wkv_t, wkv_new_t, j, p)
            mix_s = (oa, ob, w_out, g_post)
            ks_l.append(kn.reshape(n_dec, 1, A_KV_HEADS, HEAD_DIM))
            vs_l.append(vn.reshape(n_dec, 1, A_KV_HEADS, HEAD_DIM))
            shs_l.append(zb)
        else:
            w_in, w_out = at("w_in_odd", j), at("w_out_odd", j)
            lng, lnb = at("ln_g", j), at("ln_b", j)
            hp, = _gmlp(hp, g_pre, w_in, lng, lnb, at("sgu_w", j), at("sgu_bt", j), w_out, g_post,
                        chunk=chunk, tm=GMLP_CHUNKS_PER_STEP * chunk)
            hs, vsg = _gmlp(hs, g_pre, w_in, lng, lnb, at("sgu_w0", j), at("sgu_b0", j), w_out, g_post,
                            chunk=1, tm=ROW_TILE)
            vsgu_l.append(vsg.reshape(n_dec, 1, -1))
            mix_p = mix_s = None
        ffn_params = [at(name, layer) for name in ("f_pre", "wg", "wu", "cw", "cb", "wd", "f_post")]
        hp, cp = _ffn(hp, *ffn_params, tm=FFN_ROW_TILE, seq_len=t, mix=mix_p)
        hs, gp = _ffn(hs, *ffn_params, tm=FFN_ROW_TILE, state=_Layer(conv_taps, layer), mix=mix_s)
        cp_l.append(cp)
        cs_l.append(jnp.stack([state_ffn_conv[layer][:, 1], gp], axis=1))

    wkv_sample = jnp.transpose(wkv_new_t.reshape(n_even, B_HEADS, hd, hd, n_dec), (0, 4, 1, 2, 3))
    return (hp.reshape(n_seq, t, d), hs.reshape(n_dec, 1, d),
            jnp.stack(kp_l), jnp.stack(vp_l), jnp.stack(ks_l), jnp.stack(vs_l),
            jnp.stack(sp_l), wkv_sample, jnp.stack(shp_l), jnp.stack(shs_l),
            jnp.stack(vsgu_l), jnp.stack(cp_l), jnp.stack(cs_l))
```
